```python
import jax
import jax.numpy as jnp
from jax import lax
import numpy as np

D_MODEL = 1024
BATCH = 4
SEQ = 8192
DEPTH = 2
DEC_BATCH = 8
DEC_SEQ = 8192
PAST_LEN = 128

GRID_W = 64
EPS = 1e-6
NEG = -1e30
TINY = 1e-30
HEAD_DIM = 64
ROPE_THETA = 10000.0
N_BRANCH = 4
D_FF = 4 * D_MODEL

POOL_WIDTH = D_MODEL // 4
POOL_WINDOWS = (2, 4, 8, 16)
POOL_GW = POOL_WIDTH // len(POOL_WINDOWS)

GQA_HEADS = D_MODEL // 128
GQA_KV_HEADS = GQA_HEADS // 4
GQA_WINDOW = 128
GQA_BLOCK = 128

HGRN_HEADS = D_MODEL // 256
HGRN_DK = 64
HGRN_DV = 64
HGRN_CHUNK = 16

NAT_HEADS = D_MODEL // 256
NAT_KR_MAX = 8
NAT_KC = 16
NAT_QCB = 16
NAT_KCB = NAT_QCB + NAT_KC

GQA_QW = GQA_HEADS * HEAD_DIM
GQA_KVW = GQA_KV_HEADS * HEAD_DIM
HGRN_KW = HGRN_HEADS * HGRN_DK
HGRN_VW = HGRN_HEADS * HGRN_DV
NAT_W = NAT_HEADS * HEAD_DIM
IN_SPLITS = (POOL_WIDTH, GQA_QW, GQA_KVW, GQA_KVW, HGRN_KW, HGRN_KW, HGRN_VW, HGRN_KW, HGRN_VW, NAT_W, NAT_W, NAT_W, N_BRANCH * D_MODEL)
IN_WIDTH = sum(IN_SPLITS)

kernel_name = 'hybrid_bidir_encoder_trunk'


def _split_points():
    pts, acc = [], 0
    for w in IN_SPLITS[:-1]:
        acc += w
        pts.append(acc)
    return pts


def _rmsnorm(x, g):
    xf = x.astype(jnp.float32)
    y = xf * lax.rsqrt(jnp.mean(xf * xf, axis=-1, keepdims=True) + EPS)
    return (y * g.astype(jnp.float32)).astype(x.dtype)


def _rope(x, pos):
    half = x.shape[-1] // 2
    inv = ROPE_THETA ** (-jnp.arange(half, dtype=jnp.float32) / half)
    ang = pos.astype(jnp.float32)[:, None] * inv[None, :]
    cos = jnp.cos(ang)[:, None, :]
    sin = jnp.sin(ang)[:, None, :]
    xf = x.astype(jnp.float32)
    x1, x2 = xf[..., :half], xf[..., half:]
    return jnp.concatenate([x1 * cos - x2 * sin, x2 * cos + x1 * sin], axis=-1).astype(x.dtype)


def _pool_mixer(u, w_pool, scale):
    B, T, C = u.shape
    uf = u.astype(jnp.float32)
    cs = jnp.concatenate([jnp.zeros((B, 1, C), jnp.float32), jnp.cumsum(uf, axis=1)], axis=1)
    t = jnp.arange(T)
    means = []
    for g, w in enumerate(POOL_WINDOWS):
        lo = jnp.clip(t - w // 2, 0, T)
        hi = jnp.clip(t + w // 2, 0, T)
        csg = cs[..., g * POOL_GW:(g + 1) * POOL_GW]
        cnt = (hi - lo).astype(jnp.float32)[None, :, None]
        means.append((jnp.take(csg, hi, axis=1) - jnp.take(csg, lo, axis=1)) / cnt)
    mixed = (jnp.concatenate(means, axis=-1) - uf).reshape(B, T, len(POOL_WINDOWS), POOL_GW)
    out = jnp.einsum('btgc,gcd->btgd', mixed, w_pool.astype(jnp.float32)).reshape(B, T, C)
    return (out * scale.astype(jnp.float32)).astype(u.dtype)


def _window_gqa(q, k, v, sink):
    B, T, Hq, dh = q.shape
    Hkv = k.shape[2]
    G = Hq // Hkv
    Bk = GQA_BLOCK
    nb = T // Bk
    pad = ((0, 0), (Bk, Bk), (0, 0), (0, 0))
    kp = jnp.pad(k, pad).reshape(B, nb + 2, Bk, Hkv, dh)
    vp = jnp.pad(v, pad).reshape(B, nb + 2, Bk, Hkv, dh)
    band = lambda a: jnp.concatenate([a[:, :-2], a[:, 1:-1], a[:, 2:]], axis=2)
    kb, vb = band(kp), band(vp)
    qb = q.reshape(B, nb, Bk, Hkv, G, dh)
    s = jnp.einsum('bnqhgd,bnkhd->bnhgqk', qb, kb, preferred_element_type=jnp.float32) * (dh ** -0.5)
    blk = jnp.arange(nb)[:, None] * Bk
    qpos = blk + jnp.arange(Bk)[None, :]
    kpos = blk - Bk + jnp.arange(3 * Bk)[None, :]
    kp3 = kpos[:, None, :]
    valid = (jnp.abs(kp3 - qpos[:, :, None]) <= GQA_WINDOW) & (kp3 >= 0) & (kp3 < T)
    s = jnp.where(valid[None, :, None, None], s, NEG)
    sink_l = sink.astype(jnp.float32).reshape(Hkv, G)[:, :, None, None]
    m = jnp.maximum(jnp.max(s, axis=-1, keepdims=True), sink_l)
    p = jnp.exp(s - m)
    p = p / (jnp.sum(p, axis=-1, keepdims=True) + jnp.exp(sink_l - m))
    o = jnp.einsum('bnhgqk,bnkhd->bnqhgd', p.astype(v.dtype), vb)
    return o.reshape(B, T, Hq * dh)


def _gla_causal(q, k, v, logf):
    B, T, H, dk = q.shape
    dv = v.shape[-1]
    C = HGRN_CHUNK
    nc = T // C
    q = q.reshape(B, nc, C, H, dk)
    k = k.reshape(B, nc, C, H, dk)
    logf = logf.reshape(B, nc, C, H, dk)
    v = v.reshape(B, nc, C, H, dv)
    b = jnp.cumsum(logf, axis=2)
    causal = jnp.tril(jnp.ones((C, C), dtype=bool))[None, None, :, :, None, None]
    diff = jnp.where(causal, b[:, :, :, None] - b[:, :, None, :], 0.0)
    decay = jnp.where(causal, jnp.exp(diff), 0.0)
    attn = jnp.einsum('bntshd,bnthd,bnshd->bnhts', decay, q, k)
    o_intra = jnp.einsum('bnhts,bnshv->bnthv', attn, v)
    b_last = b[:, :, -1]
    u = jnp.einsum('bnshd,bnshv->bnhdv', k * jnp.exp(b_last[:, :, None] - b), v)
    a = jnp.exp(b_last)

    def step(state, inp):
        a_n, u_n = inp
        return a_n[..., None] * state + u_n, state

    s0 = jnp.zeros((B, H, dk, dv), jnp.float32)
    _, s_start = lax.scan(step, s0, (jnp.moveaxis(a, 1, 0), jnp.moveaxis(u, 1, 0)))
    s_start = jnp.moveaxis(s_start, 0, 1)
    o_inter = jnp.einsum('bnthd,bnhdv->bnthv', q * jnp.exp(b), s_start)
    return (o_intra + o_inter).reshape(B, T, H, dv)


def _hgrn_mixer(zff, zfb, zi, zq, zg, lb, onorm_g):
    B, T, _ = zi.shape
    f32 = jnp.float32
    shp = lambda z, d: z.astype(f32).reshape(B, T, HGRN_HEADS, d)
    q = jax.nn.silu(shp(zq, HGRN_DK))
    v = shp(zi, HGRN_DV)

    def gates(zf, lbd):
        f = lbd + (1.0 - lbd) * jax.nn.sigmoid(zf.astype(f32))
        logf = jnp.log(jnp.maximum(f, TINY))
        key = 1.0 - f
        return shp(key, HGRN_DK), shp(logf, HGRN_DK)

    k_f, lf_f = gates(zff, lb[0])
    k_b, lf_b = gates(zfb, lb[1])
    flip = lambda a: jnp.flip(a, axis=1)
    o = _gla_causal(q, k_f, v, lf_f) + flip(_gla_causal(flip(q), flip(k_b), flip(v), flip(lf_b)))
    o = _rmsnorm(o, onorm_g) * jax.nn.silu(shp(zg, HGRN_DV))
    return o.reshape(B, T, HGRN_VW).astype(zi.dtype)


def _neighborhood_attention(q, k, v, rpb):
    B, T, H, dh = q.shape
    rows = T // GRID_W
    kr = min(NAT_KR_MAX, rows)
    ncb = GRID_W // NAT_QCB
    r = jnp.arange(rows)
    row_idx = jnp.clip(r - kr // 2, 0, rows - kr)[:, None] + jnp.arange(kr)[None, :]
    qcol = jnp.arange(GRID_W).reshape(ncb, NAT_QCB)
    qc0 = jnp.clip(qcol - NAT_KC // 2, 0, GRID_W - NAT_KC)
    col_idx = jnp.clip(jnp.arange(ncb) * NAT_QCB - NAT_KC // 2, 0, GRID_W - NAT_KCB)[:, None] + jnp.arange(NAT_KCB)[None, :]
    kgrid = k.reshape(B, rows, GRID_W, H, dh)
    vgrid = v.reshape(B, rows, GRID_W, H, dh)
    ri = row_idx[:, None, :, None]
    ci = col_idx[None, :, None, :]
    kb = kgrid[:, ri, ci]
    vb = vgrid[:, ri, ci]
    qb = q.reshape(B, rows, ncb, NAT_QCB, H, dh)
    s = jnp.einsum('brcqhd,brcikhd->brchqik', qb, kb, preferred_element_type=jnp.float32) * (dh ** -0.5)
    drow = row_idx - r[:, None] + (NAT_KR_MAX - 1)
    dcol = jnp.clip(col_idx[:, None, :] - qcol[:, :, None] + (NAT_KC - 1), 0, 2 * NAT_KC - 2)
    bias = rpb.astype(jnp.float32)[:, drow[:, None, None, :, None], dcol[None, :, :, None, :]]
    s = s + jnp.transpose(bias, (1, 2, 0, 3, 4, 5))[None]
    in_win = (col_idx[:, None, :] >= qc0[:, :, None]) & (col_idx[:, None, :] < qc0[:, :, None] + NAT_KC)
    s = jnp.where(in_win[None, None, :, None, :, None, :], s, NEG)
    p = jax.nn.softmax(s.reshape(B, rows, ncb, H, NAT_QCB, kr * NAT_KCB), axis=-1)
    o = jnp.einsum('brchqn,brcnhd->brcqhd', p.astype(v.dtype), vb.reshape(B, rows, ncb, kr * NAT_KCB, H, dh))
    return o.reshape(B, T, H * dh)


def _trunk(x, norm1_g, w_in, pool_w, pool_scale, gqa_qnorm, gqa_knorm, gqa_sink, hgrn_lb, hgrn_onorm,
           nat_qnorm, nat_knorm, nat_rpb, w_br_pool, w_br_gqa, w_br_hgrn, w_br_nat, w_o, norm2_g, w_up, w_down):
    B, T, D = x.shape
    pos = jnp.arange(T)
    sm = jax.nn.softmax(hgrn_lb.astype(jnp.float32), axis=0)
    lower = jnp.cumsum(sm, axis=0) - sm[:1]
    pts = _split_points()
    for l in range(DEPTH):
        h = _rmsnorm(x, norm1_g[l])
        z = jnp.einsum('btd,de->bte', h, w_in[l])
        (zp, zq, zk, zv, zff, zfb, zi, zhq, zhg, znq, znk, znv, zgate) = jnp.split(z, pts, axis=-1)
        oa = _pool_mixer(zp, pool_w[l], pool_scale[l])
        q = _rope(_rmsnorm(zq.reshape(B, T, GQA_HEADS, HEAD_DIM), gqa_qnorm[l]), pos)
        k = _rope(_rmsnorm(zk.reshape(B, T, GQA_KV_HEADS, HEAD_DIM), gqa_knorm[l]), pos)
        ob = _window_gqa(q, k, zv.reshape(B, T, GQA_KV_HEADS, HEAD_DIM), gqa_sink[l])
        oc = _hgrn_mixer(zff, zfb, zi, zhq, zhg, lower[l], hgrn_onorm[l])
        nq = _rmsnorm(znq.reshape(B, T, NAT_HEADS, HEAD_DIM), nat_qnorm[l])
        nk = _rmsnorm(znk.reshape(B, T, NAT_HEADS, HEAD_DIM), nat_knorm[l])
        od = _neighborhood_attention(nq, nk, znv.reshape(B, T, NAT_HEADS, HEAD_DIM), nat_rpb[l])
        g = jax.nn.sigmoid(zgate.astype(jnp.float32)).astype(x.dtype).reshape(B, T, N_BRANCH, D)
        merged = (g[:, :, 0] * jnp.einsum('btc,cd->btd', oa, w_br_pool[l])
                  + g[:, :, 1] * jnp.einsum('btc,cd->btd', ob, w_br_gqa[l])
                  + g[:, :, 2] * jnp.einsum('btc,cd->btd', oc, w_br_hgrn[l])
                  + g[:, :, 3] * jnp.einsum('btc,cd->btd', od, w_br_nat[l]))
        x = x + jnp.einsum('btd,de->bte', merged, w_o[l])
        h2 = _rmsnorm(x, norm2_g[l])
        hid = jnp.square(jax.nn.relu(jnp.einsum('btd,df->btf', h2, w_up[l])))
        x = x + jnp.einsum('btf,fd->btd', hid, w_down[l])
    return x


def setup_inputs(seed: int = 0) -> dict:
    key = jax.random.key(seed)
    ks = jax.random.split(key, 22)
    nrm = lambda kk, shape, scale: scale * jax.random.normal(kk, shape, jnp.float32)
    L = DEPTH
    return {
        'x_prompt': nrm(ks[0], (BATCH, SEQ, D_MODEL), 1.0),
        'x_sample': nrm(ks[1], (DEC_BATCH, DEC_SEQ, D_MODEL), 1.0),
        'norm1_g': 1.0 + nrm(ks[2], (L, D_MODEL), 0.1),
        'w_in': nrm(ks[3], (L, D_MODEL, IN_WIDTH), D_MODEL ** -0.5),
        'pool_w': nrm(ks[4], (L, len(POOL_WINDOWS), POOL_GW, POOL_GW), POOL_GW ** -0.5),
        'pool_scale': 1.0 + nrm(ks[5], (L, POOL_WIDTH), 0.1),
        'gqa_qnorm': 1.0 + nrm(ks[6], (L, HEAD_DIM), 0.1),
        'gqa_knorm': 1.0 + nrm(ks[7], (L, HEAD_DIM), 0.1),
        'gqa_sink': nrm(ks[8], (L, GQA_HEADS), 0.5),
        'hgrn_lb': nrm(ks[9], (L, 2, HGRN_KW), 0.5),
        'hgrn_onorm': 1.0 + nrm(ks[10], (L, HGRN_DV), 0.1),
        'nat_qnorm': 1.0 + nrm(ks[11], (L, HEAD_DIM), 0.1),
        'nat_knorm': 1.0 + nrm(ks[12], (L, HEAD_DIM), 0.1),
        'nat_rpb': nrm(ks[13], (L, NAT_HEADS, 2 * NAT_KR_MAX - 1, 2 * NAT_KC - 1), 0.5),
        'w_br_pool': nrm(ks[14], (L, POOL_WIDTH, D_MODEL), POOL_WIDTH ** -0.5),
        'w_br_gqa': nrm(ks[15], (L, GQA_QW, D_MODEL), GQA_QW ** -0.5),
        'w_br_hgrn': nrm(ks[16], (L, HGRN_VW, D_MODEL), HGRN_VW ** -0.5),
        'w_br_nat': nrm(ks[17], (L, NAT_W, D_MODEL), NAT_W ** -0.5),
        'w_o': nrm(ks[18], (L, D_MODEL, D_MODEL), D_MODEL ** -0.5),
        'norm2_g': 1.0 + nrm(ks[19], (L, D_MODEL), 0.1),
        'w_up': nrm(ks[20], (L, D_MODEL, D_FF), D_MODEL ** -0.5),
        'w_down': nrm(ks[21], (L, D_FF, D_MODEL), D_FF ** -0.5),
    }


def reference(x_prompt, x_sample, norm1_g, w_in, pool_w, pool_scale, gqa_qnorm, gqa_knorm, gqa_sink, hgrn_lb,
              hgrn_onorm, nat_qnorm, nat_knorm, nat_rpb, w_br_pool, w_br_gqa, w_br_hgrn, w_br_nat, w_o, norm2_g,
              w_up, w_down):
    y_prompt = _trunk(x_prompt, norm1_g, w_in, pool_w, pool_scale, gqa_qnorm, gqa_knorm, gqa_sink, hgrn_lb,
                      hgrn_onorm, nat_qnorm, nat_knorm, nat_rpb, w_br_pool, w_br_gqa, w_br_hgrn, w_br_nat, w_o,
                      norm2_g, w_up, w_down)
    y_sample = _trunk(x_sample, norm1_g, w_in, pool_w, pool_scale, gqa_qnorm, gqa_knorm, gqa_sink, hgrn_lb,
                      hgrn_onorm, nat_qnorm, nat_knorm, nat_rpb, w_br_pool, w_br_gqa, w_br_hgrn, w_br_nat, w_o,
                      norm2_g, w_up, w_down)
    return (y_prompt, y_sample)
```

```python
import functools

import jax
import jax.numpy as jnp
from jax import lax
from jax.experimental import pallas as pl
from jax.experimental.pallas import tpu as pltpu

F32 = jnp.float32
BF16 = jnp.bfloat16

D_MODEL = 1024
GRID_W = 64
EPS = 1e-6
NEG = -1e30
TINY = 1e-30
HEAD_DIM = 64
ROPE_THETA = 10000.0
N_BRANCH = 4
D_FF = 4 * D_MODEL

POOL_WIDTH = 256
POOL_WINDOWS = (2, 4, 8, 16)
POOL_GW = 64
POOL_HALO = 16

GQA_HEADS = 8
GQA_KV_HEADS = 2
GQA_GROUP = GQA_HEADS // GQA_KV_HEADS
GQA_WINDOW = 128
GQA_QW = GQA_HEADS * HEAD_DIM
GQA_KVW = GQA_KV_HEADS * HEAD_DIM

HGRN_HEADS = 4
HGRN_W = HGRN_HEADS * HEAD_DIM

NAT_HEADS = 4
NAT_KR = 8
NAT_KC = 16
NAT_W = NAT_HEADS * HEAD_DIM

IN_WIDTH = 7168
GATE_OFF = 3072

TM_IN = 512
TQ_GQA = 128
TT_HGRN = 128
TM_MERGE = 256
TM_MLP = 512
VMEM_LIMIT = 56 * 1024 * 1024


def _sigmoid(x):
    return 1.0 / (1.0 + jnp.exp(-x))


def _dot(a, b):
    return jnp.dot(a, b, preferred_element_type=F32)


def _dot_nt(a, b):
    return lax.dot_general(a, b, (((1,), (1,)), ((), ())), preferred_element_type=F32)


def _dot_tn(a, b):
    return lax.dot_general(a, b, (((0,), (0,)), ((), ())), preferred_element_type=F32)


def _const_spec(shape):
    n = len(shape)
    return pl.BlockSpec(shape, lambda *_: (0,) * n, pipeline_mode=pl.Buffered(1))


def _inproj_kernel(x_ref, g_ref, w_ref, cos_ref, sin_ref, qkg_ref, ng_ref, lb_ref, grp_ref,
                   up_ref, gq_ref, gk_ref, gv_ref, lf_ref, hk_ref, hv_ref, hq_ref, hg_ref,
                   nq_ref, nk_ref, nv_ref, gate_ref):
    x = x_ref[...]
    ms = jnp.mean(x * x, axis=-1, keepdims=True)
    h = (x * lax.rsqrt(ms + EPS) * g_ref[...]).astype(BF16)

    def headnorm(y, gain):
        w = y.shape[1]
        msq = _dot((y * y).astype(BF16), grp_ref[0:w, 0:w])
        return y * lax.rsqrt(msq + EPS) * gain

    z = _dot(h, w_ref[:, 0:1024])
    up_ref[...] = z[:, 0:POOL_WIDTH].astype(BF16)
    qkw = GQA_QW + GQA_KVW
    qk = headnorm(z[:, 256:256 + qkw], qkg_ref[...])
    reps = qkw // 128
    cos = jnp.concatenate([cos_ref[...]] * reps, axis=1)
    sin = jnp.concatenate([sin_ref[...]] * reps, axis=1)
    first = (lax.broadcasted_iota(jnp.int32, (1, qkw), 1) % HEAD_DIM) < (HEAD_DIM // 2)
    partner = jnp.where(first, pltpu.roll(qk, qkw - HEAD_DIM // 2, 1), pltpu.roll(qk, HEAD_DIM // 2, 1))
    qk = qk * cos + partner * sin
    gq_ref[...] = qk[:, 0:GQA_QW].astype(BF16)
    gk_ref[...] = qk[:, GQA_QW:qkw].astype(BF16)
    gv_ref[...] = z[:, 896:1024].astype(BF16)

    z = _dot(h, w_ref[:, 1024:2048])
    lb = lb_ref[...]
    f = lb + (1.0 - lb) * _sigmoid(z[:, 0:512])
    lf_ref[...] = jnp.log(jnp.maximum(f, TINY))
    hk_ref[...] = (1.0 - f).astype(BF16)
    hv_ref[...] = z[:, 512:768].astype(BF16)
    zq = z[:, 768:1024]
    hq_ref[...] = (zq * _sigmoid(zq)).astype(BF16)

    z = _dot(h, w_ref[:, 2048:3072])
    zg = z[:, 0:256]
    hg_ref[...] = (zg * _sigmoid(zg)).astype(BF16)
    nqk = headnorm(z[:, 256:768], ng_ref[...])
    nq_ref[...] = nqk[:, 0:256].astype(BF16)
    nk_ref[...] = nqk[:, 256:512].astype(BF16)
    nv_ref[...] = z[:, 768:1024].astype(BF16)

    for c in range(N_BRANCH):
        z = _dot(h, w_ref[:, GATE_OFF + c * D_MODEL:GATE_OFF + (c + 1) * D_MODEL])
        gate_ref[:, c * D_MODEL:(c + 1) * D_MODEL] = _sigmoid(z).astype(BF16)


def _inproj(x2, g1, w_in, cos, sin, qk_gain, n_gain, lb, grp, T):
    N = x2.shape[0]
    tm = TM_IN
    tpb = T // tm
    tok = lambda w: pl.BlockSpec((tm, w), lambda i: (i, 0))
    widths = [(POOL_WIDTH, BF16), (GQA_QW, BF16), (GQA_KVW, BF16), (GQA_KVW, BF16), (512, F32), (512, BF16),
              (256, BF16), (256, BF16), (256, BF16), (256, BF16), (256, BF16), (256, BF16), (4 * D_MODEL, BF16)]
    return pl.pallas_call(
        _inproj_kernel,
        grid=(N // tm,),
        in_specs=[
            tok(D_MODEL),
            _const_spec((1, D_MODEL)),
            _const_spec((D_MODEL, IN_WIDTH)),
            pl.BlockSpec((tm, 128), lambda i: (i % tpb, 0)),
            pl.BlockSpec((tm, 128), lambda i: (i % tpb, 0)),
            _const_spec((1, GQA_QW + GQA_KVW)),
            _const_spec((1, 2 * NAT_W)),
            _const_spec((1, 512)),
            _const_spec((640, 640)),
        ],
        out_specs=[tok(w) for w, _ in widths],
        out_shape=[jax.ShapeDtypeStruct((N, w), dt) for w, dt in widths],
        compiler_params=pltpu.CompilerParams(dimension_semantics=("parallel",), vmem_limit_bytes=VMEM_LIMIT),
        name="inproj",
    )(x2, g1, w_in, cos, sin, qk_gain, n_gain, lb, grp)


def _gqa_kernel(sink_ref, q_ref, k_ref, v_ref, o_ref, *, tq, T):
    i = pl.program_id(1)
    t0 = i * tq
    win = tq + 2 * GQA_WINDOW
    w0 = pl.multiple_of(jnp.clip(t0 - GQA_WINDOW, 0, T - win), 128)
    q = q_ref[0]
    k = k_ref[0, pl.ds(w0, win), :]
    v = v_ref[0, pl.ds(w0, win), :]
    qpos = t0 + lax.broadcasted_iota(jnp.int32, (tq, 1), 0)
    kpos = w0 + lax.broadcasted_iota(jnp.int32, (1, win), 1)
    valid = jnp.abs(kpos - qpos) <= GQA_WINDOW
    outs = []
    for hk in range(GQA_KV_HEADS):
        kh = k[:, hk * HEAD_DIM:(hk + 1) * HEAD_DIM]
        vh = v[:, hk * HEAD_DIM:(hk + 1) * HEAD_DIM]
        for g in range(GQA_GROUP):
            h = hk * GQA_GROUP + g
            s = _dot_nt(q[:, h * HEAD_DIM:(h + 1) * HEAD_DIM], kh)
            s = jnp.where(valid, s, NEG)
            sink = sink_ref[h]
            m = jnp.maximum(jnp.max(s, axis=-1, keepdims=True), sink)
            p = jnp.exp(s - m)
            denom = jnp.sum(p, axis=-1, keepdims=True) + jnp.exp(sink - m)
            outs.append(_dot(p.astype(BF16), vh) / denom)
    o_ref[0] = jnp.concatenate(outs, axis=1).astype(BF16)


def _gqa(q, k, v, sink):
    Bt, T, _ = q.shape
    tq = TQ_GQA
    return pl.pallas_call(
        functools.partial(_gqa_kernel, tq=tq, T=T),
        grid=(Bt, T // tq),
        in_specs=[
            pl.BlockSpec(memory_space=pltpu.SMEM),
            pl.BlockSpec((1, tq, GQA_QW), lambda b, i: (b, i, 0)),
            pl.BlockSpec((1, T, GQA_KVW), lambda b, i: (b, 0, 0)),
            pl.BlockSpec((1, T, GQA_KVW), lambda b, i: (b, 0, 0)),
        ],
        out_specs=pl.BlockSpec((1, tq, GQA_QW), lambda b, i: (b, i, 0)),
        out_shape=jax.ShapeDtypeStruct((Bt, T, GQA_QW), BF16),
        compiler_params=pltpu.CompilerParams(dimension_semantics=("parallel", "arbitrary"),
                                             vmem_limit_bytes=VMEM_LIMIT),
        name="gqa",
    )(sink, q, k, v)


def _nat_kernel(q_ref, k_ref, v_ref, bias_ref, o_ref, *, rows):
    r = pl.program_id(1)
    r0 = jnp.clip(r - NAT_KR // 2, 0, rows - NAT_KR)
    e = r - r0
    start = pl.multiple_of(r0 * GRID_W, GRID_W)
    nk = NAT_KR * GRID_W
    q = q_ref[0]
    k = k_ref[0, pl.ds(start, nk), :]
    v = v_ref[0, pl.ds(start, nk), :]
    outs = []
    for h in range(NAT_HEADS):
        sl = slice(h * HEAD_DIM, (h + 1) * HEAD_DIM)
        s = _dot_nt(q[:, sl], k[:, sl]) + bias_ref[e, h]
        m = jnp.max(s, axis=-1, keepdims=True)
        p = jnp.exp(s - m)
        denom = jnp.sum(p, axis=-1, keepdims=True)
        outs.append(_dot(p.astype(BF16), v[:, sl]) / denom)
    o_ref[0] = jnp.concatenate(outs, axis=1).astype(BF16)


def _nat(q, k, v, bias):
    Bt, T, _ = q.shape
    rows = T // GRID_W
    assert rows >= NAT_KR
    return pl.pallas_call(
        functools.partial(_nat_kernel, rows=rows),
        grid=(Bt, rows),
        in_specs=[
            pl.BlockSpec((1, GRID_W, NAT_W), lambda b, r: (b, r, 0)),
            pl.BlockSpec((1, T, NAT_W), lambda b, r: (b, 0, 0)),
            pl.BlockSpec((1, T, NAT_W), lambda b, r: (b, 0, 0)),
            _const_spec((NAT_KR, NAT_HEADS, GRID_W, NAT_KR * GRID_W)),
        ],
        out_specs=pl.BlockSpec((1, GRID_W, NAT_W), lambda b, r: (b, r, 0)),
        out_shape=jax.ShapeDtypeStruct((Bt, T, NAT_W), BF16),
        compiler_params=pltpu.CompilerParams(dimension_semantics=("parallel", "arbitrary"),
                                             vmem_limit_bytes=VMEM_LIMIT),
        name="nat",
    )(q, k, v, bias)


def _nat_bias_table(rpb):
    e = jnp.arange(NAT_KR)[:, None, None, None]
    c = jnp.arange(GRID_W)[None, :, None, None]
    i = jnp.arange(NAT_KR)[None, None, :, None]
    kc = jnp.arange(GRID_W)[None, None, None, :]
    qc0 = jnp.clip(c - NAT_KC // 2, 0, GRID_W - NAT_KC)
    valid = (kc >= qc0) & (kc < qc0 + NAT_KC)
    drow = i - e + (NAT_KR - 1)
    dcol = jnp.clip(kc - c + (NAT_KC - 1), 0, 2 * NAT_KC - 2)
    full = (NAT_KR, GRID_W, NAT_KR, GRID_W)
    b = rpb.astype(F32)[:, jnp.broadcast_to(drow, full), jnp.broadcast_to(dcol, full)]
    b = jnp.where(jnp.broadcast_to(valid, full)[None], b, NEG)
    return jnp.transpose(b, (1, 0, 2, 3, 4)).reshape(NAT_KR, NAT_HEADS, GRID_W, NAT_KR * GRID_W)


def _hgrn_kernel(*refs, rev, final, Tt):
    if final:
        lf_ref, k_ref, v_ref, q_ref, oprev_ref, og_ref, gain_ref, grp_ref, o_ref, st_ref = refs
    else:
        lf_ref, k_ref, v_ref, q_ref, o_ref, st_ref = refs
    n = pl.program_id(1)

    @pl.when(n == 0)
    def _():
        st_ref[...] = jnp.zeros_like(st_ref)

    lf = lf_ref[0]
    k = k_ref[0].astype(F32)
    q = q_ref[0].astype(F32)
    v = v_ref[0]
    t_idx = lax.broadcasted_iota(jnp.int32, (Tt, 1), 0)
    tau = (Tt - 1 - t_idx) if rev else t_idx
    row = lax.broadcasted_iota(jnp.int32, (Tt, Tt), 0)
    col = lax.broadcasted_iota(jnp.int32, (Tt, Tt), 1)

    tri = jnp.where((col >= row) if rev else (col <= row), 1.0, 0.0).astype(BF16)
    hi = lf.astype(BF16)
    r1 = lf - hi.astype(F32)
    mid = r1.astype(BF16)
    lo = (r1 - mid.astype(F32)).astype(BF16)
    b = _dot(tri, hi) + _dot(tri, mid) + _dot(tri, lo)

    hs = [slice(h * HEAD_DIM, (h + 1) * HEAD_DIM) for h in range(HGRN_HEADS)]
    qb16 = q_ref[0]
    kb16 = k_ref[0]
    diag = row == col
    acc = [jnp.where(diag, _dot_nt(qb16[:, s], kb16[:, s]), 0.0) for s in hs]

    before = b - lf
    end = b
    size = 2
    while size <= Tt:
        half = size // 2
        upper = (tau % size) >= half
        w = jnp.exp(jnp.where(upper, b - before, end - b))
        qt = jnp.where(upper, q * w, 0.0).astype(BF16)
        kt = jnp.where(upper, 0.0, k * w).astype(BF16)
        same = (row // size) == (col // size)
        for h, s in enumerate(hs):
            acc[h] = acc[h] + jnp.where(same, _dot_nt(qt[:, s], kt[:, s]), 0.0)
        if size < Tt:
            back, fwd = (Tt - half, half) if rev else (half, Tt - half)
            before = jnp.where(upper, pltpu.roll(before, back, 0), before)
            end = jnp.where(upper, end, pltpu.roll(end, fwd, 0))
        size *= 2

    b_last = b[0:1] if rev else b[Tt - 1:Tt]
    q_in = (q * jnp.exp(b)).astype(BF16)
    k_out = (k * jnp.exp(b_last - b)).astype(BF16)
    a_out = jnp.exp(b_last)
    outs = []
    for h, s in enumerate(hs):
        st = st_ref[h]
        o = _dot(acc[h].astype(BF16), v[:, s]) + _dot_nt(q_in[:, s], st.astype(BF16))
        st_ref[h] = st * a_out[:, s] + _dot_tn(v[:, s], k_out[:, s])
        outs.append(o)
    o = jnp.concatenate(outs, axis=1)
    if final:
        o = o + oprev_ref[0].astype(F32)
        msq = _dot((o * o).astype(BF16), grp_ref[...])
        o = o * lax.rsqrt(msq + EPS) * gain_ref[...] * og_ref[0].astype(F32)
    o_ref[0] = o.astype(BF16)


def _hgrn_sweep(lf, hk, hv, hq, *, rev, extra=None):
    Bt, T, _ = hv.shape
    Tt = TT_HGRN
    nt = T // Tt
    d = 1 if rev else 0
    tile = (lambda b, n: (b, nt - 1 - n, 0)) if rev else (lambda b, n: (b, n, 0))
    tile_d = (lambda b, n: (b, nt - 1 - n, d)) if rev else (lambda b, n: (b, n, d))
    in_specs = [
        pl.BlockSpec((1, Tt, HGRN_W), tile_d),
        pl.BlockSpec((1, Tt, HGRN_W), tile_d),
        pl.BlockSpec((1, Tt, HGRN_W), tile),
        pl.BlockSpec((1, Tt, HGRN_W), tile),
    ]
    args = [lf, hk, hv, hq]
    if extra is not None:
        oprev, og, gain, grp = extra
        in_specs += [pl.BlockSpec((1, Tt, HGRN_W), tile), pl.BlockSpec((1, Tt, HGRN_W), tile),
                     _const_spec((1, HGRN_W)), _const_spec((HGRN_W, HGRN_W))]
        args += [oprev, og, gain, grp]
    return pl.pallas_call(
        functools.partial(_hgrn_kernel, rev=rev, final=extra is not None, Tt=Tt),
        grid=(Bt, nt),
        in_specs=in_specs,
        out_specs=pl.BlockSpec((1, Tt, HGRN_W), tile),
        out_shape=jax.ShapeDtypeStruct((Bt, T, HGRN_W), BF16),
        scratch_shapes=[pltpu.VMEM((HGRN_HEADS, HEAD_DIM, HEAD_DIM), F32)],
        compiler_params=pltpu.CompilerParams(dimension_semantics=("parallel", "arbitrary"),
                                             vmem_limit_bytes=VMEM_LIMIT),
        name="hgrn_bwd" if rev else "hgrn_fwd",
    )(*args)


def _merge_kernel(x_ref, u_ref, ob_ref, oc_ref, od_ref, g_ref, pw_ref, ps_ref, wbp_ref, wbg_ref, wbh_ref,
                  wbn_ref, wo_ref, o_ref, *, tm, T):
    i = pl.program_id(1)
    t0 = pl.multiple_of(i * tm, tm)
    win = tm + 2 * POOL_HALO
    w0 = pl.multiple_of(jnp.clip(t0 - POOL_HALO, 0, T - win), POOL_HALO)
    uw = u_ref[0, pl.ds(w0, win), :]
    uc = u_ref[0, pl.ds(t0, tm), :].astype(F32)
    tg = t0 + lax.broadcasted_iota(jnp.int32, (tm, 1), 0)
    sg = w0 + lax.broadcasted_iota(jnp.int32, (1, win), 1)
    lane = lax.broadcasted_iota(jnp.int32, (1, POOL_WIDTH), 1)
    mixed = jnp.zeros((tm, POOL_WIDTH), F32)
    for g, w in enumerate(POOL_WINDOWS):
        lo = jnp.maximum(tg - w // 2, 0)
        hi = jnp.minimum(tg + w // 2, T)
        inside = jnp.where(sg >= lo, jnp.where(sg < hi, 1.0, 0.0), 0.0).astype(BF16)
        mean = _dot(inside, uw) / (hi - lo).astype(F32)
        mixed = jnp.where(lane // POOL_GW == g, mean, mixed)
    mixed = mixed - uc
    oa = _dot(mixed.astype(BF16), pw_ref[...]) * ps_ref[...]
    gate = lambda c: g_ref[0, :, c * D_MODEL:(c + 1) * D_MODEL].astype(F32)
    merged = (gate(0) * _dot(oa.astype(BF16), wbp_ref[...])
              + gate(1) * _dot(ob_ref[0], wbg_ref[...])
              + gate(2) * _dot(oc_ref[0], wbh_ref[...])
              + gate(3) * _dot(od_ref[0], wbn_ref[...]))
    o_ref[0] = x_ref[0] + _dot(merged.astype(BF16), wo_ref[...])


def _merge(x, u, ob, oc, od, gates, pw, ps, wbp, wbg, wbh, wbn, wo):
    Bt, T, _ = x.shape
    tm = TM_MERGE
    tok = lambda w: pl.BlockSpec((1, tm, w), lambda b, i: (b, i, 0))
    return pl.pallas_call(
        functools.partial(_merge_kernel, tm=tm, T=T),
        grid=(Bt, T // tm),
        in_specs=[
            tok(D_MODEL),
            pl.BlockSpec((1, T, POOL_WIDTH), lambda b, i: (b, 0, 0)),
            tok(GQA_QW), tok(HGRN_W), tok(NAT_W), tok(N_BRANCH * D_MODEL),
            _const_spec((POOL_WIDTH, POOL_WIDTH)), _const_spec((1, POOL_WIDTH)),
            _const_spec((POOL_WIDTH, D_MODEL)), _const_spec((GQA_QW, D_MODEL)),
            _const_spec((HGRN_W, D_MODEL)), _const_spec((NAT_W, D_MODEL)),
            _const_spec((D_MODEL, D_MODEL)),
        ],
        out_specs=tok(D_MODEL),
        out_shape=jax.ShapeDtypeStruct((Bt, T, D_MODEL), F32),
        compiler_params=pltpu.CompilerParams(dimension_semantics=("parallel", "arbitrary"),
                                             vmem_limit_bytes=VMEM_LIMIT),
        name="merge",
    )(x, u, ob, oc, od, gates, pw, ps, wbp, wbg, wbh, wbn, wo)


def _mlp_kernel(x_ref, g_ref, wu_ref, wd_ref, o_ref):
    x = x_ref[...]
    ms = jnp.mean(x * x, axis=-1, keepdims=True)
    h = (x * lax.rsqrt(ms + EPS) * g_ref[...]).astype(BF16)
    acc = x
    for c in range(D_FF // D_MODEL):
        sl = slice(c * D_MODEL, (c + 1) * D_MODEL)
        hid = jnp.maximum(_dot(h, wu_ref[:, sl]), 0.0)
        acc = acc + _dot((hid * hid).astype(BF16), wd_ref[sl, :])
    o_ref[...] = acc


def _mlp(x2, g2, wu, wd):
    N = x2.shape[0]
    tm = TM_MLP
    return pl.pallas_call(
        _mlp_kernel,
        grid=(N // tm,),
        in_specs=[pl.BlockSpec((tm, D_MODEL), lambda i: (i, 0)), _const_spec((1, D_MODEL)),
                  _const_spec((D_MODEL, D_FF)), _const_spec((D_FF, D_MODEL))],
        out_specs=pl.BlockSpec((tm, D_MODEL), lambda i: (i, 0)),
        out_shape=jax.ShapeDtypeStruct((N, D_MODEL), F32),
        compiler_params=pltpu.CompilerParams(dimension_semantics=("parallel",), vmem_limit_bytes=VMEM_LIMIT),
        name="mlp",
    )(x2, g2, wu, wd)


def _block_diag(blocks):
    n, a, b = blocks.shape
    eye = jnp.eye(n, dtype=blocks.dtype)
    return (eye[:, None, :, None] * blocks[:, :, None, :]).reshape(n * a, n * b)


def _trunk(x, norm1_g, w_in, pool_w, pool_scale, gqa_qnorm, gqa_knorm, gqa_sink, hgrn_lb, hgrn_onorm,
           nat_qnorm, nat_knorm, nat_rpb, w_br_pool, w_br_gqa, w_br_hgrn, w_br_nat, w_o, norm2_g, w_up, w_down):
    Bt, T, D = x.shape
    N = Bt * T
    depth = w_in.shape[0]
    half = HEAD_DIM // 2
    inv = ROPE_THETA ** (-jnp.arange(half, dtype=F32) / half)
    ang = jnp.arange(T, dtype=F32)[:, None] * inv[None, :]
    cos = jnp.tile(jnp.concatenate([jnp.cos(ang), jnp.cos(ang)], axis=1), (1, 2))
    sin = jnp.tile(jnp.concatenate([-jnp.sin(ang), jnp.sin(ang)], axis=1), (1, 2))
    sm = jax.nn.softmax(hgrn_lb.astype(F32), axis=0)
    lower = jnp.cumsum(sm, axis=0) - sm[:1]
    grp = _block_diag(jnp.full((640 // HEAD_DIM, HEAD_DIM, HEAD_DIM), 1.0 / HEAD_DIM, F32)).astype(BF16)
    scale = HEAD_DIM ** -0.5

    for l in range(depth):
        qk_gain = jnp.concatenate([jnp.tile(gqa_qnorm[l] * scale, GQA_HEADS), jnp.tile(gqa_knorm[l], GQA_KV_HEADS)])
        n_gain = jnp.concatenate([jnp.tile(nat_qnorm[l] * scale, NAT_HEADS), jnp.tile(nat_knorm[l], NAT_HEADS)])
        (up, gq, gk, gv, lf, hk, hv, hq, hg, nq, nk, nv, gates) = _inproj(
            x.reshape(N, D), norm1_g[l][None, :], w_in[l].astype(BF16), cos, sin, qk_gain[None, :], n_gain[None, :],
            lower[l].reshape(1, 2 * HGRN_W), grp, T)
        r3 = lambda a: a.reshape(Bt, T, a.shape[-1])
        ob = _gqa(r3(gq), r3(gk), r3(gv), gqa_sink[l].astype(F32))
        od = _nat(r3(nq), r3(nk), r3(nv), _nat_bias_table(nat_rpb[l]))
        o_f = _hgrn_sweep(r3(lf), r3(hk), r3(hv), r3(hq), rev=False)
        oc = _hgrn_sweep(r3(lf), r3(hk), r3(hv), r3(hq), rev=True,
                         extra=(o_f, r3(hg), jnp.tile(hgrn_onorm[l], HGRN_HEADS)[None, :], grp[:HGRN_W, :HGRN_W]))
        x = _merge(x, r3(up), ob, oc, od, r3(gates),
                   _block_diag(pool_w[l]).astype(BF16), pool_scale[l][None, :],
                   w_br_pool[l].astype(BF16), w_br_gqa[l].astype(BF16), w_br_hgrn[l].astype(BF16),
                   w_br_nat[l].astype(BF16), w_o[l].astype(BF16))
        x = _mlp(x.reshape(N, D), norm2_g[l][None, :], w_up[l].astype(BF16), w_down[l].astype(BF16)).reshape(Bt, T, D)
    return x


def kernel(x_prompt, x_sample, norm1_g, w_in, pool_w, pool_scale, gqa_qnorm, gqa_knorm, gqa_sink, hgrn_lb,
           hgrn_onorm, nat_qnorm, nat_knorm, nat_rpb, w_br_pool, w_br_gqa, w_br_hgrn, w_br_nat, w_o, norm2_g,
           w_up, w_down):
    nb = x_prompt.shape[0]
    x = jnp.concatenate([x_prompt, x_sample], axis=0)
    y = _trunk(x, norm1_g, w_in, pool_w, pool_scale, gqa_qnorm, gqa_knorm, gqa_sink, hgrn_lb, hgrn_onorm,
               nat_qnorm, nat_knorm, nat_rpb, w_br_pool, w_br_gqa, w_br_hgrn, w_br_nat, w_o, norm2_g, w_up, w_down)
    return (y[:nb], y[nb:])
```

```python
import functools

import numpy as np

import jax
import jax.numpy as jnp
from jax import lax
from jax.experimental import pallas as pl
from jax.experimental.pallas import tpu as pltpu

F32 = jnp.float32
BF16 = jnp.bfloat16

D_MODEL = 1024
GRID_W = 64
EPS = 1e-6
NEG = -1e30
TINY = 1e-30
HEAD_DIM = 64
ROPE_THETA = 10000.0
N_BRANCH = 4
D_FF = 4 * D_MODEL

POOL_WIDTH = 256
POOL_WINDOWS = (2, 4, 8, 16)
POOL_GW = 64
POOL_HALO = 16

GQA_HEADS = 8
GQA_KV_HEADS = 2
GQA_GROUP = GQA_HEADS // GQA_KV_HEADS
GQA_WINDOW = 128
GQA_QW = GQA_HEADS * HEAD_DIM
GQA_KVW = GQA_KV_HEADS * HEAD_DIM

HGRN_HEADS = 4
HGRN_W = HGRN_HEADS * HEAD_DIM

NAT_HEADS = 4
NAT_KR = 8
NAT_KC = 16
NAT_W = NAT_HEADS * HEAD_DIM
NAT_QR = 4
NAT_KROWS = 12
GQA_HEAD_ORDER = tuple((j % 2) * GQA_GROUP + j // 2 for j in range(GQA_HEADS))

IN_WIDTH = 7168
GATE_OFF = 3072

TM_IN = 512
TQ_GQA = 128
TT_HGRN = 128
TM_MERGE = 256
TM_MLP = 512
VMEM_LIMIT = 56 * 1024 * 1024


def _sigmoid(x):
    return 1.0 / (1.0 + jnp.exp(-x))


def _dot(a, b):
    return jnp.dot(a, b, preferred_element_type=F32)


def _dot_nt(a, b):
    return lax.dot_general(a, b, (((1,), (1,)), ((), ())), preferred_element_type=F32)


def _dot_tn(a, b):
    return lax.dot_general(a, b, (((0,), (0,)), ((), ())), preferred_element_type=F32)


def _const_spec(shape):
    n = len(shape)
    return pl.BlockSpec(shape, lambda *_: (0,) * n, pipeline_mode=pl.Buffered(1))


def _inproj_kernel(x_ref, g_ref, w_ref, cos_ref, sin_ref, qkg_ref, ng_ref, lb_ref, grp_ref,
                   up_ref, gq_ref, gk_ref, gv_ref, lf_ref, hk_ref, hv_ref, hq_ref, hg_ref,
                   nq_ref, nk_ref, nv_ref, gate_ref):
    x = x_ref[...]
    ms = jnp.mean(x * x, axis=-1, keepdims=True)
    h = (x * lax.rsqrt(ms + EPS) * g_ref[...]).astype(BF16)

    def headnorm(y, gain):
        w = y.shape[1]
        msq = _dot((y * y).astype(BF16), grp_ref[0:w, 0:w])
        return y * lax.rsqrt(msq + EPS) * gain

    z = _dot(h, w_ref[:, 0:1024])
    up_ref[...] = z[:, 0:POOL_WIDTH].astype(BF16)
    qkw = GQA_QW + GQA_KVW
    qk = headnorm(z[:, 256:256 + qkw], qkg_ref[...])
    reps = qkw // 128
    cos = jnp.concatenate([cos_ref[...]] * reps, axis=1)
    sin = jnp.concatenate([sin_ref[...]] * reps, axis=1)
    first = (lax.broadcasted_iota(jnp.int32, (1, qkw), 1) % HEAD_DIM) < (HEAD_DIM // 2)
    partner = jnp.where(first, pltpu.roll(qk, qkw - HEAD_DIM // 2, 1), pltpu.roll(qk, HEAD_DIM // 2, 1))
    qk = qk * cos + partner * sin
    gq_ref[...] = qk[:, 0:GQA_QW].astype(BF16)
    gk_ref[...] = qk[:, GQA_QW:qkw].astype(BF16)
    gv_ref[...] = z[:, 896:1024].astype(BF16)

    z = _dot(h, w_ref[:, 1024:2048])
    lb = lb_ref[...]
    f = lb + (1.0 - lb) * _sigmoid(z[:, 0:512])
    lf_ref[...] = jnp.log(jnp.maximum(f, TINY))
    hk_ref[...] = (1.0 - f).astype(BF16)
    hv_ref[...] = z[:, 512:768].astype(BF16)
    zq = z[:, 768:1024]
    hq_ref[...] = (zq * _sigmoid(zq)).astype(BF16)

    z = _dot(h, w_ref[:, 2048:3072])
    zg = z[:, 0:256]
    hg_ref[...] = (zg * _sigmoid(zg)).astype(BF16)
    nqk = headnorm(z[:, 256:768], ng_ref[...])
    nq_ref[...] = nqk[:, 0:256].astype(BF16)
    nk_ref[...] = nqk[:, 256:512].astype(BF16)
    nv_ref[...] = z[:, 768:1024].astype(BF16)

    for c in range(N_BRANCH):
        z = _dot(h, w_ref[:, GATE_OFF + c * D_MODEL:GATE_OFF + (c + 1) * D_MODEL])
        gate_ref[:, c * D_MODEL:(c + 1) * D_MODEL] = _sigmoid(z).astype(BF16)


def _inproj(x2, g1, w_in, cos, sin, qk_gain, n_gain, lb, grp, T):
    N = x2.shape[0]
    tm = TM_IN
    tpb = T // tm
    tok = lambda w: pl.BlockSpec((tm, w), lambda i: (i, 0))
    widths = [(POOL_WIDTH, BF16), (GQA_QW, BF16), (GQA_KVW, BF16), (GQA_KVW, BF16), (512, F32), (512, BF16),
              (256, BF16), (256, BF16), (256, BF16), (256, BF16), (256, BF16), (256, BF16), (4 * D_MODEL, BF16)]
    return pl.pallas_call(
        _inproj_kernel,
        grid=(N // tm,),
        in_specs=[
            tok(D_MODEL),
            _const_spec((1, D_MODEL)),
            _const_spec((D_MODEL, IN_WIDTH)),
            pl.BlockSpec((tm, 128), lambda i: (i % tpb, 0)),
            pl.BlockSpec((tm, 128), lambda i: (i % tpb, 0)),
            _const_spec((1, GQA_QW + GQA_KVW)),
            _const_spec((1, 2 * NAT_W)),
            _const_spec((1, 512)),
            _const_spec((640, 640)),
        ],
        out_specs=[tok(w) for w, _ in widths],
        out_shape=[jax.ShapeDtypeStruct((N, w), dt) for w, dt in widths],
        compiler_params=pltpu.CompilerParams(dimension_semantics=("parallel",), vmem_limit_bytes=VMEM_LIMIT),
        name="inproj",
    )(x2, g1, w_in, cos, sin, qk_gain, n_gain, lb, grp)


def _gqa_kernel(sink_ref, q_ref, k_ref, v_ref, o_ref, *, tq, T):
    i = pl.program_id(1)
    t0 = i * tq
    win = tq + 2 * GQA_WINDOW
    w0 = pl.multiple_of(jnp.clip(t0 - GQA_WINDOW, 0, T - win), 128)
    q = q_ref[0]
    k = k_ref[0, pl.ds(w0, win), :]
    v = v_ref[0, pl.ds(w0, win), :]
    qpos = t0 + lax.broadcasted_iota(jnp.int32, (tq, 1), 0)
    kpos = w0 + lax.broadcasted_iota(jnp.int32, (1, win), 1)
    valid = jnp.abs(kpos - qpos) <= GQA_WINDOW
    lane = lax.broadcasted_iota(jnp.int32, (1, 2 * HEAD_DIM), 1)
    half = [jnp.where(lane < HEAD_DIM, 1.0, 0.0).astype(BF16), jnp.where(lane < HEAD_DIM, 0.0, 1.0).astype(BF16)]
    qs = jnp.concatenate([q[:, (j // 2) * 128:(j // 2 + 1) * 128] * half[j % 2] for j in range(GQA_HEADS)], axis=0)
    s_all = _dot_nt(qs, k)
    ps, invs = [], []
    for j in range(GQA_HEADS):
        s = jnp.where(valid, s_all[j * tq:(j + 1) * tq], NEG)
        sink = sink_ref[j]
        m = jnp.maximum(jnp.max(s, axis=-1, keepdims=True), sink)
        p = jnp.exp(s - m)
        invs.append(1.0 / (jnp.sum(p, axis=-1, keepdims=True) + jnp.exp(sink - m)))
        ps.append(p.astype(BF16))
    o_all = _dot(jnp.concatenate(ps, axis=0), v)
    outs = []
    for pair in range(GQA_HEADS // 2):
        j0, j1 = 2 * pair, 2 * pair + 1
        outs.append(jnp.where(lane < HEAD_DIM, o_all[j0 * tq:(j0 + 1) * tq] * invs[j0],
                              o_all[j1 * tq:(j1 + 1) * tq] * invs[j1]))
    o_ref[0] = jnp.concatenate(outs, axis=1).astype(BF16)


def _gqa(q, k, v, sink):
    Bt, T, _ = q.shape
    tq = TQ_GQA
    return pl.pallas_call(
        functools.partial(_gqa_kernel, tq=tq, T=T),
        grid=(Bt, T // tq),
        in_specs=[
            pl.BlockSpec(memory_space=pltpu.SMEM),
            pl.BlockSpec((1, tq, GQA_QW), lambda b, i: (b, i, 0)),
            pl.BlockSpec((1, T, GQA_KVW), lambda b, i: (b, 0, 0)),
            pl.BlockSpec((1, T, GQA_KVW), lambda b, i: (b, 0, 0)),
        ],
        out_specs=pl.BlockSpec((1, tq, GQA_QW), lambda b, i: (b, i, 0)),
        out_shape=jax.ShapeDtypeStruct((Bt, T, GQA_QW), BF16),
        compiler_params=pltpu.CompilerParams(dimension_semantics=("parallel", "arbitrary"),
                                             vmem_limit_bytes=VMEM_LIMIT),
        name="gqa",
    )(sink, q, k, v)


def _nat_kernel(q_ref, k_ref, v_ref, bias_ref, o_ref, *, nblk):
    blk = pl.program_id(1)
    first = jnp.clip(blk - 1, 0, nblk - NAT_KROWS // NAT_QR)
    variant = jnp.where(blk == 0, 0, jnp.where(blk == nblk - 1, 2, 1))
    nq, nk = NAT_QR * GRID_W, NAT_KROWS * GRID_W
    off = pl.multiple_of(first * nq, nq)
    q = q_ref[0]
    k = k_ref[0, pl.ds(off, nk), :]
    v = v_ref[0, pl.ds(off, nk), :]
    lane_head = lax.broadcasted_iota(jnp.int32, (1, NAT_W), 1) // HEAD_DIM
    qs = jnp.concatenate([q * jnp.where(lane_head == h, 1.0, 0.0).astype(BF16) for h in range(NAT_HEADS)], axis=0)
    s_all = _dot_nt(qs, k)
    ps, invs = [], []
    for h in range(NAT_HEADS):
        s = s_all[h * nq:(h + 1) * nq] + bias_ref[variant, h]
        m = jnp.max(s, axis=-1, keepdims=True)
        p = jnp.exp(s - m)
        invs.append(1.0 / jnp.sum(p, axis=-1, keepdims=True))
        ps.append(p.astype(BF16))
    o_all = _dot(jnp.concatenate(ps, axis=0), v)
    out = o_all[0:nq] * invs[0]
    for h in range(1, NAT_HEADS):
        out = jnp.where(lane_head == h, o_all[h * nq:(h + 1) * nq] * invs[h], out)
    o_ref[0] = out.astype(BF16)


def _nat(q, k, v, bias):
    Bt, T, _ = q.shape
    rows = T // GRID_W
    nblk = rows // NAT_QR
    assert rows % NAT_QR == 0 and rows >= NAT_KROWS
    nq = NAT_QR * GRID_W
    return pl.pallas_call(
        functools.partial(_nat_kernel, nblk=nblk),
        grid=(Bt, nblk),
        in_specs=[
            pl.BlockSpec((1, nq, NAT_W), lambda b, r: (b, r, 0)),
            pl.BlockSpec((1, T, NAT_W), lambda b, r: (b, 0, 0)),
            pl.BlockSpec((1, T, NAT_W), lambda b, r: (b, 0, 0)),
            _const_spec((3, NAT_HEADS, nq, NAT_KROWS * GRID_W)),
        ],
        out_specs=pl.BlockSpec((1, nq, NAT_W), lambda b, r: (b, r, 0)),
        out_shape=jax.ShapeDtypeStruct((Bt, T, NAT_W), BF16),
        compiler_params=pltpu.CompilerParams(dimension_semantics=("parallel", "arbitrary"),
                                             vmem_limit_bytes=VMEM_LIMIT),
        name="nat",
    )(q, k, v, bias)


def _nat_bias_table(rpb, rows):
    nblk = rows // NAT_QR
    c = np.arange(GRID_W)[:, None]
    kc = np.arange(GRID_W)[None, :]
    qc0 = np.clip(c - NAT_KC // 2, 0, GRID_W - NAT_KC)
    col_ok = (kc >= qc0) & (kc < qc0 + NAT_KC)
    dcol = np.clip(kc - c + (NAT_KC - 1), 0, 2 * NAT_KC - 2)
    oh_col = (dcol[None] == np.arange(2 * NAT_KC - 1)[:, None, None]) & col_ok[None]
    oh_row = np.zeros((3, NAT_QR, NAT_KROWS, 2 * NAT_KR - 1), np.float32)
    for var, blk in enumerate((0, 1, nblk - 1)):
        first = int(np.clip(blk - 1, 0, nblk - NAT_KROWS // NAT_QR))
        for ri in range(NAT_QR):
            r = blk * NAT_QR + ri
            r0 = int(np.clip(r - NAT_KR // 2, 0, rows - NAT_KR))
            for j in range(NAT_KROWS):
                kr = first * NAT_QR + j
                if r0 <= kr < r0 + NAT_KR:
                    oh_row[var, ri, j, kr - r + NAT_KR - 1] = 1.0
    hp = lax.Precision.HIGHEST
    by_col = jnp.einsum('hde,eck->hdck', rpb.astype(F32), jnp.asarray(oh_col, F32), precision=hp)
    bias = jnp.einsum('vrjd,hdck->vhrcjk', jnp.asarray(oh_row), by_col, precision=hp)
    ok = (oh_row.sum(-1) > 0)[:, None, :, None, :, None] & col_ok[None, None, None, :, None, :]
    bias = jnp.where(jnp.asarray(ok), bias, NEG)
    return bias.reshape(3, NAT_HEADS, NAT_QR * GRID_W, NAT_KROWS * GRID_W)


def _hgrn_kernel(*refs, rev, final, Tt):
    if final:
        lf_ref, k_ref, v_ref, q_ref, oprev_ref, og_ref, gain_ref, grp_ref, o_ref, st_ref = refs
    else:
        lf_ref, k_ref, v_ref, q_ref, o_ref, st_ref = refs
    n = pl.program_id(1)

    @pl.when(n == 0)
    def _():
        st_ref[...] = jnp.zeros_like(st_ref)

    lf = lf_ref[0]
    k = k_ref[0].astype(F32)
    q = q_ref[0].astype(F32)
    v = v_ref[0]
    t_idx = lax.broadcasted_iota(jnp.int32, (Tt, 1), 0)
    tau = (Tt - 1 - t_idx) if rev else t_idx
    row = lax.broadcasted_iota(jnp.int32, (Tt, Tt), 0)
    col = lax.broadcasted_iota(jnp.int32, (Tt, Tt), 1)

    tri = jnp.where((col >= row) if rev else (col <= row), 1.0, 0.0).astype(BF16)
    hi = lf.astype(BF16)
    r1 = lf - hi.astype(F32)
    mid = r1.astype(BF16)
    lo = (r1 - mid.astype(F32)).astype(BF16)
    b = _dot(tri, hi) + _dot(tri, mid) + _dot(tri, lo)

    hs = [slice(h * HEAD_DIM, (h + 1) * HEAD_DIM) for h in range(HGRN_HEADS)]
    qb16 = q_ref[0]
    kb16 = k_ref[0]
    diag = row == col
    acc = [jnp.where(diag, _dot_nt(qb16[:, s], kb16[:, s]), 0.0) for s in hs]

    before = b - lf
    end = b
    size = 2
    while size <= Tt:
        half = size // 2
        upper = (tau % size) >= half
        w = jnp.exp(jnp.where(upper, b - before, end - b))
        qt = jnp.where(upper, q * w, 0.0).astype(BF16)
        kt = jnp.where(upper, 0.0, k * w).astype(BF16)
        same = (row // size) == (col // size)
        for h, s in enumerate(hs):
            acc[h] = acc[h] + jnp.where(same, _dot_nt(qt[:, s], kt[:, s]), 0.0)
        if size < Tt:
            back, fwd = (Tt - half, half) if rev else (half, Tt - half)
            before = jnp.where(upper, pltpu.roll(before, back, 0), before)
            end = jnp.where(upper, end, pltpu.roll(end, fwd, 0))
        size *= 2

    b_last = b[0:1] if rev else b[Tt - 1:Tt]
    q_in = (q * jnp.exp(b)).astype(BF16)
    k_out = (k * jnp.exp(b_last - b)).astype(BF16)
    a_out = jnp.exp(b_last)
    outs = []
    for h, s in enumerate(hs):
        st = st_ref[h]
        o = _dot(acc[h].astype(BF16), v[:, s]) + _dot_nt(q_in[:, s], st.astype(BF16))
        st_ref[h] = st * a_out[:, s] + _dot_tn(v[:, s], k_out[:, s])
        outs.append(o)
    o = jnp.concatenate(outs, axis=1)
    if final:
        o = o + oprev_ref[0].astype(F32)
        msq = _dot((o * o).astype(BF16), grp_ref[...])
        o = o * lax.rsqrt(msq + EPS) * gain_ref[...] * og_ref[0].astype(F32)
    o_ref[0] = o.astype(BF16)


def _hgrn_sweep(lf, hk, hv, hq, *, rev, extra=None):
    Bt, T, _ = hv.shape
    Tt = TT_HGRN
    nt = T // Tt
    d = 1 if rev else 0
    tile = (lambda b, n: (b, nt - 1 - n, 0)) if rev else (lambda b, n: (b, n, 0))
    tile_d = (lambda b, n: (b, nt - 1 - n, d)) if rev else (lambda b, n: (b, n, d))
    in_specs = [
        pl.BlockSpec((1, Tt, HGRN_W), tile_d),
        pl.BlockSpec((1, Tt, HGRN_W), tile_d),
        pl.BlockSpec((1, Tt, HGRN_W), tile),
        pl.BlockSpec((1, Tt, HGRN_W), tile),
    ]
    args = [lf, hk, hv, hq]
    if extra is not None:
        oprev, og, gain, grp = extra
        in_specs += [pl.BlockSpec((1, Tt, HGRN_W), tile), pl.BlockSpec((1, Tt, HGRN_W), tile),
                     _const_spec((1, HGRN_W)), _const_spec((HGRN_W, HGRN_W))]
        args += [oprev, og, gain, grp]
    return pl.pallas_call(
        functools.partial(_hgrn_kernel, rev=rev, final=extra is not None, Tt=Tt),
        grid=(Bt, nt),
        in_specs=in_specs,
        out_specs=pl.BlockSpec((1, Tt, HGRN_W), tile),
        out_shape=jax.ShapeDtypeStruct((Bt, T, HGRN_W), BF16),
        scratch_shapes=[pltpu.VMEM((HGRN_HEADS, HEAD_DIM, HEAD_DIM), F32)],
        compiler_params=pltpu.CompilerParams(dimension_semantics=("parallel", "arbitrary"),
                                             vmem_limit_bytes=VMEM_LIMIT),
        name="hgrn_bwd" if rev else "hgrn_fwd",
    )(*args)


def _merge_kernel(x_ref, u_ref, ob_ref, oc_ref, od_ref, g_ref, pw_ref, ps_ref, wbp_ref, wbg_ref, wbh_ref,
                  wbn_ref, wo_ref, o_ref, *, tm, T):
    i = pl.program_id(1)
    t0 = pl.multiple_of(i * tm, tm)
    win = tm + 2 * POOL_HALO
    w0 = pl.multiple_of(jnp.clip(t0 - POOL_HALO, 0, T - win), POOL_HALO)
    uw = u_ref[0, pl.ds(w0, win), :]
    uc = u_ref[0, pl.ds(t0, tm), :].astype(F32)
    tg = t0 + lax.broadcasted_iota(jnp.int32, (tm, 1), 0)
    sg = w0 + lax.broadcasted_iota(jnp.int32, (1, win), 1)
    lane = lax.broadcasted_iota(jnp.int32, (1, POOL_WIDTH), 1)
    mixed = jnp.zeros((tm, POOL_WIDTH), F32)
    for g, w in enumerate(POOL_WINDOWS):
        lo = jnp.maximum(tg - w // 2, 0)
        hi = jnp.minimum(tg + w // 2, T)
        inside = jnp.where(sg >= lo, jnp.where(sg < hi, 1.0, 0.0), 0.0).astype(BF16)
        mean = _dot(inside, uw) / (hi - lo).astype(F32)
        mixed = jnp.where(lane // POOL_GW == g, mean, mixed)
    mixed = mixed - uc
    oa = _dot(mixed.astype(BF16), pw_ref[...]) * ps_ref[...]
    gate = lambda c: g_ref[0, :, c * D_MODEL:(c + 1) * D_MODEL].astype(F32)
    merged = (gate(0) * _dot(oa.astype(BF16), wbp_ref[...])
              + gate(1) * _dot(ob_ref[0], wbg_ref[...])
              + gate(2) * _dot(oc_ref[0], wbh_ref[...])
              + gate(3) * _dot(od_ref[0], wbn_ref[...]))
    o_ref[0] = x_ref[0] + _dot(merged.astype(BF16), wo_ref[...])


def _merge(x, u, ob, oc, od, gates, pw, ps, wbp, wbg, wbh, wbn, wo):
    Bt, T, _ = x.shape
    tm = TM_MERGE
    tok = lambda w: pl.BlockSpec((1, tm, w), lambda b, i: (b, i, 0))
    return pl.pallas_call(
        functools.partial(_merge_kernel, tm=tm, T=T),
        grid=(Bt, T // tm),
        in_specs=[
            tok(D_MODEL),
            pl.BlockSpec((1, T, POOL_WIDTH), lambda b, i: (b, 0, 0)),
            tok(GQA_QW), tok(HGRN_W), tok(NAT_W), tok(N_BRANCH * D_MODEL),
            _const_spec((POOL_WIDTH, POOL_WIDTH)), _const_spec((1, POOL_WIDTH)),
            _const_spec((POOL_WIDTH, D_MODEL)), _const_spec((GQA_QW, D_MODEL)),
            _const_spec((HGRN_W, D_MODEL)), _const_spec((NAT_W, D_MODEL)),
            _const_spec((D_MODEL, D_MODEL)),
        ],
        out_specs=tok(D_MODEL),
        out_shape=jax.ShapeDtypeStruct((Bt, T, D_MODEL), F32),
        compiler_params=pltpu.CompilerParams(dimension_semantics=("parallel", "arbitrary"),
                                             vmem_limit_bytes=VMEM_LIMIT),
        name="merge",
    )(x, u, ob, oc, od, gates, pw, ps, wbp, wbg, wbh, wbn, wo)


def _mlp_kernel(x_ref, g_ref, wu_ref, wd_ref, o_ref):
    x = x_ref[...]
    ms = jnp.mean(x * x, axis=-1, keepdims=True)
    h = (x * lax.rsqrt(ms + EPS) * g_ref[...]).astype(BF16)
    acc = x
    for c in range(D_FF // D_MODEL):
        sl = slice(c * D_MODEL, (c + 1) * D_MODEL)
        hid = jnp.maximum(_dot(h, wu_ref[:, sl]), 0.0)
        acc = acc + _dot((hid * hid).astype(BF16), wd_ref[sl, :])
    o_ref[...] = acc


def _mlp(x2, g2, wu, wd):
    N = x2.shape[0]
    tm = TM_MLP
    return pl.pallas_call(
        _mlp_kernel,
        grid=(N // tm,),
        in_specs=[pl.BlockSpec((tm, D_MODEL), lambda i: (i, 0)), _const_spec((1, D_MODEL)),
                  _const_spec((D_MODEL, D_FF)), _const_spec((D_FF, D_MODEL))],
        out_specs=pl.BlockSpec((tm, D_MODEL), lambda i: (i, 0)),
        out_shape=jax.ShapeDtypeStruct((N, D_MODEL), F32),
        compiler_params=pltpu.CompilerParams(dimension_semantics=("parallel",), vmem_limit_bytes=VMEM_LIMIT),
        name="mlp",
    )(x2, g2, wu, wd)


def _block_diag(blocks):
    n, a, b = blocks.shape
    eye = jnp.eye(n, dtype=blocks.dtype)
    return (eye[:, None, :, None] * blocks[:, :, None, :]).reshape(n * a, n * b)


def _trunk(x, norm1_g, w_in, pool_w, pool_scale, gqa_qnorm, gqa_knorm, gqa_sink, hgrn_lb, hgrn_onorm,
           nat_qnorm, nat_knorm, nat_rpb, w_br_pool, w_br_gqa, w_br_hgrn, w_br_nat, w_o, norm2_g, w_up, w_down):
    Bt, T, D = x.shape
    N = Bt * T
    depth = w_in.shape[0]
    half = HEAD_DIM // 2
    inv = ROPE_THETA ** (-jnp.arange(half, dtype=F32) / half)
    ang = jnp.arange(T, dtype=F32)[:, None] * inv[None, :]
    cos = jnp.tile(jnp.concatenate([jnp.cos(ang), jnp.cos(ang)], axis=1), (1, 2))
    sin = jnp.tile(jnp.concatenate([-jnp.sin(ang), jnp.sin(ang)], axis=1), (1, 2))
    sm = jax.nn.softmax(hgrn_lb.astype(F32), axis=0)
    lower = jnp.cumsum(sm, axis=0) - sm[:1]
    grp = _block_diag(jnp.full((640 // HEAD_DIM, HEAD_DIM, HEAD_DIM), 1.0 / HEAD_DIM, F32)).astype(BF16)
    scale = HEAD_DIM ** -0.5
    order = jnp.asarray(GQA_HEAD_ORDER)

    for l in range(depth):
        qk_gain = jnp.concatenate([jnp.tile(gqa_qnorm[l] * scale, GQA_HEADS), jnp.tile(gqa_knorm[l], GQA_KV_HEADS)])
        n_gain = jnp.concatenate([jnp.tile(nat_qnorm[l] * scale, NAT_HEADS), jnp.tile(nat_knorm[l], NAT_HEADS)])
        w_l = w_in[l].astype(BF16)
        w_q = w_l[:, POOL_WIDTH:POOL_WIDTH + GQA_QW].reshape(D, GQA_HEADS, HEAD_DIM)[:, order, :].reshape(D, GQA_QW)
        w_l = jnp.concatenate([w_l[:, :POOL_WIDTH], w_q, w_l[:, POOL_WIDTH + GQA_QW:]], axis=1)
        w_bg = w_br_gqa[l].astype(BF16).reshape(GQA_HEADS, HEAD_DIM, D)[order].reshape(GQA_QW, D)
        (up, gq, gk, gv, lf, hk, hv, hq, hg, nq, nk, nv, gates) = _inproj(
            x.reshape(N, D), norm1_g[l][None, :], w_l, cos, sin, qk_gain[None, :], n_gain[None, :],
            lower[l].reshape(1, 2 * HGRN_W), grp, T)
        r3 = lambda a: a.reshape(Bt, T, a.shape[-1])
        ob = _gqa(r3(gq), r3(gk), r3(gv), gqa_sink[l].astype(F32)[order])
        od = _nat(r3(nq), r3(nk), r3(nv), _nat_bias_table(nat_rpb[l], T // GRID_W))
        o_f = _hgrn_sweep(r3(lf), r3(hk), r3(hv), r3(hq), rev=False)
        oc = _hgrn_sweep(r3(lf), r3(hk), r3(hv), r3(hq), rev=True,
                         extra=(o_f, r3(hg), jnp.tile(hgrn_onorm[l], HGRN_HEADS)[None, :], grp[:HGRN_W, :HGRN_W]))
        x = _merge(x, r3(up), ob, oc, od, r3(gates),
                   _block_diag(pool_w[l]).astype(BF16), pool_scale[l][None, :],
                   w_br_pool[l].astype(BF16), w_bg, w_br_hgrn[l].astype(BF16),
                   w_br_nat[l].astype(BF16), w_o[l].astype(BF16))
        x = _mlp(x.reshape(N, D), norm2_g[l][None, :], w_up[l].astype(BF16), w_down[l].astype(BF16)).reshape(Bt, T, D)
    return x


def kernel(x_prompt, x_sample, norm1_g, w_in, pool_w, pool_scale, gqa_qnorm, gqa_knorm, gqa_sink, hgrn_lb,
           hgrn_onorm, nat_qnorm, nat_knorm, nat_rpb, w_br_pool, w_br_gqa, w_br_hgrn, w_br_nat, w_o, norm2_g,
           w_up, w_down):
    nb = x_prompt.shape[0]
    x = jnp.concatenate([x_prompt, x_sample], axis=0)
    y = _trunk(x, norm1_g, w_in, pool_w, pool_scale, gqa_qnorm, gqa_knorm, gqa_sink, hgrn_lb, hgrn_onorm,
               nat_qnorm, nat_knorm, nat_rpb, w_br_pool, w_br_gqa, w_br_hgrn, w_br_nat, w_o, norm2_g, w_up, w_down)
    return (y[:nb], y[nb:])
```

```python
import functools

import numpy as np

import jax
import jax.numpy as jnp
from jax import lax
from jax.experimental import pallas as pl
from jax.experimental.pallas import tpu as pltpu

F32 = jnp.float32
BF16 = jnp.bfloat16

D_MODEL = 1024
GRID_W = 64
EPS = 1e-6
NEG = -1e30
TINY = 1e-30
HEAD_DIM = 64
ROPE_THETA = 10000.0
N_BRANCH = 4
D_FF = 4 * D_MODEL

POOL_WIDTH = 256
POOL_WINDOWS = (2, 4, 8, 16)
POOL_GW = 64
POOL_HALO = 16

GQA_HEADS = 8
GQA_KV_HEADS = 2
GQA_GROUP = GQA_HEADS // GQA_KV_HEADS
GQA_WINDOW = 128
GQA_QW = GQA_HEADS * HEAD_DIM
GQA_KVW = GQA_KV_HEADS * HEAD_DIM

HGRN_HEADS = 4
HGRN_W = HGRN_HEADS * HEAD_DIM

NAT_HEADS = 4
NAT_KR = 8
NAT_KC = 16
NAT_W = NAT_HEADS * HEAD_DIM
NAT_QR = 4
NAT_KROWS = 12
GQA_HEAD_ORDER = tuple((j % 2) * GQA_GROUP + j // 2 for j in range(GQA_HEADS))

IN_WIDTH = 7168
GATE_OFF = 3072

TM_IN = 512
TQ_GQA = 128
TT_HGRN = 128
HGRN_CHUNK = 32
HGRN_SAFE = 60.0
TM_MERGE = 256
TM_MLP = 512
VMEM_LIMIT = 56 * 1024 * 1024


def _sigmoid(x):
    return 1.0 / (1.0 + jnp.exp(-x))


def _dot(a, b):
    return jnp.dot(a, b, preferred_element_type=F32)


def _dot_nt(a, b):
    return lax.dot_general(a, b, (((1,), (1,)), ((), ())), preferred_element_type=F32)


def _dot_tn(a, b):
    return lax.dot_general(a, b, (((0,), (0,)), ((), ())), preferred_element_type=F32)


def _const_spec(shape):
    n = len(shape)
    return pl.BlockSpec(shape, lambda *_: (0,) * n, pipeline_mode=pl.Buffered(1))


def _inproj_kernel(x_ref, g_ref, w_ref, cos_ref, sin_ref, qkg_ref, ng_ref, lb_ref, grp_ref,
                   up_ref, gq_ref, gk_ref, gv_ref, lf_ref, hk_ref, hv_ref, hq_ref, hg_ref,
                   nq_ref, nk_ref, nv_ref, gate_ref):
    x = x_ref[...]
    ms = jnp.mean(x * x, axis=-1, keepdims=True)
    h = (x * lax.rsqrt(ms + EPS) * g_ref[...]).astype(BF16)

    def headnorm(y, gain):
        w = y.shape[1]
        msq = _dot((y * y).astype(BF16), grp_ref[0:w, 0:w])
        return y * lax.rsqrt(msq + EPS) * gain

    z = _dot(h, w_ref[:, 0:1024])
    up_ref[...] = z[:, 0:POOL_WIDTH].astype(BF16)
    qkw = GQA_QW + GQA_KVW
    qk = headnorm(z[:, 256:256 + qkw], qkg_ref[...])
    reps = qkw // 128
    cos = jnp.concatenate([cos_ref[...]] * reps, axis=1)
    sin = jnp.concatenate([sin_ref[...]] * reps, axis=1)
    first = (lax.broadcasted_iota(jnp.int32, (1, qkw), 1) % HEAD_DIM) < (HEAD_DIM // 2)
    partner = jnp.where(first, pltpu.roll(qk, qkw - HEAD_DIM // 2, 1), pltpu.roll(qk, HEAD_DIM // 2, 1))
    qk = qk * cos + partner * sin
    gq_ref[...] = qk[:, 0:GQA_QW].astype(BF16)
    gk_ref[...] = qk[:, GQA_QW:qkw].astype(BF16)
    gv_ref[...] = z[:, 896:1024].astype(BF16)

    z = _dot(h, w_ref[:, 1024:2048])
    lb = lb_ref[...]
    f = lb + (1.0 - lb) * _sigmoid(z[:, 0:512])
    lf_ref[...] = jnp.log(jnp.maximum(f, TINY))
    hk_ref[...] = (1.0 - f).astype(BF16)
    hv_ref[...] = z[:, 512:768].astype(BF16)
    zq = z[:, 768:1024]
    hq_ref[...] = (zq * _sigmoid(zq)).astype(BF16)

    z = _dot(h, w_ref[:, 2048:3072])
    zg = z[:, 0:256]
    hg_ref[...] = (zg * _sigmoid(zg)).astype(BF16)
    nqk = headnorm(z[:, 256:768], ng_ref[...])
    nq_ref[...] = nqk[:, 0:256].astype(BF16)
    nk_ref[...] = nqk[:, 256:512].astype(BF16)
    nv_ref[...] = z[:, 768:1024].astype(BF16)

    for c in range(N_BRANCH):
        z = _dot(h, w_ref[:, GATE_OFF + c * D_MODEL:GATE_OFF + (c + 1) * D_MODEL])
        gate_ref[:, c * D_MODEL:(c + 1) * D_MODEL] = _sigmoid(z).astype(BF16)


def _inproj(x2, g1, w_in, cos, sin, qk_gain, n_gain, lb, grp, T):
    N = x2.shape[0]
    tm = TM_IN
    tpb = T // tm
    tok = lambda w: pl.BlockSpec((tm, w), lambda i: (i, 0))
    widths = [(POOL_WIDTH, BF16), (GQA_QW, BF16), (GQA_KVW, BF16), (GQA_KVW, BF16), (512, F32), (512, BF16),
              (256, BF16), (256, BF16), (256, BF16), (256, BF16), (256, BF16), (256, BF16), (4 * D_MODEL, BF16)]
    return pl.pallas_call(
        _inproj_kernel,
        grid=(N // tm,),
        in_specs=[
            tok(D_MODEL),
            _const_spec((1, D_MODEL)),
            _const_spec((D_MODEL, IN_WIDTH)),
            pl.BlockSpec((tm, 128), lambda i: (i % tpb, 0)),
            pl.BlockSpec((tm, 128), lambda i: (i % tpb, 0)),
            _const_spec((1, GQA_QW + GQA_KVW)),
            _const_spec((1, 2 * NAT_W)),
            _const_spec((1, 512)),
            _const_spec((640, 640)),
        ],
        out_specs=[tok(w) for w, _ in widths],
        out_shape=[jax.ShapeDtypeStruct((N, w), dt) for w, dt in widths],
        compiler_params=pltpu.CompilerParams(dimension_semantics=("parallel",), vmem_limit_bytes=VMEM_LIMIT),
        name="inproj",
    )(x2, g1, w_in, cos, sin, qk_gain, n_gain, lb, grp)


def _gqa_kernel(sink_ref, q_ref, k_ref, v_ref, o_ref, *, tq, T):
    i = pl.program_id(1)
    t0 = i * tq
    win = tq + 2 * GQA_WINDOW
    w0 = pl.multiple_of(jnp.clip(t0 - GQA_WINDOW, 0, T - win), 128)
    q = q_ref[0]
    k = k_ref[0, pl.ds(w0, win), :]
    v = v_ref[0, pl.ds(w0, win), :]
    qpos = t0 + lax.broadcasted_iota(jnp.int32, (tq, 1), 0)
    kpos = w0 + lax.broadcasted_iota(jnp.int32, (1, win), 1)
    valid = jnp.abs(kpos - qpos) <= GQA_WINDOW
    lane = lax.broadcasted_iota(jnp.int32, (1, 2 * HEAD_DIM), 1)
    half = [jnp.where(lane < HEAD_DIM, 1.0, 0.0).astype(BF16), jnp.where(lane < HEAD_DIM, 0.0, 1.0).astype(BF16)]
    qs = jnp.concatenate([q[:, (j // 2) * 128:(j // 2 + 1) * 128] * half[j % 2] for j in range(GQA_HEADS)], axis=0)
    s_all = _dot_nt(qs, k)
    ps, invs = [], []
    for j in range(GQA_HEADS):
        s = jnp.where(valid, s_all[j * tq:(j + 1) * tq], NEG)
        sink = sink_ref[j]
        m = jnp.maximum(jnp.max(s, axis=-1, keepdims=True), sink)
        p = jnp.exp(s - m)
        invs.append(1.0 / (jnp.sum(p, axis=-1, keepdims=True) + jnp.exp(sink - m)))
        ps.append(p.astype(BF16))
    o_all = _dot(jnp.concatenate(ps, axis=0), v)
    outs = []
    for pair in range(GQA_HEADS // 2):
        j0, j1 = 2 * pair, 2 * pair + 1
        outs.append(jnp.where(lane < HEAD_DIM, o_all[j0 * tq:(j0 + 1) * tq] * invs[j0],
                              o_all[j1 * tq:(j1 + 1) * tq] * invs[j1]))
    o_ref[0] = jnp.concatenate(outs, axis=1).astype(BF16)


def _gqa(q, k, v, sink):
    Bt, T, _ = q.shape
    tq = TQ_GQA
    return pl.pallas_call(
        functools.partial(_gqa_kernel, tq=tq, T=T),
        grid=(Bt, T // tq),
        in_specs=[
            pl.BlockSpec(memory_space=pltpu.SMEM),
            pl.BlockSpec((1, tq, GQA_QW), lambda b, i: (b, i, 0)),
            pl.BlockSpec((1, T, GQA_KVW), lambda b, i: (b, 0, 0)),
            pl.BlockSpec((1, T, GQA_KVW), lambda b, i: (b, 0, 0)),
        ],
        out_specs=pl.BlockSpec((1, tq, GQA_QW), lambda b, i: (b, i, 0)),
        out_shape=jax.ShapeDtypeStruct((Bt, T, GQA_QW), BF16),
        compiler_params=pltpu.CompilerParams(dimension_semantics=("parallel", "arbitrary"),
                                             vmem_limit_bytes=VMEM_LIMIT),
        name="gqa",
    )(sink, q, k, v)


def _nat_kernel(q_ref, k_ref, v_ref, bias_ref, o_ref, *, nblk):
    blk = pl.program_id(1)
    first = jnp.clip(blk - 1, 0, nblk - NAT_KROWS // NAT_QR)
    variant = jnp.where(blk == 0, 0, jnp.where(blk == nblk - 1, 2, 1))
    nq, nk = NAT_QR * GRID_W, NAT_KROWS * GRID_W
    off = pl.multiple_of(first * nq, nq)
    q = q_ref[0]
    k = k_ref[0, pl.ds(off, nk), :]
    v = v_ref[0, pl.ds(off, nk), :]
    lane_head = lax.broadcasted_iota(jnp.int32, (1, NAT_W), 1) // HEAD_DIM
    qs = jnp.concatenate([q * jnp.where(lane_head == h, 1.0, 0.0).astype(BF16) for h in range(NAT_HEADS)], axis=0)
    s_all = _dot_nt(qs, k)
    ps, invs = [], []
    for h in range(NAT_HEADS):
        s = s_all[h * nq:(h + 1) * nq] + bias_ref[variant, h]
        m = jnp.max(s, axis=-1, keepdims=True)
        p = jnp.exp(s - m)
        invs.append(1.0 / jnp.sum(p, axis=-1, keepdims=True))
        ps.append(p.astype(BF16))
    o_all = _dot(jnp.concatenate(ps, axis=0), v)
    out = o_all[0:nq] * invs[0]
    for h in range(1, NAT_HEADS):
        out = jnp.where(lane_head == h, o_all[h * nq:(h + 1) * nq] * invs[h], out)
    o_ref[0] = out.astype(BF16)


def _nat(q, k, v, bias):
    Bt, T, _ = q.shape
    rows = T // GRID_W
    nblk = rows // NAT_QR
    assert rows % NAT_QR == 0 and rows >= NAT_KROWS
    nq = NAT_QR * GRID_W
    return pl.pallas_call(
        functools.partial(_nat_kernel, nblk=nblk),
        grid=(Bt, nblk),
        in_specs=[
            pl.BlockSpec((1, nq, NAT_W), lambda b, r: (b, r, 0)),
            pl.BlockSpec((1, T, NAT_W), lambda b, r: (b, 0, 0)),
            pl.BlockSpec((1, T, NAT_W), lambda b, r: (b, 0, 0)),
            _const_spec((3, NAT_HEADS, nq, NAT_KROWS * GRID_W)),
        ],
        out_specs=pl.BlockSpec((1, nq, NAT_W), lambda b, r: (b, r, 0)),
        out_shape=jax.ShapeDtypeStruct((Bt, T, NAT_W), BF16),
        compiler_params=pltpu.CompilerParams(dimension_semantics=("parallel", "arbitrary"),
                                             vmem_limit_bytes=VMEM_LIMIT),
        name="nat",
    )(q, k, v, bias)


def _nat_bias_table(rpb, rows):
    nblk = rows // NAT_QR
    c = np.arange(GRID_W)[:, None]
    kc = np.arange(GRID_W)[None, :]
    qc0 = np.clip(c - NAT_KC // 2, 0, GRID_W - NAT_KC)
    col_ok = (kc >= qc0) & (kc < qc0 + NAT_KC)
    dcol = np.clip(kc - c + (NAT_KC - 1), 0, 2 * NAT_KC - 2)
    oh_col = (dcol[None] == np.arange(2 * NAT_KC - 1)[:, None, None]) & col_ok[None]
    oh_row = np.zeros((3, NAT_QR, NAT_KROWS, 2 * NAT_KR - 1), np.float32)
    for var, blk in enumerate((0, 1, nblk - 1)):
        first = int(np.clip(blk - 1, 0, nblk - NAT_KROWS // NAT_QR))
        for ri in range(NAT_QR):
            r = blk * NAT_QR + ri
            r0 = int(np.clip(r - NAT_KR // 2, 0, rows - NAT_KR))
            for j in range(NAT_KROWS):
                kr = first * NAT_QR + j
                if r0 <= kr < r0 + NAT_KR:
                    oh_row[var, ri, j, kr - r + NAT_KR - 1] = 1.0
    hp = lax.Precision.HIGHEST
    by_col = jnp.einsum('hde,eck->hdck', rpb.astype(F32), jnp.asarray(oh_col, F32), precision=hp)
    bias = jnp.einsum('vrjd,hdck->vhrcjk', jnp.asarray(oh_row), by_col, precision=hp)
    ok = (oh_row.sum(-1) > 0)[:, None, :, None, :, None] & col_ok[None, None, None, :, None, :]
    bias = jnp.where(jnp.asarray(ok), bias, NEG)
    return bias.reshape(3, NAT_HEADS, NAT_QR * GRID_W, NAT_KROWS * GRID_W)


def _hgrn_consts(Tt, C, rev):
    t = np.arange(Tt)
    row, col = t[:, None], t[None, :]
    same = ((row // C) == (col // C)) & ((col >= row) if rev else (col <= row))
    pair = (row // (2 * C)) == (col // (2 * C))
    masks = np.stack([np.tile(same, (HGRN_HEADS, 1)), np.tile(pair, (HGRN_HEADS, 1))]).astype(np.float32)
    hl = np.arange(HGRN_W) // HEAD_DIM
    bd = (hl[:, None] == hl[None, :]).astype(np.float32)
    return jnp.asarray(same.astype(np.float32), BF16), jnp.asarray(masks), jnp.asarray(bd)


def _hgrn_kernel(*refs, rev, final, Tt, C):
    if final:
        (lf_ref, k_ref, v_ref, q_ref, tri_ref, mask_ref, bd_ref, oprev_ref, og_ref, gain_ref, grp_ref,
         o_ref, st_ref) = refs
    else:
        lf_ref, k_ref, v_ref, q_ref, tri_ref, mask_ref, bd_ref, o_ref, st_ref = refs
    n = pl.program_id(1)
    nc = Tt // C

    @pl.when(n == 0)
    def _():
        st_ref[...] = jnp.zeros_like(st_ref)

    lf = lf_ref[0]
    k = k_ref[0].astype(F32)
    q = q_ref[0].astype(F32)
    v = v_ref[0]
    lane_head = lax.broadcasted_iota(jnp.int32, (1, HGRN_W), 1) // HEAD_DIM
    head_sel = [jnp.where(lane_head == h, 1.0, 0.0).astype(BF16) for h in range(HGRN_HEADS)]

    def stack_heads(x16):
        return jnp.concatenate([x16 * m for m in head_sel], axis=0)

    tri = tri_ref[...]
    hi = lf.astype(BF16)
    lo = (lf - hi.astype(F32)).astype(BF16)
    bc = _dot(tri, hi) + _dot(tri, lo)

    cs = [slice(c * C, (c + 1) * C) for c in range(nc)]
    pos = [(nc - 1 - c) if rev else c for c in range(nc)]
    at = [pos.index(p) for p in range(nc)]
    last = 0 if rev else C - 1
    tot = [bc[c * C + last:c * C + last + 1] for c in range(nc)]
    qa = [q[cs[c]] * jnp.exp(bc[cs[c]]) for c in range(nc)]
    kb = [k[cs[c]] * jnp.exp(tot[c] - bc[cs[c]]) for c in range(nc)]
    zero = jnp.zeros((C, HGRN_W), F32)
    cat = lambda xs: jnp.concatenate(xs, axis=0).astype(BF16)

    q_pair = cat([qa[c] if pos[c] % 2 == 1 else zero for c in range(nc)])
    k_pair = cat([kb[c] if pos[c] % 2 == 0 else zero for c in range(nc)])
    g_q = jnp.exp(tot[at[2]])
    g_k = jnp.exp(tot[at[1]])
    q_half = cat([zero if pos[c] < 2 else (qa[c] * g_q if pos[c] == 3 else qa[c]) for c in range(nc)])
    k_half = cat([zero if pos[c] >= 2 else (kb[c] * g_k if pos[c] == 0 else kb[c]) for c in range(nc)])
    a_cross = (jnp.where(mask_ref[1] > 0.5, _dot_nt(stack_heads(q_pair), k_pair), 0.0)
               + _dot_nt(stack_heads(q_half), k_half))

    def intra_fast():
        ka = cat([k[cs[c]] * jnp.exp(-bc[cs[c]]) for c in range(nc)])
        return jnp.where(mask_ref[0] > 0.5, _dot_nt(stack_heads(cat(qa)), ka), 0.0)

    def intra_safe():
        t_idx = lax.broadcasted_iota(jnp.int32, (Tt, 1), 0)
        tau = (Tt - 1 - t_idx) if rev else t_idx
        row = lax.broadcasted_iota(jnp.int32, (HGRN_HEADS * Tt, Tt), 0) % Tt
        col = lax.broadcasted_iota(jnp.int32, (HGRN_HEADS * Tt, Tt), 1)
        acc = jnp.where(row == col, _dot_nt(stack_heads(q_ref[0]), k_ref[0]), 0.0)
        before = bc - lf
        end = bc
        size = 2
        while size <= C:
            half = size // 2
            upper = (tau % size) >= half
            w = jnp.exp(jnp.where(upper, bc - before, end - bc))
            qt = jnp.where(upper, q * w, 0.0).astype(BF16)
            kt = jnp.where(upper, 0.0, k * w).astype(BF16)
            acc = acc + jnp.where((row // size) == (col // size), _dot_nt(stack_heads(qt), kt), 0.0)
            if size < C:
                back, fwd = (Tt - half, half) if rev else (half, Tt - half)
                before = jnp.where(upper, pltpu.roll(before, back, 0), before)
                end = jnp.where(upper, end, pltpu.roll(end, fwd, 0))
            size *= 2
        return acc

    tmin = functools.reduce(jnp.minimum, tot)
    a_all = a_cross + lax.cond(jnp.min(tmin) >= -HGRN_SAFE, intra_fast, intra_safe)

    pre, suf = [None] * nc, [None] * nc
    run = jnp.zeros_like(tot[0])
    for p in range(nc):
        pre[at[p]] = run
        run = run + tot[at[p]]
    b_tile = run
    run = jnp.zeros_like(tot[0])
    for p in reversed(range(nc)):
        suf[at[p]] = run
        run = run + tot[at[p]]
    q_in = cat([qa[c] * jnp.exp(pre[c]) for c in range(nc)])
    k_out = cat([kb[c] * jnp.exp(suf[c]) for c in range(nc)])

    st = st_ref[...]
    o_heads = _dot(a_all.astype(BF16), v)
    o = _dot_nt(q_in, st.astype(BF16))
    for h in range(HGRN_HEADS):
        o = o + jnp.where(lane_head == h, o_heads[h * Tt:(h + 1) * Tt], 0.0)
    st_ref[...] = st * jnp.exp(b_tile) + jnp.where(bd_ref[...] > 0.5, _dot_tn(v, k_out), 0.0)
    if final:
        o = o + oprev_ref[0].astype(F32)
        msq = _dot((o * o).astype(BF16), grp_ref[...])
        o = o * lax.rsqrt(msq + EPS) * gain_ref[...] * og_ref[0].astype(F32)
    o_ref[0] = o.astype(BF16)


def _hgrn_sweep(lf, hk, hv, hq, *, rev, extra=None):
    Bt, T, _ = hv.shape
    Tt, C = TT_HGRN, HGRN_CHUNK
    assert Tt == 4 * C
    nt = T // Tt
    d = 1 if rev else 0
    tile = (lambda b, n: (b, nt - 1 - n, 0)) if rev else (lambda b, n: (b, n, 0))
    tile_d = (lambda b, n: (b, nt - 1 - n, d)) if rev else (lambda b, n: (b, n, d))
    tri, masks, bd = _hgrn_consts(Tt, C, rev)
    in_specs = [
        pl.BlockSpec((1, Tt, HGRN_W), tile_d),
        pl.BlockSpec((1, Tt, HGRN_W), tile_d),
        pl.BlockSpec((1, Tt, HGRN_W), tile),
        pl.BlockSpec((1, Tt, HGRN_W), tile),
        _const_spec((Tt, Tt)), _const_spec((2, HGRN_HEADS * Tt, Tt)), _const_spec((HGRN_W, HGRN_W)),
    ]
    args = [lf, hk, hv, hq, tri, masks, bd]
    if extra is not None:
        oprev, og, gain, grp = extra
        in_specs += [pl.BlockSpec((1, Tt, HGRN_W), tile), pl.BlockSpec((1, Tt, HGRN_W), tile),
                     _const_spec((1, HGRN_W)), _const_spec((HGRN_W, HGRN_W))]
        args += [oprev, og, gain, grp]
    return pl.pallas_call(
        functools.partial(_hgrn_kernel, rev=rev, final=extra is not None, Tt=Tt, C=C),
        grid=(Bt, nt),
        in_specs=in_specs,
        out_specs=pl.BlockSpec((1, Tt, HGRN_W), tile),
        out_shape=jax.ShapeDtypeStruct((Bt, T, HGRN_W), BF16),
        scratch_shapes=[pltpu.VMEM((HGRN_W, HGRN_W), F32)],
        compiler_params=pltpu.CompilerParams(dimension_semantics=("parallel", "arbitrary"),
                                             vmem_limit_bytes=VMEM_LIMIT),
        name="hgrn_bwd" if rev else "hgrn_fwd",
    )(*args)


def _merge_kernel(x_ref, u_ref, ob_ref, oc_ref, od_ref, g_ref, pw_ref, ps_ref, wbp_ref, wbg_ref, wbh_ref,
                  wbn_ref, wo_ref, o_ref, *, tm, T):
    i = pl.program_id(1)
    t0 = pl.multiple_of(i * tm, tm)
    win = tm + 2 * POOL_HALO
    w0 = pl.multiple_of(jnp.clip(t0 - POOL_HALO, 0, T - win), POOL_HALO)
    uw = u_ref[0, pl.ds(w0, win), :]
    uc = u_ref[0, pl.ds(t0, tm), :].astype(F32)
    tg = t0 + lax.broadcasted_iota(jnp.int32, (tm, 1), 0)
    sg = w0 + lax.broadcasted_iota(jnp.int32, (1, win), 1)
    lane = lax.broadcasted_iota(jnp.int32, (1, POOL_WIDTH), 1)
    mixed = jnp.zeros((tm, POOL_WIDTH), F32)
    for g, w in enumerate(POOL_WINDOWS):
        lo = jnp.maximum(tg - w // 2, 0)
        hi = jnp.minimum(tg + w // 2, T)
        inside = jnp.where(sg >= lo, jnp.where(sg < hi, 1.0, 0.0), 0.0).astype(BF16)
        mean = _dot(inside, uw) / (hi - lo).astype(F32)
        mixed = jnp.where(lane // POOL_GW == g, mean, mixed)
    mixed = mixed - uc
    oa = _dot(mixed.astype(BF16), pw_ref[...]) * ps_ref[...]
    gate = lambda c: g_ref[0, :, c * D_MODEL:(c + 1) * D_MODEL].astype(F32)
    merged = (gate(0) * _dot(oa.astype(BF16), wbp_ref[...])
              + gate(1) * _dot(ob_ref[0], wbg_ref[...])
              + gate(2) * _dot(oc_ref[0], wbh_ref[...])
              + gate(3) * _dot(od_ref[0], wbn_ref[...]))
    o_ref[0] = x_ref[0] + _dot(merged.astype(BF16), wo_ref[...])


def _merge(x, u, ob, oc, od, gates, pw, ps, wbp, wbg, wbh, wbn, wo):
    Bt, T, _ = x.shape
    tm = TM_MERGE
    tok = lambda w: pl.BlockSpec((1, tm, w), lambda b, i: (b, i, 0))
    return pl.pallas_call(
        functools.partial(_merge_kernel, tm=tm, T=T),
        grid=(Bt, T // tm),
        in_specs=[
            tok(D_MODEL),
            pl.BlockSpec((1, T, POOL_WIDTH), lambda b, i: (b, 0, 0)),
            tok(GQA_QW), tok(HGRN_W), tok(NAT_W), tok(N_BRANCH * D_MODEL),
            _const_spec((POOL_WIDTH, POOL_WIDTH)), _const_spec((1, POOL_WIDTH)),
            _const_spec((POOL_WIDTH, D_MODEL)), _const_spec((GQA_QW, D_MODEL)),
            _const_spec((HGRN_W, D_MODEL)), _const_spec((NAT_W, D_MODEL)),
            _const_spec((D_MODEL, D_MODEL)),
        ],
        out_specs=tok(D_MODEL),
        out_shape=jax.ShapeDtypeStruct((Bt, T, D_MODEL), F32),
        compiler_params=pltpu.CompilerParams(dimension_semantics=("parallel", "arbitrary"),
                                             vmem_limit_bytes=VMEM_LIMIT),
        name="merge",
    )(x, u, ob, oc, od, gates, pw, ps, wbp, wbg, wbh, wbn, wo)


def _mlp_kernel(x_ref, g_ref, wu_ref, wd_ref, o_ref):
    x = x_ref[...]
    ms = jnp.mean(x * x, axis=-1, keepdims=True)
    h = (x * lax.rsqrt(ms + EPS) * g_ref[...]).astype(BF16)
    acc = x
    for c in range(D_FF // D_MODEL):
        sl = slice(c * D_MODEL, (c + 1) * D_MODEL)
        hid = jnp.maximum(_dot(h, wu_ref[:, sl]), 0.0)
        acc = acc + _dot((hid * hid).astype(BF16), wd_ref[sl, :])
    o_ref[...] = acc


def _mlp(x2, g2, wu, wd):
    N = x2.shape[0]
    tm = TM_MLP
    return pl.pallas_call(
        _mlp_kernel,
        grid=(N // tm,),
        in_specs=[pl.BlockSpec((tm, D_MODEL), lambda i: (i, 0)), _const_spec((1, D_MODEL)),
                  _const_spec((D_MODEL, D_FF)), _const_spec((D_FF, D_MODEL))],
        out_specs=pl.BlockSpec((tm, D_MODEL), lambda i: (i, 0)),
        out_shape=jax.ShapeDtypeStruct((N, D_MODEL), F32),
        compiler_params=pltpu.CompilerParams(dimension_semantics=("parallel",), vmem_limit_bytes=VMEM_LIMIT),
        name="mlp",
    )(x2, g2, wu, wd)


def _block_diag(blocks):
    n, a, b = blocks.shape
    eye = jnp.eye(n, dtype=blocks.dtype)
    return (eye[:, None, :, None] * blocks[:, :, None, :]).reshape(n * a, n * b)


def _prepare(T, norm1_g, w_in, pool_w, pool_scale, gqa_qnorm, gqa_knorm, gqa_sink, hgrn_lb, hgrn_onorm,
             nat_qnorm, nat_knorm, nat_rpb, w_br_pool, w_br_gqa, w_br_hgrn, w_br_nat, w_o, norm2_g, w_up, w_down):
    D = D_MODEL
    half = HEAD_DIM // 2
    inv = ROPE_THETA ** (-jnp.arange(half, dtype=F32) / half)
    ang = jnp.arange(T, dtype=F32)[:, None] * inv[None, :]
    cos = jnp.tile(jnp.concatenate([jnp.cos(ang), jnp.cos(ang)], axis=1), (1, 2))
    sin = jnp.tile(jnp.concatenate([-jnp.sin(ang), jnp.sin(ang)], axis=1), (1, 2))
    sm = jax.nn.softmax(hgrn_lb.astype(F32), axis=0)
    lower = jnp.cumsum(sm, axis=0) - sm[:1]
    grp = _block_diag(jnp.full((640 // HEAD_DIM, HEAD_DIM, HEAD_DIM), 1.0 / HEAD_DIM, F32)).astype(BF16)
    scale = HEAD_DIM ** -0.5
    order = jnp.asarray(GQA_HEAD_ORDER)
    layers = []
    for l in range(w_in.shape[0]):
        w_l = w_in[l].astype(BF16)
        w_q = w_l[:, POOL_WIDTH:POOL_WIDTH + GQA_QW].reshape(D, GQA_HEADS, HEAD_DIM)[:, order, :].reshape(D, GQA_QW)
        layers.append(dict(
            g1=norm1_g[l][None, :],
            w_in=jnp.concatenate([w_l[:, :POOL_WIDTH], w_q, w_l[:, POOL_WIDTH + GQA_QW:]], axis=1),
            qk_gain=jnp.concatenate([jnp.tile(gqa_qnorm[l] * scale, GQA_HEADS),
                                     jnp.tile(gqa_knorm[l], GQA_KV_HEADS)])[None, :],
            n_gain=jnp.concatenate([jnp.tile(nat_qnorm[l] * scale, NAT_HEADS),
                                    jnp.tile(nat_knorm[l], NAT_HEADS)])[None, :],
            lower=lower[l].reshape(1, 2 * HGRN_W),
            sink=gqa_sink[l].astype(F32)[order],
            nat_bias=_nat_bias_table(nat_rpb[l], T // GRID_W),
            onorm=jnp.tile(hgrn_onorm[l], HGRN_HEADS)[None, :],
            pool_w=_block_diag(pool_w[l]).astype(BF16), pool_scale=pool_scale[l][None, :],
            w_bp=w_br_pool[l].astype(BF16),
            w_bg=w_br_gqa[l].astype(BF16).reshape(GQA_HEADS, HEAD_DIM, D)[order].reshape(GQA_QW, D),
            w_bh=w_br_hgrn[l].astype(BF16), w_bn=w_br_nat[l].astype(BF16), w_o=w_o[l].astype(BF16),
            g2=norm2_g[l][None, :], w_up=w_up[l].astype(BF16), w_down=w_down[l].astype(BF16)))
    return dict(cos=cos, sin=sin, grp=grp, layers=layers)


def _trunk(x, prep):
    Bt, T, D = x.shape
    N = Bt * T
    grp = prep["grp"]
    r3 = lambda a: a.reshape(Bt, T, a.shape[-1])
    for p in prep["layers"]:
        (up, gq, gk, gv, lf, hk, hv, hq, hg, nq, nk, nv, gates) = _inproj(
            x.reshape(N, D), p["g1"], p["w_in"], prep["cos"], prep["sin"], p["qk_gain"], p["n_gain"], p["lower"], grp, T)
        ob = _gqa(r3(gq), r3(gk), r3(gv), p["sink"])
        od = _nat(r3(nq), r3(nk), r3(nv), p["nat_bias"])
        o_f = _hgrn_sweep(r3(lf), r3(hk), r3(hv), r3(hq), rev=False)
        oc = _hgrn_sweep(r3(lf), r3(hk), r3(hv), r3(hq), rev=True,
                         extra=(o_f, r3(hg), p["onorm"], grp[:HGRN_W, :HGRN_W]))
        x = _merge(x, r3(up), ob, oc, od, r3(gates), p["pool_w"], p["pool_scale"],
                   p["w_bp"], p["w_bg"], p["w_bh"], p["w_bn"], p["w_o"])
        x = _mlp(x.reshape(N, D), p["g2"], p["w_up"], p["w_down"]).reshape(Bt, T, D)
    return x


def kernel(x_prompt, x_sample, norm1_g, w_in, pool_w, pool_scale, gqa_qnorm, gqa_knorm, gqa_sink, hgrn_lb,
           hgrn_onorm, nat_qnorm, nat_knorm, nat_rpb, w_br_pool, w_br_gqa, w_br_hgrn, w_br_nat, w_o, norm2_g,
           w_up, w_down):
    assert x_prompt.shape[1] == x_sample.shape[1]
    prep = _prepare(x_prompt.shape[1], norm1_g, w_in, pool_w, pool_scale, gqa_qnorm, gqa_knorm, gqa_sink, hgrn_lb,
                    hgrn_onorm, nat_qnorm, nat_knorm, nat_rpb, w_br_pool, w_br_gqa, w_br_hgrn, w_br_nat, w_o,
                    norm2_g, w_up, w_down)
    return (_trunk(x_prompt, prep), _trunk(x_sample, prep))
```

```python
import functools

import numpy as np

import jax
import jax.numpy as jnp
from jax import lax
from jax.experimental import pallas as pl
from jax.experimental.pallas import tpu as pltpu

F32 = jnp.float32
BF16 = jnp.bfloat16

D_MODEL = 1024
GRID_W = 64
EPS = 1e-6
NEG = -1e30
TINY = 1e-30
HEAD_DIM = 64
ROPE_THETA = 10000.0
N_BRANCH = 4
D_FF = 4 * D_MODEL

POOL_WIDTH = 256
POOL_WINDOWS = (2, 4, 8, 16)
POOL_GW = 64
POOL_HALO = 16

GQA_HEADS = 8
GQA_KV_HEADS = 2
GQA_GROUP = GQA_HEADS // GQA_KV_HEADS
GQA_WINDOW = 128
GQA_QW = GQA_HEADS * HEAD_DIM
GQA_KVW = GQA_KV_HEADS * HEAD_DIM

HGRN_HEADS = 4
HGRN_W = HGRN_HEADS * HEAD_DIM

NAT_HEADS = 4
NAT_KR = 8
NAT_KC = 16
NAT_W = NAT_HEADS * HEAD_DIM
NAT_QR = 4
NAT_KROWS = 12
GQA_HEAD_ORDER = tuple((j % 2) * GQA_GROUP + j // 2 for j in range(GQA_HEADS))

IN_WIDTH = 7168
GATE_OFF = 3072

TM_IN = 512
TQ_GQA = 128
TT_HGRN = 128
HGRN_CHUNK = 32
HGRN_SAFE = 60.0
TM_MERGE = 256
TM_MLP = 512
VMEM_LIMIT = 56 * 1024 * 1024


def _sigmoid(x):
    return 1.0 / (1.0 + jnp.exp(-x))


def _dot(a, b):
    return jnp.dot(a, b, preferred_element_type=F32)


def _dot_nt(a, b):
    return lax.dot_general(a, b, (((1,), (1,)), ((), ())), preferred_element_type=F32)


def _dot_tn(a, b):
    return lax.dot_general(a, b, (((0,), (0,)), ((), ())), preferred_element_type=F32)


def _const_spec(shape):
    n = len(shape)
    return pl.BlockSpec(shape, lambda *_: (0,) * n, pipeline_mode=pl.Buffered(1))


def _inproj_kernel(x_ref, g_ref, w_ref, cos_ref, sin_ref, qkg_ref, ng_ref, lb_ref, grp_ref,
                   up_ref, gq_ref, gk_ref, gv_ref, lf_ref, hk_ref, hv_ref, hq_ref, hg_ref,
                   nq_ref, nk_ref, nv_ref, gate_ref):
    x = x_ref[...]
    ms = jnp.mean(x * x, axis=-1, keepdims=True)
    h = (x * lax.rsqrt(ms + EPS) * g_ref[...]).astype(BF16)

    def headnorm(y, gain):
        w = y.shape[1]
        msq = _dot((y * y).astype(BF16), grp_ref[0:w, 0:w])
        return y * lax.rsqrt(msq + EPS) * gain

    z = _dot(h, w_ref[:, 0:1024])
    up_ref[...] = z[:, 0:POOL_WIDTH].astype(BF16)
    qkw = GQA_QW + GQA_KVW
    qk = headnorm(z[:, 256:256 + qkw], qkg_ref[...])
    reps = qkw // 128
    cos = jnp.concatenate([cos_ref[...]] * reps, axis=1)
    sin = jnp.concatenate([sin_ref[...]] * reps, axis=1)
    first = (lax.broadcasted_iota(jnp.int32, (1, qkw), 1) % HEAD_DIM) < (HEAD_DIM // 2)
    partner = jnp.where(first, pltpu.roll(qk, qkw - HEAD_DIM // 2, 1), pltpu.roll(qk, HEAD_DIM // 2, 1))
    qk = qk * cos + partner * sin
    gq_ref[...] = qk[:, 0:GQA_QW].astype(BF16)
    gk_ref[...] = qk[:, GQA_QW:qkw].astype(BF16)
    gv_ref[...] = z[:, 896:1024].astype(BF16)

    z = _dot(h, w_ref[:, 1024:2048])
    lb = lb_ref[...]
    f = lb + (1.0 - lb) * _sigmoid(z[:, 0:512])
    lf_ref[...] = jnp.log(jnp.maximum(f, TINY))
    hk_ref[...] = (1.0 - f).astype(BF16)
    hv_ref[...] = z[:, 512:768].astype(BF16)
    zq = z[:, 768:1024]
    hq_ref[...] = (zq * _sigmoid(zq)).astype(BF16)

    z = _dot(h, w_ref[:, 2048:3072])
    zg = z[:, 0:256]
    hg_ref[...] = (zg * _sigmoid(zg)).astype(BF16)
    nqk = headnorm(z[:, 256:768], ng_ref[...])
    nq_ref[...] = nqk[:, 0:256].astype(BF16)
    nk_ref[...] = nqk[:, 256:512].astype(BF16)
    nv_ref[...] = z[:, 768:1024].astype(BF16)

    for c in range(N_BRANCH):
        z = _dot(h, w_ref[:, GATE_OFF + c * D_MODEL:GATE_OFF + (c + 1) * D_MODEL])
        gate_ref[:, c * D_MODEL:(c + 1) * D_MODEL] = _sigmoid(z).astype(BF16)


def _inproj(x2, g1, w_in, cos, sin, qk_gain, n_gain, lb, grp, T):
    N = x2.shape[0]
    tm = TM_IN
    tpb = T // tm
    tok = lambda w: pl.BlockSpec((tm, w), lambda i: (i, 0))
    widths = [(POOL_WIDTH, BF16), (GQA_QW, BF16), (GQA_KVW, BF16), (GQA_KVW, BF16), (512, F32), (512, BF16),
              (256, BF16), (256, BF16), (256, BF16), (256, BF16), (256, BF16), (256, BF16), (4 * D_MODEL, BF16)]
    return pl.pallas_call(
        _inproj_kernel,
        grid=(N // tm,),
        in_specs=[
            tok(D_MODEL),
            _const_spec((1, D_MODEL)),
            _const_spec((D_MODEL, IN_WIDTH)),
            pl.BlockSpec((tm, 128), lambda i: (i % tpb, 0)),
            pl.BlockSpec((tm, 128), lambda i: (i % tpb, 0)),
            _const_spec((1, GQA_QW + GQA_KVW)),
            _const_spec((1, 2 * NAT_W)),
            _const_spec((1, 512)),
            _const_spec((640, 640)),
        ],
        out_specs=[tok(w) for w, _ in widths],
        out_shape=[jax.ShapeDtypeStruct((N, w), dt) for w, dt in widths],
        compiler_params=pltpu.CompilerParams(dimension_semantics=("parallel",), vmem_limit_bytes=VMEM_LIMIT),
        name="inproj",
    )(x2, g1, w_in, cos, sin, qk_gain, n_gain, lb, grp)


def _gqa_kernel(sink_ref, q_ref, k_ref, v_ref, o_ref, *, tq, T):
    i = pl.program_id(1)
    t0 = i * tq
    win = tq + 2 * GQA_WINDOW
    w0 = pl.multiple_of(jnp.clip(t0 - GQA_WINDOW, 0, T - win), 128)
    q = q_ref[0]
    k = k_ref[0, pl.ds(w0, win), :]
    v = v_ref[0, pl.ds(w0, win), :]
    qpos = t0 + lax.broadcasted_iota(jnp.int32, (tq, 1), 0)
    kpos = w0 + lax.broadcasted_iota(jnp.int32, (1, win), 1)
    valid = jnp.abs(kpos - qpos) <= GQA_WINDOW
    lane = lax.broadcasted_iota(jnp.int32, (1, 2 * HEAD_DIM), 1)
    half = [jnp.where(lane < HEAD_DIM, 1.0, 0.0).astype(BF16), jnp.where(lane < HEAD_DIM, 0.0, 1.0).astype(BF16)]
    qs = jnp.concatenate([q[:, (j // 2) * 128:(j // 2 + 1) * 128] * half[j % 2] for j in range(GQA_HEADS)], axis=0)
    s_all = _dot_nt(qs, k)
    ps, invs = [], []
    for j in range(GQA_HEADS):
        s = jnp.where(valid, s_all[j * tq:(j + 1) * tq], NEG)
        sink = sink_ref[j]
        m = jnp.maximum(jnp.max(s, axis=-1, keepdims=True), sink)
        p = jnp.exp(s - m)
        invs.append(1.0 / (jnp.sum(p, axis=-1, keepdims=True) + jnp.exp(sink - m)))
        ps.append(p.astype(BF16))
    o_all = _dot(jnp.concatenate(ps, axis=0), v)
    outs = []
    for pair in range(GQA_HEADS // 2):
        j0, j1 = 2 * pair, 2 * pair + 1
        outs.append(jnp.where(lane < HEAD_DIM, o_all[j0 * tq:(j0 + 1) * tq] * invs[j0],
                              o_all[j1 * tq:(j1 + 1) * tq] * invs[j1]))
    o_ref[0] = jnp.concatenate(outs, axis=1).astype(BF16)


def _gqa(q, k, v, sink):
    Bt, T, _ = q.shape
    tq = TQ_GQA
    return pl.pallas_call(
        functools.partial(_gqa_kernel, tq=tq, T=T),
        grid=(Bt, T // tq),
        in_specs=[
            pl.BlockSpec(memory_space=pltpu.SMEM),
            pl.BlockSpec((1, tq, GQA_QW), lambda b, i: (b, i, 0)),
            pl.BlockSpec((1, T, GQA_KVW), lambda b, i: (b, 0, 0)),
            pl.BlockSpec((1, T, GQA_KVW), lambda b, i: (b, 0, 0)),
        ],
        out_specs=pl.BlockSpec((1, tq, GQA_QW), lambda b, i: (b, i, 0)),
        out_shape=jax.ShapeDtypeStruct((Bt, T, GQA_QW), BF16),
        compiler_params=pltpu.CompilerParams(dimension_semantics=("parallel", "arbitrary"),
                                             vmem_limit_bytes=VMEM_LIMIT),
        name="gqa",
    )(sink, q, k, v)


def _nat_kernel(q_ref, k_ref, v_ref, bias_ref, o_ref, *, nblk):
    blk = pl.program_id(1)
    first = jnp.clip(blk - 1, 0, nblk - NAT_KROWS // NAT_QR)
    variant = jnp.where(blk == 0, 0, jnp.where(blk == nblk - 1, 2, 1))
    nq, nk = NAT_QR * GRID_W, NAT_KROWS * GRID_W
    off = pl.multiple_of(first * nq, nq)
    q = q_ref[0]
    k = k_ref[0, pl.ds(off, nk), :]
    v = v_ref[0, pl.ds(off, nk), :]
    lane_head = lax.broadcasted_iota(jnp.int32, (1, NAT_W), 1) // HEAD_DIM
    qs = jnp.concatenate([q * jnp.where(lane_head == h, 1.0, 0.0).astype(BF16) for h in range(NAT_HEADS)], axis=0)
    s_all = _dot_nt(qs, k)
    ps, invs = [], []
    for h in range(NAT_HEADS):
        s = s_all[h * nq:(h + 1) * nq] + bias_ref[variant, h]
        m = jnp.max(s, axis=-1, keepdims=True)
        p = jnp.exp(s - m)
        invs.append(1.0 / jnp.sum(p, axis=-1, keepdims=True))
        ps.append(p.astype(BF16))
    o_all = _dot(jnp.concatenate(ps, axis=0), v)
    out = o_all[0:nq] * invs[0]
    for h in range(1, NAT_HEADS):
        out = jnp.where(lane_head == h, o_all[h * nq:(h + 1) * nq] * invs[h], out)
    o_ref[0] = out.astype(BF16)


def _nat(q, k, v, bias):
    Bt, T, _ = q.shape
    rows = T // GRID_W
    nblk = rows // NAT_QR
    assert rows % NAT_QR == 0 and rows >= NAT_KROWS
    nq = NAT_QR * GRID_W
    return pl.pallas_call(
        functools.partial(_nat_kernel, nblk=nblk),
        grid=(Bt, nblk),
        in_specs=[
            pl.BlockSpec((1, nq, NAT_W), lambda b, r: (b, r, 0)),
            pl.BlockSpec((1, T, NAT_W), lambda b, r: (b, 0, 0)),
            pl.BlockSpec((1, T, NAT_W), lambda b, r: (b, 0, 0)),
            _const_spec((3, NAT_HEADS, nq, NAT_KROWS * GRID_W)),
        ],
        out_specs=pl.BlockSpec((1, nq, NAT_W), lambda b, r: (b, r, 0)),
        out_shape=jax.ShapeDtypeStruct((Bt, T, NAT_W), BF16),
        compiler_params=pltpu.CompilerParams(dimension_semantics=("parallel", "arbitrary"),
                                             vmem_limit_bytes=VMEM_LIMIT),
        name="nat",
    )(q, k, v, bias)


def _nat_bias_table(rpb, rows):
    nblk = rows // NAT_QR
    c = np.arange(GRID_W)[:, None]
    kc = np.arange(GRID_W)[None, :]
    qc0 = np.clip(c - NAT_KC // 2, 0, GRID_W - NAT_KC)
    col_ok = (kc >= qc0) & (kc < qc0 + NAT_KC)
    dcol = np.clip(kc - c + (NAT_KC - 1), 0, 2 * NAT_KC - 2)
    oh_col = (dcol[None] == np.arange(2 * NAT_KC - 1)[:, None, None]) & col_ok[None]
    oh_row = np.zeros((3, NAT_QR, NAT_KROWS, 2 * NAT_KR - 1), np.float32)
    for var, blk in enumerate((0, 1, nblk - 1)):
        first = int(np.clip(blk - 1, 0, nblk - NAT_KROWS // NAT_QR))
        for ri in range(NAT_QR):
            r = blk * NAT_QR + ri
            r0 = int(np.clip(r - NAT_KR // 2, 0, rows - NAT_KR))
            for j in range(NAT_KROWS):
                kr = first * NAT_QR + j
                if r0 <= kr < r0 + NAT_KR:
                    oh_row[var, ri, j, kr - r + NAT_KR - 1] = 1.0
    hp = lax.Precision.HIGHEST
    by_col = jnp.einsum('hde,eck->hdck', rpb.astype(F32), jnp.asarray(oh_col, F32), precision=hp)
    bias = jnp.einsum('vrjd,hdck->vhrcjk', jnp.asarray(oh_row), by_col, precision=hp)
    ok = (oh_row.sum(-1) > 0)[:, None, :, None, :, None] & col_ok[None, None, None, :, None, :]
    bias = jnp.where(jnp.asarray(ok), bias, NEG)
    return bias.reshape(3, NAT_HEADS, NAT_QR * GRID_W, NAT_KROWS * GRID_W)


def _hgrn_consts(Tt, C, rev):
    t = np.arange(Tt)
    row, col = t[:, None], t[None, :]
    same = ((row // C) == (col // C)) & ((col >= row) if rev else (col <= row))
    pair = (row // (2 * C)) == (col // (2 * C))
    masks = np.stack([np.tile(same, (HGRN_HEADS, 1)), np.tile(pair, (HGRN_HEADS, 1))]).astype(np.float32)
    hl = np.arange(HGRN_W) // HEAD_DIM
    bd = (hl[:, None] == hl[None, :]).astype(np.float32)
    return jnp.asarray(same.astype(np.float32), BF16), jnp.asarray(masks), jnp.asarray(bd)


def _hgrn_kernel(lf_f, k_f, v_f, q_f, lf_b, k_b, v_b, q_b, tri_f, mask_f, tri_b, mask_b, bd_ref,
                 of_ref, ob_ref, st_f, st_b, *, Tt, C):
    @pl.when(pl.program_id(1) == 0)
    def _():
        st_f[...] = jnp.zeros_like(st_f)
        st_b[...] = jnp.zeros_like(st_b)

    min_f, fast_f, safe_f, finish_f = _hgrn_tile(lf_f, k_f, v_f, q_f, tri_f, mask_f, bd_ref, st_f, rev=False, Tt=Tt, C=C)
    min_b, fast_b, safe_b, finish_b = _hgrn_tile(lf_b, k_b, v_b, q_b, tri_b, mask_b, bd_ref, st_b, rev=True, Tt=Tt, C=C)
    a_f, a_b = lax.cond(jnp.minimum(min_f, min_b) >= -HGRN_SAFE,
                        lambda: (fast_f(), fast_b()), lambda: (safe_f(), safe_b()))
    of_ref[0] = finish_f(a_f).astype(BF16)
    ob_ref[0] = finish_b(a_b).astype(BF16)


def _hgrn_tile(lf_ref, k_ref, v_ref, q_ref, tri_ref, mask_ref, bd_ref, st_ref, *, rev, Tt, C):
    nc = Tt // C
    lf = lf_ref[0]
    lane_head = lax.broadcasted_iota(jnp.int32, (1, HGRN_W), 1) // HEAD_DIM
    head_sel = [jnp.where(lane_head == h, 1.0, 0.0).astype(BF16) for h in range(HGRN_HEADS)]

    def stack_heads(x16):
        return jnp.concatenate([x16 * m for m in head_sel], axis=0)

    tri = tri_ref[...]
    hi = lf.astype(BF16)
    lo = (lf - hi.astype(F32)).astype(BF16)
    bc = _dot(tri, hi) + _dot(tri, lo)

    cs = [slice(c * C, (c + 1) * C) for c in range(nc)]
    pos = [(nc - 1 - c) if rev else c for c in range(nc)]
    at = [pos.index(p) for p in range(nc)]
    last = 0 if rev else C - 1
    tot = [bc[c * C + last:c * C + last + 1] for c in range(nc)]
    zero = jnp.zeros((C, HGRN_W), F32)
    cat = lambda xs: jnp.concatenate(xs, axis=0).astype(BF16)

    def intra_fast():
        qa = (q_ref[0].astype(F32) * jnp.exp(bc)).astype(BF16)
        ka = (k_ref[0].astype(F32) * jnp.exp(-bc)).astype(BF16)
        return jnp.where(mask_ref[0] > 0.5, _dot_nt(stack_heads(qa), ka), 0.0)

    def intra_safe():
        q = q_ref[0].astype(F32)
        k = k_ref[0].astype(F32)
        t_idx = lax.broadcasted_iota(jnp.int32, (Tt, 1), 0)
        tau = (Tt - 1 - t_idx) if rev else t_idx
        row = lax.broadcasted_iota(jnp.int32, (HGRN_HEADS * Tt, Tt), 0) % Tt
        col = lax.broadcasted_iota(jnp.int32, (HGRN_HEADS * Tt, Tt), 1)
        acc = jnp.where(row == col, _dot_nt(stack_heads(q_ref[0]), k_ref[0]), 0.0)
        before = bc - lf
        end = bc
        size = 2
        while size <= C:
            half = size // 2
            upper = (tau % size) >= half
            w = jnp.exp(jnp.where(upper, bc - before, end - bc))
            qt = jnp.where(upper, q * w, 0.0).astype(BF16)
            kt = jnp.where(upper, 0.0, k * w).astype(BF16)
            acc = acc + jnp.where((row // size) == (col // size), _dot_nt(stack_heads(qt), kt), 0.0)
            if size < C:
                back, fwd = (Tt - half, half) if rev else (half, Tt - half)
                before = jnp.where(upper, pltpu.roll(before, back, 0), before)
                end = jnp.where(upper, end, pltpu.roll(end, fwd, 0))
            size *= 2
        return acc

    def finish(a_intra):
        q = q_ref[0].astype(F32)
        k = k_ref[0].astype(F32)
        v = v_ref[0]
        qa = [q[cs[c]] * jnp.exp(bc[cs[c]]) for c in range(nc)]
        kb = [k[cs[c]] * jnp.exp(tot[c] - bc[cs[c]]) for c in range(nc)]
        q_pair = cat([qa[c] if pos[c] % 2 == 1 else zero for c in range(nc)])
        k_pair = cat([kb[c] if pos[c] % 2 == 0 else zero for c in range(nc)])
        g_q = jnp.exp(tot[at[2]])
        g_k = jnp.exp(tot[at[1]])
        q_half = cat([zero if pos[c] < 2 else (qa[c] * g_q if pos[c] == 3 else qa[c]) for c in range(nc)])
        k_half = cat([zero if pos[c] >= 2 else (kb[c] * g_k if pos[c] == 0 else kb[c]) for c in range(nc)])
        a_all = (a_intra + jnp.where(mask_ref[1] > 0.5, _dot_nt(stack_heads(q_pair), k_pair), 0.0)
                 + _dot_nt(stack_heads(q_half), k_half))

        pre, suf = [None] * nc, [None] * nc
        run = jnp.zeros_like(tot[0])
        for p in range(nc):
            pre[at[p]] = run
            run = run + tot[at[p]]
        b_tile = run
        run = jnp.zeros_like(tot[0])
        for p in reversed(range(nc)):
            suf[at[p]] = run
            run = run + tot[at[p]]
        q_in = cat([qa[c] * jnp.exp(pre[c]) for c in range(nc)])
        k_out = cat([kb[c] * jnp.exp(suf[c]) for c in range(nc)])

        st = st_ref[...]
        o = _dot_nt(q_in, st.astype(BF16))
        st_ref[...] = st * jnp.exp(b_tile) + jnp.where(bd_ref[...] > 0.5, _dot_tn(v, k_out), 0.0)
        o_heads = _dot(a_all.astype(BF16), v)
        for h in range(HGRN_HEADS):
            o = o + jnp.where(lane_head == h, o_heads[h * Tt:(h + 1) * Tt], 0.0)
        return o

    return jnp.min(functools.reduce(jnp.minimum, tot)), intra_fast, intra_safe, finish


def _hgrn(lf, hk, hv, hq):
    Bt, T, _ = hv.shape
    Tt, C = TT_HGRN, HGRN_CHUNK
    assert Tt == 4 * C
    nt = T // Tt
    fwd = lambda col: pl.BlockSpec((1, Tt, HGRN_W), lambda b, n: (b, n, col))
    bwd = lambda col: pl.BlockSpec((1, Tt, HGRN_W), lambda b, n: (b, nt - 1 - n, col))
    tri_f, mask_f, bd = _hgrn_consts(Tt, C, False)
    tri_b, mask_b, _ = _hgrn_consts(Tt, C, True)
    tri_spec, mask_spec = _const_spec((Tt, Tt)), _const_spec((2, HGRN_HEADS * Tt, Tt))
    out = jax.ShapeDtypeStruct((Bt, T, HGRN_W), BF16)
    return pl.pallas_call(
        functools.partial(_hgrn_kernel, Tt=Tt, C=C),
        grid=(Bt, nt),
        in_specs=[fwd(0), fwd(0), fwd(0), fwd(0), bwd(1), bwd(1), bwd(0), bwd(0),
                  tri_spec, mask_spec, tri_spec, mask_spec, _const_spec((HGRN_W, HGRN_W))],
        out_specs=[fwd(0), bwd(0)],
        out_shape=[out, out],
        scratch_shapes=[pltpu.VMEM((HGRN_W, HGRN_W), F32), pltpu.VMEM((HGRN_W, HGRN_W), F32)],
        compiler_params=pltpu.CompilerParams(dimension_semantics=("parallel", "arbitrary"),
                                             vmem_limit_bytes=VMEM_LIMIT),
        name="hgrn",
    )(lf, hk, hv, hq, lf, hk, hv, hq, tri_f, mask_f, tri_b, mask_b, bd)


def _merge_kernel(x_ref, u_ref, ob_ref, hf_ref, hb_ref, hg_ref, hn_ref, grp_ref, od_ref, g_ref, pw_ref, ps_ref,
                  wbp_ref, wbg_ref, wbh_ref, wbn_ref, wo_ref, o_ref, *, tm, T):
    i = pl.program_id(1)
    t0 = pl.multiple_of(i * tm, tm)
    win = tm + 2 * POOL_HALO
    w0 = pl.multiple_of(jnp.clip(t0 - POOL_HALO, 0, T - win), POOL_HALO)
    uw = u_ref[0, pl.ds(w0, win), :]
    uc = u_ref[0, pl.ds(t0, tm), :].astype(F32)
    tg = t0 + lax.broadcasted_iota(jnp.int32, (tm, 1), 0)
    sg = w0 + lax.broadcasted_iota(jnp.int32, (1, win), 1)
    lane = lax.broadcasted_iota(jnp.int32, (1, POOL_WIDTH), 1)
    mixed = jnp.zeros((tm, POOL_WIDTH), F32)
    for g, w in enumerate(POOL_WINDOWS):
        lo = jnp.maximum(tg - w // 2, 0)
        hi = jnp.minimum(tg + w // 2, T)
        inside = jnp.where(sg >= lo, jnp.where(sg < hi, 1.0, 0.0), 0.0).astype(BF16)
        mean = _dot(inside, uw) / (hi - lo).astype(F32)
        mixed = jnp.where(lane // POOL_GW == g, mean, mixed)
    mixed = mixed - uc
    oa = _dot(mixed.astype(BF16), pw_ref[...]) * ps_ref[...]
    hsum = hf_ref[0].astype(F32) + hb_ref[0].astype(F32)
    msq = _dot((hsum * hsum).astype(BF16), grp_ref[...])
    oc = hsum * lax.rsqrt(msq + EPS) * hn_ref[...] * hg_ref[0].astype(F32)
    gate = lambda c: g_ref[0, :, c * D_MODEL:(c + 1) * D_MODEL].astype(F32)
    merged = (gate(0) * _dot(oa.astype(BF16), wbp_ref[...])
              + gate(1) * _dot(ob_ref[0], wbg_ref[...])
              + gate(2) * _dot(oc.astype(BF16), wbh_ref[...])
              + gate(3) * _dot(od_ref[0], wbn_ref[...]))
    o_ref[0] = x_ref[0] + _dot(merged.astype(BF16), wo_ref[...])


def _merge(x, u, ob, hf, hb, hg, hn, grp, od, gates, pw, ps, wbp, wbg, wbh, wbn, wo):
    Bt, T, _ = x.shape
    tm = TM_MERGE
    tok = lambda w: pl.BlockSpec((1, tm, w), lambda b, i: (b, i, 0))
    return pl.pallas_call(
        functools.partial(_merge_kernel, tm=tm, T=T),
        grid=(Bt, T // tm),
        in_specs=[
            tok(D_MODEL),
            pl.BlockSpec((1, T, POOL_WIDTH), lambda b, i: (b, 0, 0)),
            tok(GQA_QW), tok(HGRN_W), tok(HGRN_W), tok(HGRN_W), _const_spec((1, HGRN_W)),
            _const_spec((HGRN_W, HGRN_W)), tok(NAT_W), tok(N_BRANCH * D_MODEL),
            _const_spec((POOL_WIDTH, POOL_WIDTH)), _const_spec((1, POOL_WIDTH)),
            _const_spec((POOL_WIDTH, D_MODEL)), _const_spec((GQA_QW, D_MODEL)),
            _const_spec((HGRN_W, D_MODEL)), _const_spec((NAT_W, D_MODEL)),
            _const_spec((D_MODEL, D_MODEL)),
        ],
        out_specs=tok(D_MODEL),
        out_shape=jax.ShapeDtypeStruct((Bt, T, D_MODEL), F32),
        compiler_params=pltpu.CompilerParams(dimension_semantics=("parallel", "arbitrary"),
                                             vmem_limit_bytes=VMEM_LIMIT),
        name="merge",
    )(x, u, ob, hf, hb, hg, hn, grp, od, gates, pw, ps, wbp, wbg, wbh, wbn, wo)


def _mlp_kernel(x_ref, g_ref, wu_ref, wd_ref, o_ref):
    x = x_ref[...]
    ms = jnp.mean(x * x, axis=-1, keepdims=True)
    h = (x * lax.rsqrt(ms + EPS) * g_ref[...]).astype(BF16)
    acc = x
    for c in range(D_FF // D_MODEL):
        sl = slice(c * D_MODEL, (c + 1) * D_MODEL)
        hid = jnp.maximum(_dot(h, wu_ref[:, sl]), 0.0)
        acc = acc + _dot((hid * hid).astype(BF16), wd_ref[sl, :])
    o_ref[...] = acc


def _mlp(x2, g2, wu, wd):
    N = x2.shape[0]
    tm = TM_MLP
    return pl.pallas_call(
        _mlp_kernel,
        grid=(N // tm,),
        in_specs=[pl.BlockSpec((tm, D_MODEL), lambda i: (i, 0)), _const_spec((1, D_MODEL)),
                  _const_spec((D_MODEL, D_FF)), _const_spec((D_FF, D_MODEL))],
        out_specs=pl.BlockSpec((tm, D_MODEL), lambda i: (i, 0)),
        out_shape=jax.ShapeDtypeStruct((N, D_MODEL), F32),
        compiler_params=pltpu.CompilerParams(dimension_semantics=("parallel",), vmem_limit_bytes=VMEM_LIMIT),
        name="mlp",
    )(x2, g2, wu, wd)


def _block_diag(blocks):
    n, a, b = blocks.shape
    eye = jnp.eye(n, dtype=blocks.dtype)
    return (eye[:, None, :, None] * blocks[:, :, None, :]).reshape(n * a, n * b)


def _prepare(T, norm1_g, w_in, pool_w, pool_scale, gqa_qnorm, gqa_knorm, gqa_sink, hgrn_lb, hgrn_onorm,
             nat_qnorm, nat_knorm, nat_rpb, w_br_pool, w_br_gqa, w_br_hgrn, w_br_nat, w_o, norm2_g, w_up, w_down):
    D = D_MODEL
    half = HEAD_DIM // 2
    inv = ROPE_THETA ** (-jnp.arange(half, dtype=F32) / half)
    ang = jnp.arange(T, dtype=F32)[:, None] * inv[None, :]
    cos = jnp.tile(jnp.concatenate([jnp.cos(ang), jnp.cos(ang)], axis=1), (1, 2))
    sin = jnp.tile(jnp.concatenate([-jnp.sin(ang), jnp.sin(ang)], axis=1), (1, 2))
    sm = jax.nn.softmax(hgrn_lb.astype(F32), axis=0)
    lower = jnp.cumsum(sm, axis=0) - sm[:1]
    grp = _block_diag(jnp.full((640 // HEAD_DIM, HEAD_DIM, HEAD_DIM), 1.0 / HEAD_DIM, F32)).astype(BF16)
    scale = HEAD_DIM ** -0.5
    order = jnp.asarray(GQA_HEAD_ORDER)
    layers = []
    for l in range(w_in.shape[0]):
        w_l = w_in[l].astype(BF16)
        w_q = w_l[:, POOL_WIDTH:POOL_WIDTH + GQA_QW].reshape(D, GQA_HEADS, HEAD_DIM)[:, order, :].reshape(D, GQA_QW)
        layers.append(dict(
            g1=norm1_g[l][None, :],
            w_in=jnp.concatenate([w_l[:, :POOL_WIDTH], w_q, w_l[:, POOL_WIDTH + GQA_QW:]], axis=1),
            qk_gain=jnp.concatenate([jnp.tile(gqa_qnorm[l] * scale, GQA_HEADS),
                                     jnp.tile(gqa_knorm[l], GQA_KV_HEADS)])[None, :],
            n_gain=jnp.concatenate([jnp.tile(nat_qnorm[l] * scale, NAT_HEADS),
                                    jnp.tile(nat_knorm[l], NAT_HEADS)])[None, :],
            lower=lower[l].reshape(1, 2 * HGRN_W),
            sink=gqa_sink[l].astype(F32)[order],
            nat_bias=_nat_bias_table(nat_rpb[l], T // GRID_W),
            onorm=jnp.tile(hgrn_onorm[l], HGRN_HEADS)[None, :],
            pool_w=_block_diag(pool_w[l]).astype(BF16), pool_scale=pool_scale[l][None, :],
            w_bp=w_br_pool[l].astype(BF16),
            w_bg=w_br_gqa[l].astype(BF16).reshape(GQA_HEADS, HEAD_DIM, D)[order].reshape(GQA_QW, D),
            w_bh=w_br_hgrn[l].astype(BF16), w_bn=w_br_nat[l].astype(BF16), w_o=w_o[l].astype(BF16),
            g2=norm2_g[l][None, :], w_up=w_up[l].astype(BF16), w_down=w_down[l].astype(BF16)))
    return dict(cos=cos, sin=sin, grp=grp, layers=layers)


def _trunk(x, prep):
    Bt, T, D = x.shape
    N = Bt * T
    grp = prep["grp"]
    r3 = lambda a: a.reshape(Bt, T, a.shape[-1])
    for p in prep["layers"]:
        (up, gq, gk, gv, lf, hk, hv, hq, hg, nq, nk, nv, gates) = _inproj(
            x.reshape(N, D), p["g1"], p["w_in"], prep["cos"], prep["sin"], p["qk_gain"], p["n_gain"], p["lower"], grp, T)
        ob = _gqa(r3(gq), r3(gk), r3(gv), p["sink"])
        od = _nat(r3(nq), r3(nk), r3(nv), p["nat_bias"])
        h_f, h_b = _hgrn(r3(lf), r3(hk), r3(hv), r3(hq))
        x = _merge(x, r3(up), ob, h_f, h_b, r3(hg), p["onorm"], grp[:HGRN_W, :HGRN_W], od, r3(gates),
                   p["pool_w"], p["pool_scale"],
                   p["w_bp"], p["w_bg"], p["w_bh"], p["w_bn"], p["w_o"])
        x = _mlp(x.reshape(N, D), p["g2"], p["w_up"], p["w_down"]).reshape(Bt, T, D)
    return x


def kernel(x_prompt, x_sample, norm1_g, w_in, pool_w, pool_scale, gqa_qnorm, gqa_knorm, gqa_sink, hgrn_lb,
           hgrn_onorm, nat_qnorm, nat_knorm, nat_rpb, w_br_pool, w_br_gqa, w_br_hgrn, w_br_nat, w_o, norm2_g,
           w_up, w_down):
    assert x_prompt.shape[1] == x_sample.shape[1]
    prep = _prepare(x_prompt.shape[1], norm1_g, w_in, pool_w, pool_scale, gqa_qnorm, gqa_knorm, gqa_sink, hgrn_lb,
                    hgrn_onorm, nat_qnorm, nat_knorm, nat_rpb, w_br_pool, w_br_gqa, w_br_hgrn, w_br_nat, w_o,
                    norm2_g, w_up, w_down)
    return (_trunk(x_prompt, prep), _trunk(x_sample, prep))
```

```python
import functools

import numpy as np

import jax
import jax.numpy as jnp
from jax import lax
from jax.experimental import pallas as pl
from jax.experimental.pallas import tpu as pltpu

F32 = jnp.float32
BF16 = jnp.bfloat16

D_MODEL = 1024
GRID_W = 64
EPS = 1e-6
NEG = -1e30
TINY = 1e-30
HEAD_DIM = 64
ROPE_THETA = 10000.0
N_BRANCH = 4
D_FF = 4 * D_MODEL

POOL_WIDTH = 256
POOL_WINDOWS = (2, 4, 8, 16)
POOL_GW = 64
POOL_HALO = 16

GQA_HEADS = 8
GQA_KV_HEADS = 2
GQA_GROUP = GQA_HEADS // GQA_KV_HEADS
GQA_WINDOW = 128
GQA_QW = GQA_HEADS * HEAD_DIM
GQA_KVW = GQA_KV_HEADS * HEAD_DIM

HGRN_HEADS = 4
HGRN_W = HGRN_HEADS * HEAD_DIM

NAT_HEADS = 4
NAT_KR = 8
NAT_KC = 16
NAT_W = NAT_HEADS * HEAD_DIM
NAT_QR = 4
NAT_KROWS = 12
GQA_HEAD_ORDER = tuple((j % 2) * GQA_GROUP + j // 2 for j in range(GQA_HEADS))

IN_WIDTH = 7168
GATE_OFF = 3072

TM_IN = 512
TQ_GQA = 128
GQA_STACK = 8
TT_HGRN = 128
HGRN_CHUNK = 32
HGRN_SAFE = 60.0
TM_MERGE = 512
TM_MLP = 512
VMEM_LIMIT = 56 * 1024 * 1024


def _sigmoid(x):
    return 1.0 / (1.0 + jnp.exp(-x))


def _dot(a, b):
    return jnp.dot(a, b, preferred_element_type=F32)


def _dot_nt(a, b):
    return lax.dot_general(a, b, (((1,), (1,)), ((), ())), preferred_element_type=F32)


def _dot_tn(a, b):
    return lax.dot_general(a, b, (((0,), (0,)), ((), ())), preferred_element_type=F32)


def _const_spec(shape):
    n = len(shape)
    return pl.BlockSpec(shape, lambda *_: (0,) * n, pipeline_mode=pl.Buffered(1))


def _inproj_kernel(x_ref, g_ref, w_ref, cos_ref, sin_ref, qkg_ref, ng_ref, lb_ref, grp_ref,
                   up_ref, gq_ref, gk_ref, gv_ref, lf_ref, hk_ref, hv_ref, hq_ref, hg_ref,
                   nq_ref, nk_ref, nv_ref, gate_ref):
    x = x_ref[...]
    ms = jnp.mean(x * x, axis=-1, keepdims=True)
    h = (x * lax.rsqrt(ms + EPS) * g_ref[...]).astype(BF16)

    def headnorm(y, gain):
        w = y.shape[1]
        msq = _dot((y * y).astype(BF16), grp_ref[0:w, 0:w])
        return y * lax.rsqrt(msq + EPS) * gain

    z = _dot(h, w_ref[:, 0:1024])
    up_ref[...] = z[:, 0:POOL_WIDTH].astype(BF16)
    qkw = GQA_QW + GQA_KVW
    qk = headnorm(z[:, 256:256 + qkw], qkg_ref[...])
    reps = qkw // 128
    cos = jnp.concatenate([cos_ref[...]] * reps, axis=1)
    sin = jnp.concatenate([sin_ref[...]] * reps, axis=1)
    first = (lax.broadcasted_iota(jnp.int32, (1, qkw), 1) % HEAD_DIM) < (HEAD_DIM // 2)
    partner = jnp.where(first, pltpu.roll(qk, qkw - HEAD_DIM // 2, 1), pltpu.roll(qk, HEAD_DIM // 2, 1))
    qk = qk * cos + partner * sin
    gq_ref[...] = qk[:, 0:GQA_QW].astype(BF16)
    gk_ref[...] = qk[:, GQA_QW:qkw].astype(BF16)
    gv_ref[...] = z[:, 896:1024].astype(BF16)

    z = _dot(h, w_ref[:, 1024:2048])
    lb = lb_ref[...]
    f = lb + (1.0 - lb) * _sigmoid(z[:, 0:512])
    lf_ref[...] = jnp.log(jnp.maximum(f, TINY))
    hk_ref[...] = (1.0 - f).astype(BF16)
    hv_ref[...] = z[:, 512:768].astype(BF16)
    zq = z[:, 768:1024]
    hq_ref[...] = (zq * _sigmoid(zq)).astype(BF16)

    z = _dot(h, w_ref[:, 2048:3072])
    zg = z[:, 0:256]
    hg_ref[...] = (zg * _sigmoid(zg)).astype(BF16)
    nqk = headnorm(z[:, 256:768], ng_ref[...])
    nq_ref[...] = nqk[:, 0:256].astype(BF16)
    nk_ref[...] = nqk[:, 256:512].astype(BF16)
    nv_ref[...] = z[:, 768:1024].astype(BF16)

    for c in range(N_BRANCH):
        z = _dot(h, w_ref[:, GATE_OFF + c * D_MODEL:GATE_OFF + (c + 1) * D_MODEL])
        gate_ref[:, c * D_MODEL:(c + 1) * D_MODEL] = _sigmoid(z).astype(BF16)


def _inproj(x2, g1, w_in, cos, sin, qk_gain, n_gain, lb, grp, T):
    N = x2.shape[0]
    tm = TM_IN
    tpb = T // tm
    tok = lambda w: pl.BlockSpec((tm, w), lambda i: (i, 0))
    widths = [(POOL_WIDTH, BF16), (GQA_QW, BF16), (GQA_KVW, BF16), (GQA_KVW, BF16), (512, F32), (512, BF16),
              (256, BF16), (256, BF16), (256, BF16), (256, BF16), (256, BF16), (256, BF16), (4 * D_MODEL, BF16)]
    return pl.pallas_call(
        _inproj_kernel,
        grid=(N // tm,),
        in_specs=[
            tok(D_MODEL),
            _const_spec((1, D_MODEL)),
            _const_spec((D_MODEL, IN_WIDTH)),
            pl.BlockSpec((tm, 128), lambda i: (i % tpb, 0)),
            pl.BlockSpec((tm, 128), lambda i: (i % tpb, 0)),
            _const_spec((1, GQA_QW + GQA_KVW)),
            _const_spec((1, 2 * NAT_W)),
            _const_spec((1, 512)),
            _const_spec((640, 640)),
        ],
        out_specs=[tok(w) for w, _ in widths],
        out_shape=[jax.ShapeDtypeStruct((N, w), dt) for w, dt in widths],
        compiler_params=pltpu.CompilerParams(dimension_semantics=("parallel",), vmem_limit_bytes=VMEM_LIMIT),
        name="inproj",
    )(x2, g1, w_in, cos, sin, qk_gain, n_gain, lb, grp)


def _gqa_kernel(sink_ref, q_ref, k_ref, v_ref, o_ref, *, tq, T):
    i = pl.program_id(1)
    t0 = i * tq
    win = tq + 2 * GQA_WINDOW
    w0 = pl.multiple_of(jnp.clip(t0 - GQA_WINDOW, 0, T - win), 128)
    q = q_ref[0]
    k = k_ref[0, pl.ds(w0, win), :]
    v = v_ref[0, pl.ds(w0, win), :]
    qpos = t0 + lax.broadcasted_iota(jnp.int32, (tq, 1), 0)
    kpos = w0 + lax.broadcasted_iota(jnp.int32, (1, win), 1)
    valid = jnp.abs(kpos - qpos) <= GQA_WINDOW
    lane = lax.broadcasted_iota(jnp.int32, (1, 2 * HEAD_DIM), 1)
    half = [jnp.where(lane < HEAD_DIM, 1.0, 0.0).astype(BF16), jnp.where(lane < HEAD_DIM, 0.0, 1.0).astype(BF16)]
    o_heads = []
    for j0 in range(0, GQA_HEADS, GQA_STACK):
        js = range(j0, j0 + GQA_STACK)
        qs = jnp.concatenate([q[:, (j // 2) * 128:(j // 2 + 1) * 128] * half[j % 2] for j in js], axis=0)
        s_all = _dot_nt(qs, k)
        ps, invs = [], []
        for i, j in enumerate(js):
            s = jnp.where(valid, s_all[i * tq:(i + 1) * tq], NEG)
            sink = sink_ref[j]
            m = jnp.maximum(jnp.max(s, axis=-1, keepdims=True), sink)
            p = jnp.exp(s - m)
            invs.append(1.0 / (jnp.sum(p, axis=-1, keepdims=True) + jnp.exp(sink - m)))
            ps.append(p.astype(BF16))
        o_all = _dot(jnp.concatenate(ps, axis=0), v)
        o_heads += [o_all[i * tq:(i + 1) * tq] * invs[i] for i in range(GQA_STACK)]
    outs = [jnp.where(lane < HEAD_DIM, o_heads[2 * pair], o_heads[2 * pair + 1]) for pair in range(GQA_HEADS // 2)]
    o_ref[0] = jnp.concatenate(outs, axis=1).astype(BF16)


def _gqa(q, k, v, sink):
    Bt, T, _ = q.shape
    tq = TQ_GQA
    return pl.pallas_call(
        functools.partial(_gqa_kernel, tq=tq, T=T),
        grid=(Bt, T // tq),
        in_specs=[
            pl.BlockSpec(memory_space=pltpu.SMEM),
            pl.BlockSpec((1, tq, GQA_QW), lambda b, i: (b, i, 0)),
            pl.BlockSpec((1, T, GQA_KVW), lambda b, i: (b, 0, 0)),
            pl.BlockSpec((1, T, GQA_KVW), lambda b, i: (b, 0, 0)),
        ],
        out_specs=pl.BlockSpec((1, tq, GQA_QW), lambda b, i: (b, i, 0)),
        out_shape=jax.ShapeDtypeStruct((Bt, T, GQA_QW), BF16),
        compiler_params=pltpu.CompilerParams(dimension_semantics=("parallel", "arbitrary"),
                                             vmem_limit_bytes=VMEM_LIMIT),
        name="gqa",
    )(sink, q, k, v)


def _nat_kernel(q_ref, k_ref, v_ref, bias_ref, o_ref, *, nblk):
    blk = pl.program_id(1)
    first = jnp.clip(blk - 1, 0, nblk - NAT_KROWS // NAT_QR)
    variant = jnp.where(blk == 0, 0, jnp.where(blk == nblk - 1, 2, 1))
    nq, nk = NAT_QR * GRID_W, NAT_KROWS * GRID_W
    off = pl.multiple_of(first * nq, nq)
    q = q_ref[0]
    k = k_ref[0, pl.ds(off, nk), :]
    v = v_ref[0, pl.ds(off, nk), :]
    lane_head = lax.broadcasted_iota(jnp.int32, (1, NAT_W), 1) // HEAD_DIM
    out = None
    for h in range(NAT_HEADS):
        s = _dot_nt(q * jnp.where(lane_head == h, 1.0, 0.0).astype(BF16), k) + bias_ref[variant, h]
        m = jnp.max(s, axis=-1, keepdims=True)
        p = jnp.exp(s - m)
        inv = 1.0 / jnp.sum(p, axis=-1, keepdims=True)
        o_h = _dot(p.astype(BF16), v) * inv
        out = o_h if out is None else jnp.where(lane_head == h, o_h, out)
    o_ref[0] = out.astype(BF16)


def _nat(q, k, v, bias):
    Bt, T, _ = q.shape
    rows = T // GRID_W
    nblk = rows // NAT_QR
    assert rows % NAT_QR == 0 and rows >= NAT_KROWS
    nq = NAT_QR * GRID_W
    return pl.pallas_call(
        functools.partial(_nat_kernel, nblk=nblk),
        grid=(Bt, nblk),
        in_specs=[
            pl.BlockSpec((1, nq, NAT_W), lambda b, r: (b, r, 0)),
            pl.BlockSpec((1, T, NAT_W), lambda b, r: (b, 0, 0)),
            pl.BlockSpec((1, T, NAT_W), lambda b, r: (b, 0, 0)),
            _const_spec((3, NAT_HEADS, nq, NAT_KROWS * GRID_W)),
        ],
        out_specs=pl.BlockSpec((1, nq, NAT_W), lambda b, r: (b, r, 0)),
        out_shape=jax.ShapeDtypeStruct((Bt, T, NAT_W), BF16),
        compiler_params=pltpu.CompilerParams(dimension_semantics=("parallel", "arbitrary"),
                                             vmem_limit_bytes=VMEM_LIMIT),
        name="nat",
    )(q, k, v, bias)


def _nat_bias_table(rpb, rows):
    nblk = rows // NAT_QR
    c = np.arange(GRID_W)[:, None]
    kc = np.arange(GRID_W)[None, :]
    qc0 = np.clip(c - NAT_KC // 2, 0, GRID_W - NAT_KC)
    col_ok = (kc >= qc0) & (kc < qc0 + NAT_KC)
    dcol = np.clip(kc - c + (NAT_KC - 1), 0, 2 * NAT_KC - 2)
    oh_col = (dcol[None] == np.arange(2 * NAT_KC - 1)[:, None, None]) & col_ok[None]
    oh_row = np.zeros((3, NAT_QR, NAT_KROWS, 2 * NAT_KR - 1), np.float32)
    for var, blk in enumerate((0, 1, nblk - 1)):
        first = int(np.clip(blk - 1, 0, nblk - NAT_KROWS // NAT_QR))
        for ri in range(NAT_QR):
            r = blk * NAT_QR + ri
            r0 = int(np.clip(r - NAT_KR // 2, 0, rows - NAT_KR))
            for j in range(NAT_KROWS):
                kr = first * NAT_QR + j
                if r0 <= kr < r0 + NAT_KR:
                    oh_row[var, ri, j, kr - r + NAT_KR - 1] = 1.0
    hp = lax.Precision.HIGHEST
    by_col = jnp.einsum('hde,eck->hdck', rpb.astype(F32), jnp.asarray(oh_col, F32), precision=hp)
    bias = jnp.einsum('vrjd,hdck->vhrcjk', jnp.asarray(oh_row), by_col, precision=hp)
    ok = (oh_row.sum(-1) > 0)[:, None, :, None, :, None] & col_ok[None, None, None, :, None, :]
    bias = jnp.where(jnp.asarray(ok), bias, NEG)
    return bias.reshape(3, NAT_HEADS, NAT_QR * GRID_W, NAT_KROWS * GRID_W)


def _hgrn_consts(Tt, C, rev):
    t = np.arange(Tt)
    row, col = t[:, None], t[None, :]
    same = ((row // C) == (col // C)) & ((col >= row) if rev else (col <= row))
    pair = (row // (2 * C)) == (col // (2 * C))
    masks = np.stack([np.tile(same, (HGRN_HEADS, 1)), np.tile(pair, (HGRN_HEADS, 1))]).astype(np.float32)
    hl = np.arange(HGRN_W) // HEAD_DIM
    bd = (hl[:, None] == hl[None, :]).astype(np.float32)
    return jnp.asarray(same.astype(np.float32), BF16), jnp.asarray(masks), jnp.asarray(bd)


def _hgrn_kernel(lf_f, k_f, v_f, q_f, lf_b, k_b, v_b, q_b, tri_f, mask_f, tri_b, mask_b, bd_ref,
                 of_ref, ob_ref, st_f, st_b, *, Tt, C):
    @pl.when(pl.program_id(1) == 0)
    def _():
        st_f[...] = jnp.zeros_like(st_f)
        st_b[...] = jnp.zeros_like(st_b)

    min_f, fast_f, safe_f, finish_f = _hgrn_tile(lf_f, k_f, v_f, q_f, tri_f, mask_f, bd_ref, st_f, rev=False, Tt=Tt, C=C)
    min_b, fast_b, safe_b, finish_b = _hgrn_tile(lf_b, k_b, v_b, q_b, tri_b, mask_b, bd_ref, st_b, rev=True, Tt=Tt, C=C)
    a_f, a_b = lax.cond(jnp.minimum(min_f, min_b) >= -HGRN_SAFE,
                        lambda: (fast_f(), fast_b()), lambda: (safe_f(), safe_b()))
    of_ref[0] = finish_f(a_f).astype(BF16)
    ob_ref[0] = finish_b(a_b).astype(BF16)


def _hgrn_tile(lf_ref, k_ref, v_ref, q_ref, tri_ref, mask_ref, bd_ref, st_ref, *, rev, Tt, C):
    nc = Tt // C
    lf = lf_ref[0]
    lane_head = lax.broadcasted_iota(jnp.int32, (1, HGRN_W), 1) // HEAD_DIM
    head_sel = [jnp.where(lane_head == h, 1.0, 0.0).astype(BF16) for h in range(HGRN_HEADS)]

    def stack_heads(x16):
        return jnp.concatenate([x16 * m for m in head_sel], axis=0)

    tri = tri_ref[...]
    hi = lf.astype(BF16)
    lo = (lf - hi.astype(F32)).astype(BF16)
    bc = _dot(tri, hi) + _dot(tri, lo)

    cs = [slice(c * C, (c + 1) * C) for c in range(nc)]
    pos = [(nc - 1 - c) if rev else c for c in range(nc)]
    at = [pos.index(p) for p in range(nc)]
    last = 0 if rev else C - 1
    tot = [bc[c * C + last:c * C + last + 1] for c in range(nc)]
    zero = jnp.zeros((C, HGRN_W), F32)
    cat = lambda xs: jnp.concatenate(xs, axis=0).astype(BF16)

    def intra_fast():
        qa = (q_ref[0].astype(F32) * jnp.exp(bc)).astype(BF16)
        ka = (k_ref[0].astype(F32) * jnp.exp(-bc)).astype(BF16)
        return jnp.where(mask_ref[0] > 0.5, _dot_nt(stack_heads(qa), ka), 0.0)

    def intra_safe():
        q = q_ref[0].astype(F32)
        k = k_ref[0].astype(F32)
        t_idx = lax.broadcasted_iota(jnp.int32, (Tt, 1), 0)
        tau = (Tt - 1 - t_idx) if rev else t_idx
        row = lax.broadcasted_iota(jnp.int32, (HGRN_HEADS * Tt, Tt), 0) % Tt
        col = lax.broadcasted_iota(jnp.int32, (HGRN_HEADS * Tt, Tt), 1)
        acc = jnp.where(row == col, _dot_nt(stack_heads(q_ref[0]), k_ref[0]), 0.0)
        before = bc - lf
        end = bc
        size = 2
        while size <= C:
            half = size // 2
            upper = (tau % size) >= half
            w = jnp.exp(jnp.where(upper, bc - before, end - bc))
            qt = jnp.where(upper, q * w, 0.0).astype(BF16)
            kt = jnp.where(upper, 0.0, k * w).astype(BF16)
            acc = acc + jnp.where((row // size) == (col // size), _dot_nt(stack_heads(qt), kt), 0.0)
            if size < C:
                back, fwd = (Tt - half, half) if rev else (half, Tt - half)
                before = jnp.where(upper, pltpu.roll(before, back, 0), before)
                end = jnp.where(upper, end, pltpu.roll(end, fwd, 0))
            size *= 2
        return acc

    def finish(a_intra):
        q = q_ref[0].astype(F32)
        k = k_ref[0].astype(F32)
        v = v_ref[0]
        qa = [q[cs[c]] * jnp.exp(bc[cs[c]]) for c in range(nc)]
        kb = [k[cs[c]] * jnp.exp(tot[c] - bc[cs[c]]) for c in range(nc)]
        q_pair = cat([qa[c] if pos[c] % 2 == 1 else zero for c in range(nc)])
        k_pair = cat([kb[c] if pos[c] % 2 == 0 else zero for c in range(nc)])
        g_q = jnp.exp(tot[at[2]])
        g_k = jnp.exp(tot[at[1]])
        q_half = cat([zero if pos[c] < 2 else (qa[c] * g_q if pos[c] == 3 else qa[c]) for c in range(nc)])
        k_half = cat([zero if pos[c] >= 2 else (kb[c] * g_k if pos[c] == 0 else kb[c]) for c in range(nc)])
        a_all = (a_intra + jnp.where(mask_ref[1] > 0.5, _dot_nt(stack_heads(q_pair), k_pair), 0.0)
                 + _dot_nt(stack_heads(q_half), k_half))

        pre, suf = [None] * nc, [None] * nc
        run = jnp.zeros_like(tot[0])
        for p in range(nc):
            pre[at[p]] = run
            run = run + tot[at[p]]
        b_tile = run
        run = jnp.zeros_like(tot[0])
        for p in reversed(range(nc)):
            suf[at[p]] = run
            run = run + tot[at[p]]
        q_in = cat([qa[c] * jnp.exp(pre[c]) for c in range(nc)])
        k_out = cat([kb[c] * jnp.exp(suf[c]) for c in range(nc)])

        st = st_ref[...]
        o = _dot_nt(q_in, st.astype(BF16))
        st_ref[...] = st * jnp.exp(b_tile) + jnp.where(bd_ref[...] > 0.5, _dot_tn(v, k_out), 0.0)
        o_heads = _dot(a_all.astype(BF16), v)
        for h in range(HGRN_HEADS):
            o = o + jnp.where(lane_head == h, o_heads[h * Tt:(h + 1) * Tt], 0.0)
        return o

    return jnp.min(functools.reduce(jnp.minimum, tot)), intra_fast, intra_safe, finish


def _hgrn(lf, hk, hv, hq):
    Bt, T, _ = hv.shape
    Tt, C = TT_HGRN, HGRN_CHUNK
    assert Tt == 4 * C
    nt = T // Tt
    fwd = lambda col: pl.BlockSpec((1, Tt, HGRN_W), lambda b, n: (b, n, col))
    bwd = lambda col: pl.BlockSpec((1, Tt, HGRN_W), lambda b, n: (b, nt - 1 - n, col))
    tri_f, mask_f, bd = _hgrn_consts(Tt, C, False)
    tri_b, mask_b, _ = _hgrn_consts(Tt, C, True)
    tri_spec, mask_spec = _const_spec((Tt, Tt)), _const_spec((2, HGRN_HEADS * Tt, Tt))
    out = jax.ShapeDtypeStruct((Bt, T, HGRN_W), BF16)
    return pl.pallas_call(
        functools.partial(_hgrn_kernel, Tt=Tt, C=C),
        grid=(Bt, nt),
        in_specs=[fwd(0), fwd(0), fwd(0), fwd(0), bwd(1), bwd(1), bwd(0), bwd(0),
                  tri_spec, mask_spec, tri_spec, mask_spec, _const_spec((HGRN_W, HGRN_W))],
        out_specs=[fwd(0), bwd(0)],
        out_shape=[out, out],
        scratch_shapes=[pltpu.VMEM((HGRN_W, HGRN_W), F32), pltpu.VMEM((HGRN_W, HGRN_W), F32)],
        compiler_params=pltpu.CompilerParams(dimension_semantics=("parallel", "arbitrary"),
                                             vmem_limit_bytes=VMEM_LIMIT),
        name="hgrn",
    )(lf, hk, hv, hq, lf, hk, hv, hq, tri_f, mask_f, tri_b, mask_b, bd)


def _merge_kernel(x_ref, u_ref, ob_ref, hf_ref, hb_ref, hg_ref, hn_ref, grp_ref, od_ref, g_ref, pw_ref, ps_ref,
                  wbp_ref, wbg_ref, wbh_ref, wbn_ref, wo_ref, o_ref, *, tm, T):
    i = pl.program_id(1)
    t0 = pl.multiple_of(i * tm, tm)
    win = tm + 2 * POOL_HALO
    w0 = pl.multiple_of(jnp.clip(t0 - POOL_HALO, 0, T - win), POOL_HALO)
    uw = u_ref[0, pl.ds(w0, win), :]
    uc = u_ref[0, pl.ds(t0, tm), :].astype(F32)
    tg = t0 + lax.broadcasted_iota(jnp.int32, (tm, 1), 0)
    sg = w0 + lax.broadcasted_iota(jnp.int32, (1, win), 1)
    lane = lax.broadcasted_iota(jnp.int32, (1, POOL_WIDTH), 1)
    mixed = jnp.zeros((tm, POOL_WIDTH), F32)
    for g, w in enumerate(POOL_WINDOWS):
        lo = jnp.maximum(tg - w // 2, 0)
        hi = jnp.minimum(tg + w // 2, T)
        inside = jnp.where(sg >= lo, jnp.where(sg < hi, 1.0, 0.0), 0.0).astype(BF16)
        mean = _dot(inside, uw) / (hi - lo).astype(F32)
        mixed = jnp.where(lane // POOL_GW == g, mean, mixed)
    mixed = mixed - uc
    oa = _dot(mixed.astype(BF16), pw_ref[...]) * ps_ref[...]
    hsum = hf_ref[0].astype(F32) + hb_ref[0].astype(F32)
    msq = _dot((hsum * hsum).astype(BF16), grp_ref[...])
    oc = hsum * lax.rsqrt(msq + EPS) * hn_ref[...] * hg_ref[0].astype(F32)
    gate = lambda c: g_ref[0, :, c * D_MODEL:(c + 1) * D_MODEL].astype(F32)
    merged = (gate(0) * _dot(oa.astype(BF16), wbp_ref[...])
              + gate(1) * _dot(ob_ref[0], wbg_ref[...])
              + gate(2) * _dot(oc.astype(BF16), wbh_ref[...])
              + gate(3) * _dot(od_ref[0], wbn_ref[...]))
    o_ref[0] = x_ref[0] + _dot(merged.astype(BF16), wo_ref[...])


def _merge(x, u, ob, hf, hb, hg, hn, grp, od, gates, pw, ps, wbp, wbg, wbh, wbn, wo):
    Bt, T, _ = x.shape
    tm = TM_MERGE
    tok = lambda w: pl.BlockSpec((1, tm, w), lambda b, i: (b, i, 0))
    return pl.pallas_call(
        functools.partial(_merge_kernel, tm=tm, T=T),
        grid=(Bt, T // tm),
        in_specs=[
            tok(D_MODEL),
            pl.BlockSpec((1, T, POOL_WIDTH), lambda b, i: (b, 0, 0)),
            tok(GQA_QW), tok(HGRN_W), tok(HGRN_W), tok(HGRN_W), _const_spec((1, HGRN_W)),
            _const_spec((HGRN_W, HGRN_W)), tok(NAT_W), tok(N_BRANCH * D_MODEL),
            _const_spec((POOL_WIDTH, POOL_WIDTH)), _const_spec((1, POOL_WIDTH)),
            _const_spec((POOL_WIDTH, D_MODEL)), _const_spec((GQA_QW, D_MODEL)),
            _const_spec((HGRN_W, D_MODEL)), _const_spec((NAT_W, D_MODEL)),
            _const_spec((D_MODEL, D_MODEL)),
        ],
        out_specs=tok(D_MODEL),
        out_shape=jax.ShapeDtypeStruct((Bt, T, D_MODEL), F32),
        compiler_params=pltpu.CompilerParams(dimension_semantics=("parallel", "arbitrary"),
                                             vmem_limit_bytes=VMEM_LIMIT),
        name="merge",
    )(x, u, ob, hf, hb, hg, hn, grp, od, gates, pw, ps, wbp, wbg, wbh, wbn, wo)


def _mlp_kernel(x_ref, g_ref, wu_ref, wd_ref, o_ref):
    x = x_ref[...]
    ms = jnp.mean(x * x, axis=-1, keepdims=True)
    h = (x * lax.rsqrt(ms + EPS) * g_ref[...]).astype(BF16)
    acc = x
    for c in range(D_FF // D_MODEL):
        sl = slice(c * D_MODEL, (c + 1) * D_MODEL)
        hid = jnp.maximum(_dot(h, wu_ref[:, sl]), 0.0)
        acc = acc + _dot((hid * hid).astype(BF16), wd_ref[sl, :])
    o_ref[...] = acc


def _mlp(x2, g2, wu, wd):
    N = x2.shape[0]
    tm = TM_MLP
    return pl.pallas_call(
        _mlp_kernel,
        grid=(N // tm,),
        in_specs=[pl.BlockSpec((tm, D_MODEL), lambda i: (i, 0)), _const_spec((1, D_MODEL)),
                  _const_spec((D_MODEL, D_FF)), _const_spec((D_FF, D_MODEL))],
        out_specs=pl.BlockSpec((tm, D_MODEL), lambda i: (i, 0)),
        out_shape=jax.ShapeDtypeStruct((N, D_MODEL), F32),
        compiler_params=pltpu.CompilerParams(dimension_semantics=("parallel",), vmem_limit_bytes=VMEM_LIMIT),
        name="mlp",
    )(x2, g2, wu, wd)


def _block_diag(blocks):
    n, a, b = blocks.shape
    eye = jnp.eye(n, dtype=blocks.dtype)
    return (eye[:, None, :, None] * blocks[:, :, None, :]).reshape(n * a, n * b)


def _prepare(T, norm1_g, w_in, pool_w, pool_scale, gqa_qnorm, gqa_knorm, gqa_sink, hgrn_lb, hgrn_onorm,
             nat_qnorm, nat_knorm, nat_rpb, w_br_pool, w_br_gqa, w_br_hgrn, w_br_nat, w_o, norm2_g, w_up, w_down):
    D = D_MODEL
    half = HEAD_DIM // 2
    inv = ROPE_THETA ** (-jnp.arange(half, dtype=F32) / half)
    ang = jnp.arange(T, dtype=F32)[:, None] * inv[None, :]
    cos = jnp.tile(jnp.concatenate([jnp.cos(ang), jnp.cos(ang)], axis=1), (1, 2))
    sin = jnp.tile(jnp.concatenate([-jnp.sin(ang), jnp.sin(ang)], axis=1), (1, 2))
    sm = jax.nn.softmax(hgrn_lb.astype(F32), axis=0)
    lower = jnp.cumsum(sm, axis=0) - sm[:1]
    grp = _block_diag(jnp.full((640 // HEAD_DIM, HEAD_DIM, HEAD_DIM), 1.0 / HEAD_DIM, F32)).astype(BF16)
    scale = HEAD_DIM ** -0.5
    order = jnp.asarray(GQA_HEAD_ORDER)
    layers = []
    for l in range(w_in.shape[0]):
        w_l = w_in[l].astype(BF16)
        w_q = w_l[:, POOL_WIDTH:POOL_WIDTH + GQA_QW].reshape(D, GQA_HEADS, HEAD_DIM)[:, order, :].reshape(D, GQA_QW)
        layers.append(dict(
            g1=norm1_g[l][None, :],
            w_in=jnp.concatenate([w_l[:, :POOL_WIDTH], w_q, w_l[:, POOL_WIDTH + GQA_QW:]], axis=1),
            qk_gain=jnp.concatenate([jnp.tile(gqa_qnorm[l] * scale, GQA_HEADS),
                                     jnp.tile(gqa_knorm[l], GQA_KV_HEADS)])[None, :],
            n_gain=jnp.concatenate([jnp.tile(nat_qnorm[l] * scale, NAT_HEADS),
                                    jnp.tile(nat_knorm[l], NAT_HEADS)])[None, :],
            lower=lower[l].reshape(1, 2 * HGRN_W),
            sink=gqa_sink[l].astype(F32)[order],
            nat_bias=_nat_bias_table(nat_rpb[l], T // GRID_W),
            onorm=jnp.tile(hgrn_onorm[l], HGRN_HEADS)[None, :],
            pool_w=_block_diag(pool_w[l]).astype(BF16), pool_scale=pool_scale[l][None, :],
            w_bp=w_br_pool[l].astype(BF16),
            w_bg=w_br_gqa[l].astype(BF16).reshape(GQA_HEADS, HEAD_DIM, D)[order].reshape(GQA_QW, D),
            w_bh=w_br_hgrn[l].astype(BF16), w_bn=w_br_nat[l].astype(BF16), w_o=w_o[l].astype(BF16),
            g2=norm2_g[l][None, :], w_up=w_up[l].astype(BF16), w_down=w_down[l].astype(BF16)))
    return dict(cos=cos, sin=sin, grp=grp, layers=layers)


def _trunk(x, prep):
    Bt, T, D = x.shape
    N = Bt * T
    grp = prep["grp"]
    r3 = lambda a: a.reshape(Bt, T, a.shape[-1])
    for p in prep["layers"]:
        (up, gq, gk, gv, lf, hk, hv, hq, hg, nq, nk, nv, gates) = _inproj(
            x.reshape(N, D), p["g1"], p["w_in"], prep["cos"], prep["sin"], p["qk_gain"], p["n_gain"], p["lower"], grp, T)
        ob = _gqa(r3(gq), r3(gk), r3(gv), p["sink"])
        od = _nat(r3(nq), r3(nk), r3(nv), p["nat_bias"])
        h_f, h_b = _hgrn(r3(lf), r3(hk), r3(hv), r3(hq))
        x = _merge(x, r3(up), ob, h_f, h_b, r3(hg), p["onorm"], grp[:HGRN_W, :HGRN_W], od, r3(gates),
                   p["pool_w"], p["pool_scale"],
                   p["w_bp"], p["w_bg"], p["w_bh"], p["w_bn"], p["w_o"])
        x = _mlp(x.reshape(N, D), p["g2"], p["w_up"], p["w_down"]).reshape(Bt, T, D)
    return x


def kernel(x_prompt, x_sample, norm1_g, w_in, pool_w, pool_scale, gqa_qnorm, gqa_knorm, gqa_sink, hgrn_lb,
           hgrn_onorm, nat_qnorm, nat_knorm, nat_rpb, w_br_pool, w_br_gqa, w_br_hgrn, w_br_nat, w_o, norm2_g,
           w_up, w_down):
    assert x_prompt.shape[1] == x_sample.shape[1]
    prep = _prepare(x_prompt.shape[1], norm1_g, w_in, pool_w, pool_scale, gqa_qnorm, gqa_knorm, gqa_sink, hgrn_lb,
                    hgrn_onorm, nat_qnorm, nat_knorm, nat_rpb, w_br_pool, w_br_gqa, w_br_hgrn, w_br_nat, w_o,
                    norm2_g, w_up, w_down)
    return (_trunk(x_prompt, prep), _trunk(x_sample, prep))
```

```python
import functools

import numpy as np

import jax
import jax.numpy as jnp
from jax import lax
from jax.experimental import pallas as pl
from jax.experimental.pallas import tpu as pltpu

F32 = jnp.float32
BF16 = jnp.bfloat16

D_MODEL = 1024
GRID_W = 64
EPS = 1e-6
NEG = -1e30
TINY = 1e-30
HEAD_DIM = 64
ROPE_THETA = 10000.0
N_BRANCH = 4
D_FF = 4 * D_MODEL

POOL_WIDTH = 256
POOL_WINDOWS = (2, 4, 8, 16)
POOL_GW = 64
POOL_HALO = 16

GQA_HEADS = 8
GQA_KV_HEADS = 2
GQA_GROUP = GQA_HEADS // GQA_KV_HEADS
GQA_WINDOW = 128
GQA_QW = GQA_HEADS * HEAD_DIM
GQA_KVW = GQA_KV_HEADS * HEAD_DIM

HGRN_HEADS = 4
HGRN_W = HGRN_HEADS * HEAD_DIM

NAT_HEADS = 4
NAT_KR = 8
NAT_KC = 16
NAT_W = NAT_HEADS * HEAD_DIM
NAT_QR = 4
NAT_KROWS = 12
GQA_HEAD_ORDER = tuple((j % 2) * GQA_GROUP + j // 2 for j in range(GQA_HEADS))

IN_WIDTH = 7168
GATE_OFF = 3072

TM_IN = 512
TQ_GQA = 128
GQA_STACK = 8
TT_HGRN = 128
HGRN_BB = 2
HGRN_CHUNK = 32
HGRN_SAFE = 60.0
TM_MERGE = 512
TM_MLP = 512
VMEM_LIMIT = 56 * 1024 * 1024


def _sigmoid(x):
    return 1.0 / (1.0 + jnp.exp(-x))


def _dot(a, b):
    return jnp.dot(a, b, preferred_element_type=F32)


def _dot_nt(a, b):
    return lax.dot_general(a, b, (((1,), (1,)), ((), ())), preferred_element_type=F32)


def _dot_tn(a, b):
    return lax.dot_general(a, b, (((0,), (0,)), ((), ())), preferred_element_type=F32)


def _const_spec(shape):
    n = len(shape)
    return pl.BlockSpec(shape, lambda *_: (0,) * n, pipeline_mode=pl.Buffered(1))


def _inproj_kernel(x_ref, g_ref, w_ref, cos_ref, sin_ref, qkg_ref, ng_ref, lb_ref, grp_ref,
                   up_ref, gq_ref, gk_ref, gv_ref, lf_ref, hk_ref, hv_ref, hq_ref, hg_ref,
                   nq_ref, nk_ref, nv_ref, gate_ref):
    x = x_ref[...]
    ms = jnp.mean(x * x, axis=-1, keepdims=True)
    h = (x * lax.rsqrt(ms + EPS) * g_ref[...]).astype(BF16)

    def headnorm(y, gain):
        w = y.shape[1]
        msq = _dot((y * y).astype(BF16), grp_ref[0:w, 0:w])
        return y * lax.rsqrt(msq + EPS) * gain

    z = _dot(h, w_ref[:, 0:1024])
    up_ref[...] = z[:, 0:POOL_WIDTH].astype(BF16)
    qkw = GQA_QW + GQA_KVW
    qk = headnorm(z[:, 256:256 + qkw], qkg_ref[...])
    reps = qkw // 128
    cos = jnp.concatenate([cos_ref[...]] * reps, axis=1)
    sin = jnp.concatenate([sin_ref[...]] * reps, axis=1)
    first = (lax.broadcasted_iota(jnp.int32, (1, qkw), 1) % HEAD_DIM) < (HEAD_DIM // 2)
    partner = jnp.where(first, pltpu.roll(qk, qkw - HEAD_DIM // 2, 1), pltpu.roll(qk, HEAD_DIM // 2, 1))
    qk = qk * cos + partner * sin
    gq_ref[...] = qk[:, 0:GQA_QW].astype(BF16)
    gk_ref[...] = qk[:, GQA_QW:qkw].astype(BF16)
    gv_ref[...] = z[:, 896:1024].astype(BF16)

    z = _dot(h, w_ref[:, 1024:2048])
    lb = lb_ref[...]
    f = lb + (1.0 - lb) * _sigmoid(z[:, 0:512])
    lf_ref[...] = jnp.log(jnp.maximum(f, TINY))
    hk_ref[...] = (1.0 - f).astype(BF16)
    hv_ref[...] = z[:, 512:768].astype(BF16)
    zq = z[:, 768:1024]
    hq_ref[...] = (zq * _sigmoid(zq)).astype(BF16)

    z = _dot(h, w_ref[:, 2048:3072])
    zg = z[:, 0:256]
    hg_ref[...] = (zg * _sigmoid(zg)).astype(BF16)
    nqk = headnorm(z[:, 256:768], ng_ref[...])
    nq_ref[...] = nqk[:, 0:256].astype(BF16)
    nk_ref[...] = nqk[:, 256:512].astype(BF16)
    nv_ref[...] = z[:, 768:1024].astype(BF16)

    for c in range(N_BRANCH):
        z = _dot(h, w_ref[:, GATE_OFF + c * D_MODEL:GATE_OFF + (c + 1) * D_MODEL])
        gate_ref[:, c * D_MODEL:(c + 1) * D_MODEL] = _sigmoid(z).astype(BF16)


def _inproj(x2, g1, w_in, cos, sin, qk_gain, n_gain, lb, grp, T):
    N = x2.shape[0]
    tm = TM_IN
    tpb = T // tm
    tok = lambda w: pl.BlockSpec((tm, w), lambda i: (i, 0))
    widths = [(POOL_WIDTH, BF16), (GQA_QW, BF16), (GQA_KVW, BF16), (GQA_KVW, BF16), (512, F32), (512, BF16),
              (256, BF16), (256, BF16), (256, BF16), (256, BF16), (256, BF16), (256, BF16), (4 * D_MODEL, BF16)]
    return pl.pallas_call(
        _inproj_kernel,
        grid=(N // tm,),
        in_specs=[
            tok(D_MODEL),
            _const_spec((1, D_MODEL)),
            _const_spec((D_MODEL, IN_WIDTH)),
            pl.BlockSpec((tm, 128), lambda i: (i % tpb, 0)),
            pl.BlockSpec((tm, 128), lambda i: (i % tpb, 0)),
            _const_spec((1, GQA_QW + GQA_KVW)),
            _const_spec((1, 2 * NAT_W)),
            _const_spec((1, 512)),
            _const_spec((640, 640)),
        ],
        out_specs=[tok(w) for w, _ in widths],
        out_shape=[jax.ShapeDtypeStruct((N, w), dt) for w, dt in widths],
        compiler_params=pltpu.CompilerParams(dimension_semantics=("parallel",), vmem_limit_bytes=VMEM_LIMIT),
        name="inproj",
    )(x2, g1, w_in, cos, sin, qk_gain, n_gain, lb, grp)


def _gqa_kernel(sink_ref, q_ref, k_ref, v_ref, o_ref, *, tq, T):
    i = pl.program_id(1)
    t0 = i * tq
    win = tq + 2 * GQA_WINDOW
    w0 = pl.multiple_of(jnp.clip(t0 - GQA_WINDOW, 0, T - win), 128)
    q = q_ref[0]
    k = k_ref[0, pl.ds(w0, win), :]
    v = v_ref[0, pl.ds(w0, win), :]
    qpos = t0 + lax.broadcasted_iota(jnp.int32, (tq, 1), 0)
    kpos = w0 + lax.broadcasted_iota(jnp.int32, (1, win), 1)
    valid = jnp.abs(kpos - qpos) <= GQA_WINDOW
    lane = lax.broadcasted_iota(jnp.int32, (1, 2 * HEAD_DIM), 1)
    half = [jnp.where(lane < HEAD_DIM, 1.0, 0.0).astype(BF16), jnp.where(lane < HEAD_DIM, 0.0, 1.0).astype(BF16)]
    o_heads = []
    for j0 in range(0, GQA_HEADS, GQA_STACK):
        js = range(j0, j0 + GQA_STACK)
        qs = jnp.concatenate([q[:, (j // 2) * 128:(j // 2 + 1) * 128] * half[j % 2] for j in js], axis=0)
        s_all = _dot_nt(qs, k)
        ps, invs = [], []
        for i, j in enumerate(js):
            s = jnp.where(valid, s_all[i * tq:(i + 1) * tq], NEG)
            sink = sink_ref[j]
            m = jnp.maximum(jnp.max(s, axis=-1, keepdims=True), sink)
            p = jnp.exp(s - m)
            invs.append(1.0 / (jnp.sum(p, axis=-1, keepdims=True) + jnp.exp(sink - m)))
            ps.append(p.astype(BF16))
        o_all = _dot(jnp.concatenate(ps, axis=0), v)
        o_heads += [o_all[i * tq:(i + 1) * tq] * invs[i] for i in range(GQA_STACK)]
    outs = [jnp.where(lane < HEAD_DIM, o_heads[2 * pair], o_heads[2 * pair + 1]) for pair in range(GQA_HEADS // 2)]
    o_ref[0] = jnp.concatenate(outs, axis=1).astype(BF16)


def _gqa(q, k, v, sink):
    Bt, T, _ = q.shape
    tq = TQ_GQA
    return pl.pallas_call(
        functools.partial(_gqa_kernel, tq=tq, T=T),
        grid=(Bt, T // tq),
        in_specs=[
            pl.BlockSpec(memory_space=pltpu.SMEM),
            pl.BlockSpec((1, tq, GQA_QW), lambda b, i: (b, i, 0)),
            pl.BlockSpec((1, T, GQA_KVW), lambda b, i: (b, 0, 0)),
            pl.BlockSpec((1, T, GQA_KVW), lambda b, i: (b, 0, 0)),
        ],
        out_specs=pl.BlockSpec((1, tq, GQA_QW), lambda b, i: (b, i, 0)),
        out_shape=jax.ShapeDtypeStruct((Bt, T, GQA_QW), BF16),
        compiler_params=pltpu.CompilerParams(dimension_semantics=("parallel", "arbitrary"),
                                             vmem_limit_bytes=VMEM_LIMIT),
        name="gqa",
    )(sink, q, k, v)


def _nat_kernel(q_ref, k_ref, v_ref, bias_ref, o_ref, *, nblk):
    blk = pl.program_id(1)
    first = jnp.clip(blk - 1, 0, nblk - NAT_KROWS // NAT_QR)
    variant = jnp.where(blk == 0, 0, jnp.where(blk == nblk - 1, 2, 1))
    nq, nk = NAT_QR * GRID_W, NAT_KROWS * GRID_W
    off = pl.multiple_of(first * nq, nq)
    q = q_ref[0]
    k = k_ref[0, pl.ds(off, nk), :]
    v = v_ref[0, pl.ds(off, nk), :]
    lane_head = lax.broadcasted_iota(jnp.int32, (1, NAT_W), 1) // HEAD_DIM
    out = None
    for h in range(NAT_HEADS):
        s = _dot_nt(q * jnp.where(lane_head == h, 1.0, 0.0).astype(BF16), k) + bias_ref[variant, h]
        m = jnp.max(s, axis=-1, keepdims=True)
        p = jnp.exp(s - m)
        inv = 1.0 / jnp.sum(p, axis=-1, keepdims=True)
        o_h = _dot(p.astype(BF16), v) * inv
        out = o_h if out is None else jnp.where(lane_head == h, o_h, out)
    o_ref[0] = out.astype(BF16)


def _nat(q, k, v, bias):
    Bt, T, _ = q.shape
    rows = T // GRID_W
    nblk = rows // NAT_QR
    assert rows % NAT_QR == 0 and rows >= NAT_KROWS
    nq = NAT_QR * GRID_W
    return pl.pallas_call(
        functools.partial(_nat_kernel, nblk=nblk),
        grid=(Bt, nblk),
        in_specs=[
            pl.BlockSpec((1, nq, NAT_W), lambda b, r: (b, r, 0)),
            pl.BlockSpec((1, T, NAT_W), lambda b, r: (b, 0, 0)),
            pl.BlockSpec((1, T, NAT_W), lambda b, r: (b, 0, 0)),
            _const_spec((3, NAT_HEADS, nq, NAT_KROWS * GRID_W)),
        ],
        out_specs=pl.BlockSpec((1, nq, NAT_W), lambda b, r: (b, r, 0)),
        out_shape=jax.ShapeDtypeStruct((Bt, T, NAT_W), BF16),
        compiler_params=pltpu.CompilerParams(dimension_semantics=("parallel", "arbitrary"),
                                             vmem_limit_bytes=VMEM_LIMIT),
        name="nat",
    )(q, k, v, bias)


def _nat_bias_table(rpb, rows):
    nblk = rows // NAT_QR
    c = np.arange(GRID_W)[:, None]
    kc = np.arange(GRID_W)[None, :]
    qc0 = np.clip(c - NAT_KC // 2, 0, GRID_W - NAT_KC)
    col_ok = (kc >= qc0) & (kc < qc0 + NAT_KC)
    dcol = np.clip(kc - c + (NAT_KC - 1), 0, 2 * NAT_KC - 2)
    oh_col = (dcol[None] == np.arange(2 * NAT_KC - 1)[:, None, None]) & col_ok[None]
    oh_row = np.zeros((3, NAT_QR, NAT_KROWS, 2 * NAT_KR - 1), np.float32)
    for var, blk in enumerate((0, 1, nblk - 1)):
        first = int(np.clip(blk - 1, 0, nblk - NAT_KROWS // NAT_QR))
        for ri in range(NAT_QR):
            r = blk * NAT_QR + ri
            r0 = int(np.clip(r - NAT_KR // 2, 0, rows - NAT_KR))
            for j in range(NAT_KROWS):
                kr = first * NAT_QR + j
                if r0 <= kr < r0 + NAT_KR:
                    oh_row[var, ri, j, kr - r + NAT_KR - 1] = 1.0
    hp = lax.Precision.HIGHEST
    by_col = jnp.einsum('hde,eck->hdck', rpb.astype(F32), jnp.asarray(oh_col, F32), precision=hp)
    bias = jnp.einsum('vrjd,hdck->vhrcjk', jnp.asarray(oh_row), by_col, precision=hp)
    ok = (oh_row.sum(-1) > 0)[:, None, :, None, :, None] & col_ok[None, None, None, :, None, :]
    bias = jnp.where(jnp.asarray(ok), bias, NEG)
    return bias.reshape(3, NAT_HEADS, NAT_QR * GRID_W, NAT_KROWS * GRID_W)


def _hgrn_consts(Tt, C, rev):
    nc = Tt // C
    t = np.arange(Tt)
    row, col = t[:, None], t[None, :]
    same = ((row // C) == (col // C)) & ((col >= row) if rev else (col <= row))
    at = [(nc - 1 - p) if rev else p for p in range(nc)]
    pair = np.concatenate([np.broadcast_to((t // C) == at[p - 1], (C, Tt)) for p in (1, 3)])
    masks = np.concatenate([np.tile(same, (HGRN_HEADS, 1)), np.tile(pair, (HGRN_HEADS, 1))]).astype(np.float32)
    hl = np.arange(HGRN_W) // HEAD_DIM
    bd = (hl[:, None] == hl[None, :]).astype(np.float32)
    return jnp.asarray(same.astype(np.float32), BF16), jnp.asarray(masks), jnp.asarray(bd)


def _hgrn_kernel(lf_f, k_f, v_f, q_f, lf_b, k_b, v_b, q_b, tri_f, mask_f, tri_b, mask_b, bd_ref,
                 of_ref, ob_ref, st_f, st_b, *, Tt, C):
    @pl.when(pl.program_id(1) == 0)
    def _():
        st_f[...] = jnp.zeros_like(st_f)
        st_b[...] = jnp.zeros_like(st_b)

    tiles = []
    for bi in range(HGRN_BB):
        tiles.append((of_ref, bi) + _hgrn_tile(lf_f.at[bi], k_f.at[bi], v_f.at[bi], q_f.at[bi], tri_f, mask_f, bd_ref,
                                               st_f.at[bi], rev=False, Tt=Tt, C=C))
        tiles.append((ob_ref, bi) + _hgrn_tile(lf_b.at[bi], k_b.at[bi], v_b.at[bi], q_b.at[bi], tri_b, mask_b, bd_ref,
                                               st_b.at[bi], rev=True, Tt=Tt, C=C))
    for o_ref, bi, o, _, _ in tiles:
        o_ref[bi] = o.astype(BF16)

    @pl.when(functools.reduce(jnp.minimum, [t[3] for t in tiles]) < -HGRN_SAFE)
    def _():
        for o_ref, bi, _, _, redo in tiles:
            o_ref[bi] = redo().astype(BF16)


def _hgrn_tile(lf_ref, k_ref, v_ref, q_ref, tri_ref, mask_ref, bd_ref, st_ref, *, rev, Tt, C):
    nc = Tt // C
    lf = lf_ref[...]
    lane_head = lax.broadcasted_iota(jnp.int32, (1, HGRN_W), 1) // HEAD_DIM
    head_sel = [jnp.where(lane_head == h, 1.0, 0.0).astype(BF16) for h in range(HGRN_HEADS)]

    def stack_heads(x16):
        return jnp.concatenate([x16 * m for m in head_sel], axis=0)

    tri = tri_ref[...]
    hi = lf.astype(BF16)
    lo = (lf - hi.astype(F32)).astype(BF16)
    bc = _dot(tri, hi) + _dot(tri, lo)

    cs = [slice(c * C, (c + 1) * C) for c in range(nc)]
    pos = [(nc - 1 - c) if rev else c for c in range(nc)]
    at = [pos.index(p) for p in range(nc)]
    last = 0 if rev else C - 1
    tot = [bc[c * C + last:c * C + last + 1] for c in range(nc)]
    zero = jnp.zeros((C, HGRN_W), F32)
    cat = lambda xs: jnp.concatenate(xs, axis=0).astype(BF16)

    def intra_fast():
        ka = (k * jnp.exp(-bc)).astype(BF16)
        return jnp.where(mask_ref[0:HGRN_HEADS * Tt, :] > 0.5, _dot_nt(stack_heads(cat(qa)), ka), 0.0)

    def intra_safe():
        q = q_ref[...].astype(F32)
        k = k_ref[...].astype(F32)
        t_idx = lax.broadcasted_iota(jnp.int32, (Tt, 1), 0)
        tau = (Tt - 1 - t_idx) if rev else t_idx
        row = lax.broadcasted_iota(jnp.int32, (HGRN_HEADS * Tt, Tt), 0) % Tt
        col = lax.broadcasted_iota(jnp.int32, (HGRN_HEADS * Tt, Tt), 1)
        acc = jnp.where(row == col, _dot_nt(stack_heads(q_ref[...]), k_ref[...]), 0.0)
        before = bc - lf
        end = bc
        size = 2
        while size <= C:
            half = size // 2
            upper = (tau % size) >= half
            w = jnp.exp(jnp.where(upper, bc - before, end - bc))
            qt = jnp.where(upper, q * w, 0.0).astype(BF16)
            kt = jnp.where(upper, 0.0, k * w).astype(BF16)
            acc = acc + jnp.where((row // size) == (col // size), _dot_nt(stack_heads(qt), kt), 0.0)
            if size < C:
                back, fwd = (Tt - half, half) if rev else (half, Tt - half)
                before = jnp.where(upper, pltpu.roll(before, back, 0), before)
                end = jnp.where(upper, end, pltpu.roll(end, fwd, 0))
            size *= 2
        return acc

    q = q_ref[...].astype(F32)
    k = k_ref[...].astype(F32)
    v = v_ref[...]
    qa = [q[cs[c]] * jnp.exp(bc[cs[c]]) for c in range(nc)]
    kb = [k[cs[c]] * jnp.exp(tot[c] - bc[cs[c]]) for c in range(nc)]
    q_pair = cat([qa[at[1]], qa[at[3]]])
    k_pair = cat([kb[c] if pos[c] % 2 == 0 else zero for c in range(nc)])
    a_pair = jnp.where(mask_ref[HGRN_HEADS * Tt:, :] > 0.5, _dot_nt(stack_heads(q_pair), k_pair), 0.0)
    q_half = cat([qa[at[2]], qa[at[3]] * jnp.exp(tot[at[2]])])
    k_half = cat([zero if pos[c] >= 2 else (kb[c] * jnp.exp(tot[at[1]]) if pos[c] == 0 else kb[c]) for c in range(nc)])
    a_half = _dot_nt(stack_heads(q_half), k_half)

    def add_cross(a_intra):
        blocks = []
        for h in range(HGRN_HEADS):
            for c in range(nc):
                blk = a_intra[h * Tt + c * C:h * Tt + (c + 1) * C]
                if pos[c] % 2 == 1:
                    r0 = h * 2 * C + (pos[c] // 2) * C
                    blk = blk + a_pair[r0:r0 + C]
                if pos[c] >= 2:
                    r0 = h * 2 * C + (pos[c] - 2) * C
                    blk = blk + a_half[r0:r0 + C]
                blocks.append(blk)
        return jnp.concatenate(blocks, axis=0)

    pre, suf = [None] * nc, [None] * nc
    run = jnp.zeros_like(tot[0])
    for p in range(nc):
        pre[at[p]] = run
        run = run + tot[at[p]]
    b_tile = run
    run = jnp.zeros_like(tot[0])
    for p in reversed(range(nc)):
        suf[at[p]] = run
        run = run + tot[at[p]]
    q_in = cat([qa[c] * jnp.exp(pre[c]) for c in range(nc)])
    k_out = cat([kb[c] * jnp.exp(suf[c]) for c in range(nc)])

    st = st_ref[...]
    o_inter = _dot_nt(q_in, st.astype(BF16))
    st_ref[...] = st * jnp.exp(b_tile) + jnp.where(bd_ref[...] > 0.5, _dot_tn(v, k_out), 0.0)

    def output(a_intra):
        o_heads = _dot(add_cross(a_intra).astype(BF16), v)
        o = o_inter
        for h in range(HGRN_HEADS):
            o = o + jnp.where(lane_head == h, o_heads[h * Tt:(h + 1) * Tt], 0.0)
        return o

    return output(intra_fast()), jnp.min(functools.reduce(jnp.minimum, tot)), lambda: output(intra_safe())


def _hgrn(lf, hk, hv, hq):
    Bt, T, _ = hv.shape
    Tt, C = TT_HGRN, HGRN_CHUNK
    bb = HGRN_BB
    assert Tt == 4 * C and Bt % bb == 0
    nt = T // Tt
    fwd = lambda col: pl.BlockSpec((bb, Tt, HGRN_W), lambda b, n: (b, n, col))
    bwd = lambda col: pl.BlockSpec((bb, Tt, HGRN_W), lambda b, n: (b, nt - 1 - n, col))
    tri_f, mask_f, bd = _hgrn_consts(Tt, C, False)
    tri_b, mask_b, _ = _hgrn_consts(Tt, C, True)
    tri_spec, mask_spec = _const_spec((Tt, Tt)), _const_spec((HGRN_HEADS * (Tt + 2 * C), Tt))
    out = jax.ShapeDtypeStruct((Bt, T, HGRN_W), BF16)
    return pl.pallas_call(
        functools.partial(_hgrn_kernel, Tt=Tt, C=C),
        grid=(Bt // bb, nt),
        in_specs=[fwd(0), fwd(0), fwd(0), fwd(0), bwd(1), bwd(1), bwd(0), bwd(0),
                  tri_spec, mask_spec, tri_spec, mask_spec, _const_spec((HGRN_W, HGRN_W))],
        out_specs=[fwd(0), bwd(0)],
        out_shape=[out, out],
        scratch_shapes=[pltpu.VMEM((bb, HGRN_W, HGRN_W), F32), pltpu.VMEM((bb, HGRN_W, HGRN_W), F32)],
        compiler_params=pltpu.CompilerParams(dimension_semantics=("parallel", "arbitrary"),
                                             vmem_limit_bytes=VMEM_LIMIT),
        name="hgrn",
    )(lf, hk, hv, hq, lf, hk, hv, hq, tri_f, mask_f, tri_b, mask_b, bd)


def _merge_kernel(x_ref, u_ref, ob_ref, hf_ref, hb_ref, hg_ref, hn_ref, grp_ref, od_ref, g_ref, pw_ref, ps_ref,
                  wbp_ref, wbg_ref, wbh_ref, wbn_ref, wo_ref, o_ref, *, tm, T):
    i = pl.program_id(1)
    t0 = pl.multiple_of(i * tm, tm)
    win = tm + 2 * POOL_HALO
    w0 = pl.multiple_of(jnp.clip(t0 - POOL_HALO, 0, T - win), POOL_HALO)
    uw = u_ref[0, pl.ds(w0, win), :]
    uc = u_ref[0, pl.ds(t0, tm), :].astype(F32)
    tg = t0 + lax.broadcasted_iota(jnp.int32, (tm, 1), 0)
    sg = w0 + lax.broadcasted_iota(jnp.int32, (1, win), 1)
    lane = lax.broadcasted_iota(jnp.int32, (1, POOL_WIDTH), 1)
    mixed = jnp.zeros((tm, POOL_WIDTH), F32)
    for g, w in enumerate(POOL_WINDOWS):
        lo = jnp.maximum(tg - w // 2, 0)
        hi = jnp.minimum(tg + w // 2, T)
        inside = jnp.where(sg >= lo, jnp.where(sg < hi, 1.0, 0.0), 0.0).astype(BF16)
        mean = _dot(inside, uw) / (hi - lo).astype(F32)
        mixed = jnp.where(lane // POOL_GW == g, mean, mixed)
    mixed = mixed - uc
    oa = _dot(mixed.astype(BF16), pw_ref[...]) * ps_ref[...]
    hsum = hf_ref[0].astype(F32) + hb_ref[0].astype(F32)
    msq = _dot((hsum * hsum).astype(BF16), grp_ref[...])
    oc = hsum * lax.rsqrt(msq + EPS) * hn_ref[...] * hg_ref[0].astype(F32)
    gate = lambda c: g_ref[0, :, c * D_MODEL:(c + 1) * D_MODEL].astype(F32)
    merged = (gate(0) * _dot(oa.astype(BF16), wbp_ref[...])
              + gate(1) * _dot(ob_ref[0], wbg_ref[...])
              + gate(2) * _dot(oc.astype(BF16), wbh_ref[...])
              + gate(3) * _dot(od_ref[0], wbn_ref[...]))
    o_ref[0] = x_ref[0] + _dot(merged.astype(BF16), wo_ref[...])


def _merge(x, u, ob, hf, hb, hg, hn, grp, od, gates, pw, ps, wbp, wbg, wbh, wbn, wo):
    Bt, T, _ = x.shape
    tm = TM_MERGE
    tok = lambda w: pl.BlockSpec((1, tm, w), lambda b, i: (b, i, 0))
    return pl.pallas_call(
        functools.partial(_merge_kernel, tm=tm, T=T),
        grid=(Bt, T // tm),
        in_specs=[
            tok(D_MODEL),
            pl.BlockSpec((1, T, POOL_WIDTH), lambda b, i: (b, 0, 0)),
            tok(GQA_QW), tok(HGRN_W), tok(HGRN_W), tok(HGRN_W), _const_spec((1, HGRN_W)),
            _const_spec((HGRN_W, HGRN_W)), tok(NAT_W), tok(N_BRANCH * D_MODEL),
            _const_spec((POOL_WIDTH, POOL_WIDTH)), _const_spec((1, POOL_WIDTH)),
            _const_spec((POOL_WIDTH, D_MODEL)), _const_spec((GQA_QW, D_MODEL)),
            _const_spec((HGRN_W, D_MODEL)), _const_spec((NAT_W, D_MODEL)),
            _const_spec((D_MODEL, D_MODEL)),
        ],
        out_specs=tok(D_MODEL),
        out_shape=jax.ShapeDtypeStruct((Bt, T, D_MODEL), F32),
        compiler_params=pltpu.CompilerParams(dimension_semantics=("parallel", "arbitrary"),
                                             vmem_limit_bytes=VMEM_LIMIT),
        name="merge",
    )(x, u, ob, hf, hb, hg, hn, grp, od, gates, pw, ps, wbp, wbg, wbh, wbn, wo)


def _mlp_kernel(x_ref, g_ref, wu_ref, wd_ref, o_ref):
    x = x_ref[...]
    ms = jnp.mean(x * x, axis=-1, keepdims=True)
    h = (x * lax.rsqrt(ms + EPS) * g_ref[...]).astype(BF16)
    acc = x
    for c in range(D_FF // D_MODEL):
        sl = slice(c * D_MODEL, (c + 1) * D_MODEL)
        hid = jnp.maximum(_dot(h, wu_ref[:, sl]), 0.0)
        acc = acc + _dot((hid * hid).astype(BF16), wd_ref[sl, :])
    o_ref[...] = acc


def _mlp(x2, g2, wu, wd):
    N = x2.shape[0]
    tm = TM_MLP
    return pl.pallas_call(
        _mlp_kernel,
        grid=(N // tm,),
        in_specs=[pl.BlockSpec((tm, D_MODEL), lambda i: (i, 0)), _const_spec((1, D_MODEL)),
                  _const_spec((D_MODEL, D_FF)), _const_spec((D_FF, D_MODEL))],
        out_specs=pl.BlockSpec((tm, D_MODEL), lambda i: (i, 0)),
        out_shape=jax.ShapeDtypeStruct((N, D_MODEL), F32),
        compiler_params=pltpu.CompilerParams(dimension_semantics=("parallel",), vmem_limit_bytes=VMEM_LIMIT),
        name="mlp",
    )(x2, g2, wu, wd)


def _block_diag(blocks):
    n, a, b = blocks.shape
    eye = jnp.eye(n, dtype=blocks.dtype)
    return (eye[:, None, :, None] * blocks[:, :, None, :]).reshape(n * a, n * b)


def _prepare(T, norm1_g, w_in, pool_w, pool_scale, gqa_qnorm, gqa_knorm, gqa_sink, hgrn_lb, hgrn_onorm,
             nat_qnorm, nat_knorm, nat_rpb, w_br_pool, w_br_gqa, w_br_hgrn, w_br_nat, w_o, norm2_g, w_up, w_down):
    D = D_MODEL
    half = HEAD_DIM // 2
    inv = ROPE_THETA ** (-jnp.arange(half, dtype=F32) / half)
    ang = jnp.arange(T, dtype=F32)[:, None] * inv[None, :]
    cos = jnp.tile(jnp.concatenate([jnp.cos(ang), jnp.cos(ang)], axis=1), (1, 2))
    sin = jnp.tile(jnp.concatenate([-jnp.sin(ang), jnp.sin(ang)], axis=1), (1, 2))
    sm = jax.nn.softmax(hgrn_lb.astype(F32), axis=0)
    lower = jnp.cumsum(sm, axis=0) - sm[:1]
    grp = _block_diag(jnp.full((640 // HEAD_DIM, HEAD_DIM, HEAD_DIM), 1.0 / HEAD_DIM, F32)).astype(BF16)
    scale = HEAD_DIM ** -0.5
    order = jnp.asarray(GQA_HEAD_ORDER)
    layers = []
    for l in range(w_in.shape[0]):
        w_l = w_in[l].astype(BF16)
        w_q = w_l[:, POOL_WIDTH:POOL_WIDTH + GQA_QW].reshape(D, GQA_HEADS, HEAD_DIM)[:, order, :].reshape(D, GQA_QW)
        layers.append(dict(
            g1=norm1_g[l][None, :],
            w_in=jnp.concatenate([w_l[:, :POOL_WIDTH], w_q, w_l[:, POOL_WIDTH + GQA_QW:]], axis=1),
            qk_gain=jnp.concatenate([jnp.tile(gqa_qnorm[l] * scale, GQA_HEADS),
                                     jnp.tile(gqa_knorm[l], GQA_KV_HEADS)])[None, :],
            n_gain=jnp.concatenate([jnp.tile(nat_qnorm[l] * scale, NAT_HEADS),
                                    jnp.tile(nat_knorm[l], NAT_HEADS)])[None, :],
            lower=lower[l].reshape(1, 2 * HGRN_W),
            sink=gqa_sink[l].astype(F32)[order],
            nat_bias=_nat_bias_table(nat_rpb[l], T // GRID_W),
            onorm=jnp.tile(hgrn_onorm[l], HGRN_HEADS)[None, :],
            pool_w=_block_diag(pool_w[l]).astype(BF16), pool_scale=pool_scale[l][None, :],
            w_bp=w_br_pool[l].astype(BF16),
            w_bg=w_br_gqa[l].astype(BF16).reshape(GQA_HEADS, HEAD_DIM, D)[order].reshape(GQA_QW, D),
            w_bh=w_br_hgrn[l].astype(BF16), w_bn=w_br_nat[l].astype(BF16), w_o=w_o[l].astype(BF16),
            g2=norm2_g[l][None, :], w_up=w_up[l].astype(BF16), w_down=w_down[l].astype(BF16)))
    return dict(cos=cos, sin=sin, grp=grp, layers=layers)


def _trunk(x, prep):
    Bt, T, D = x.shape
    N = Bt * T
    grp = prep["grp"]
    r3 = lambda a: a.reshape(Bt, T, a.shape[-1])
    for p in prep["layers"]:
        (up, gq, gk, gv, lf, hk, hv, hq, hg, nq, nk, nv, gates) = _inproj(
            x.reshape(N, D), p["g1"], p["w_in"], prep["cos"], prep["sin"], p["qk_gain"], p["n_gain"], p["lower"], grp, T)
        ob = _gqa(r3(gq), r3(gk), r3(gv), p["sink"])
        od = _nat(r3(nq), r3(nk), r3(nv), p["nat_bias"])
        h_f, h_b = _hgrn(r3(lf), r3(hk), r3(hv), r3(hq))
        x = _merge(x, r3(up), ob, h_f, h_b, r3(hg), p["onorm"], grp[:HGRN_W, :HGRN_W], od, r3(gates),
                   p["pool_w"], p["pool_scale"],
                   p["w_bp"], p["w_bg"], p["w_bh"], p["w_bn"], p["w_o"])
        x = _mlp(x.reshape(N, D), p["g2"], p["w_up"], p["w_down"]).reshape(Bt, T, D)
    return x


def kernel(x_prompt, x_sample, norm1_g, w_in, pool_w, pool_scale, gqa_qnorm, gqa_knorm, gqa_sink, hgrn_lb,
           hgrn_onorm, nat_qnorm, nat_knorm, nat_rpb, w_br_pool, w_br_gqa, w_br_hgrn, w_br_nat, w_o, norm2_g,
           w_up, w_down):
    assert x_prompt.shape[1] == x_sample.shape[1]
    prep = _prepare(x_prompt.shape[1], norm1_g, w_in, pool_w, pool_scale, gqa_qnorm, gqa_knorm, gqa_sink, hgrn_lb,
                    hgrn_onorm, nat_qnorm, nat_knorm, nat_rpb, w_br_pool, w_br_gqa, w_br_hgrn, w_br_nat, w_o,
                    norm2_g, w_up, w_down)
    return (_trunk(x_prompt, prep), _trunk(x_sample, prep))
```

```python
import functools

import numpy as np

import jax
import jax.numpy as jnp
from jax import lax
from jax.experimental import pallas as pl
from jax.experimental.pallas import tpu as pltpu

F32 = jnp.float32
BF16 = jnp.bfloat16

D_MODEL = 1024
GRID_W = 64
EPS = 1e-6
NEG = -1e30
TINY = 1e-30
HEAD_DIM = 64
ROPE_THETA = 10000.0
N_BRANCH = 4
D_FF = 4 * D_MODEL

POOL_WIDTH = 256
POOL_WINDOWS = (2, 4, 8, 16)
POOL_GW = 64
POOL_HALO = 16

GQA_HEADS = 8
GQA_KV_HEADS = 2
GQA_GROUP = GQA_HEADS // GQA_KV_HEADS
GQA_WINDOW = 128
GQA_QW = GQA_HEADS * HEAD_DIM
GQA_KVW = GQA_KV_HEADS * HEAD_DIM

HGRN_HEADS = 4
HGRN_W = HGRN_HEADS * HEAD_DIM

NAT_HEADS = 4
NAT_KR = 8
NAT_KC = 16
NAT_W = NAT_HEADS * HEAD_DIM
NAT_QR = 4
NAT_KROWS = 12
NAT_SUB = 4
GQA_HEAD_ORDER = tuple((j % 2) * GQA_GROUP + j // 2 for j in range(GQA_HEADS))

IN_WIDTH = 7168
GATE_OFF = 3072

TM_IN = 512
TQ_GQA = 128
GQA_STACK = 8
GQA_SUB = 4
TT_HGRN = 128
HGRN_BB = 2
HGRN_CHUNK = 32
HGRN_SAFE = 60.0
TM_MERGE = 512
TM_MLP = 512
VMEM_LIMIT = 56 * 1024 * 1024


def _sigmoid(x):
    return 1.0 / (1.0 + jnp.exp(-x))


def _dot(a, b):
    return jnp.dot(a, b, preferred_element_type=F32)


def _dot_nt(a, b):
    return lax.dot_general(a, b, (((1,), (1,)), ((), ())), preferred_element_type=F32)


def _dot_tn(a, b):
    return lax.dot_general(a, b, (((0,), (0,)), ((), ())), preferred_element_type=F32)


def _const_spec(shape):
    n = len(shape)
    return pl.BlockSpec(shape, lambda *_: (0,) * n, pipeline_mode=pl.Buffered(1))


def _inproj_kernel(x_ref, g_ref, w_ref, cos_ref, sin_ref, qkg_ref, ng_ref, lb_ref, grp_ref,
                   up_ref, gq_ref, gk_ref, gv_ref, lf_ref, hk_ref, hv_ref, hq_ref, hg_ref,
                   nq_ref, nk_ref, nv_ref, gate_ref):
    x = x_ref[...]
    ms = jnp.mean(x * x, axis=-1, keepdims=True)
    h = (x * lax.rsqrt(ms + EPS) * g_ref[...]).astype(BF16)

    def headnorm(y, gain):
        w = y.shape[1]
        msq = _dot((y * y).astype(BF16), grp_ref[0:w, 0:w])
        return y * lax.rsqrt(msq + EPS) * gain

    z = _dot(h, w_ref[:, 0:1024])
    up_ref[...] = z[:, 0:POOL_WIDTH].astype(BF16)
    qkw = GQA_QW + GQA_KVW
    qk = headnorm(z[:, 256:256 + qkw], qkg_ref[...])
    reps = qkw // 128
    cos = jnp.concatenate([cos_ref[...]] * reps, axis=1)
    sin = jnp.concatenate([sin_ref[...]] * reps, axis=1)
    first = (lax.broadcasted_iota(jnp.int32, (1, qkw), 1) % HEAD_DIM) < (HEAD_DIM // 2)
    partner = jnp.where(first, pltpu.roll(qk, qkw - HEAD_DIM // 2, 1), pltpu.roll(qk, HEAD_DIM // 2, 1))
    qk = qk * cos + partner * sin
    gq_ref[...] = qk[:, 0:GQA_QW].astype(BF16)
    gk_ref[...] = qk[:, GQA_QW:qkw].astype(BF16)
    gv_ref[...] = z[:, 896:1024].astype(BF16)

    z = _dot(h, w_ref[:, 1024:2048])
    lb = lb_ref[...]
    f = lb + (1.0 - lb) * _sigmoid(z[:, 0:512])
    lf_ref[...] = jnp.log(jnp.maximum(f, TINY))
    hk_ref[...] = (1.0 - f).astype(BF16)
    hv_ref[...] = z[:, 512:768].astype(BF16)
    zq = z[:, 768:1024]
    hq_ref[...] = (zq * _sigmoid(zq)).astype(BF16)

    z = _dot(h, w_ref[:, 2048:3072])
    zg = z[:, 0:256]
    hg_ref[...] = (zg * _sigmoid(zg)).astype(BF16)
    nqk = headnorm(z[:, 256:768], ng_ref[...])
    nq_ref[...] = nqk[:, 0:256].astype(BF16)
    nk_ref[...] = nqk[:, 256:512].astype(BF16)
    nv_ref[...] = z[:, 768:1024].astype(BF16)

    for c in range(N_BRANCH):
        z = _dot(h, w_ref[:, GATE_OFF + c * D_MODEL:GATE_OFF + (c + 1) * D_MODEL])
        gate_ref[:, c * D_MODEL:(c + 1) * D_MODEL] = _sigmoid(z).astype(BF16)


def _inproj(x2, g1, w_in, cos, sin, qk_gain, n_gain, lb, grp, T):
    N = x2.shape[0]
    tm = TM_IN
    tpb = T // tm
    tok = lambda w: pl.BlockSpec((tm, w), lambda i: (i, 0))
    widths = [(POOL_WIDTH, BF16), (GQA_QW, BF16), (GQA_KVW, BF16), (GQA_KVW, BF16), (512, F32), (512, BF16),
              (256, BF16), (256, BF16), (256, BF16), (256, BF16), (256, BF16), (256, BF16), (4 * D_MODEL, BF16)]
    return pl.pallas_call(
        _inproj_kernel,
        grid=(N // tm,),
        in_specs=[
            tok(D_MODEL),
            _const_spec((1, D_MODEL)),
            _const_spec((D_MODEL, IN_WIDTH)),
            pl.BlockSpec((tm, 128), lambda i: (i % tpb, 0)),
            pl.BlockSpec((tm, 128), lambda i: (i % tpb, 0)),
            _const_spec((1, GQA_QW + GQA_KVW)),
            _const_spec((1, 2 * NAT_W)),
            _const_spec((1, 512)),
            _const_spec((640, 640)),
        ],
        out_specs=[tok(w) for w, _ in widths],
        out_shape=[jax.ShapeDtypeStruct((N, w), dt) for w, dt in widths],
        compiler_params=pltpu.CompilerParams(dimension_semantics=("parallel",), vmem_limit_bytes=VMEM_LIMIT),
        name="inproj",
    )(x2, g1, w_in, cos, sin, qk_gain, n_gain, lb, grp)


def _gqa_kernel(sink_ref, q_ref, k_ref, v_ref, o_ref, *, tq, T):
    for sub in range(GQA_SUB):
        t0 = (pl.program_id(1) * GQA_SUB + sub) * tq
        o_ref[0, sub * tq:(sub + 1) * tq, :] = _gqa_tile(sink_ref, q_ref[0, sub * tq:(sub + 1) * tq, :], k_ref, v_ref,
                                                        t0, tq, T)


def _gqa_tile(sink_ref, q, k_ref, v_ref, t0, tq, T):
    win = tq + 2 * GQA_WINDOW
    w0 = pl.multiple_of(jnp.clip(t0 - GQA_WINDOW, 0, T - win), 128)
    k = k_ref[0, pl.ds(w0, win), :]
    v = v_ref[0, pl.ds(w0, win), :]
    qpos = t0 + lax.broadcasted_iota(jnp.int32, (tq, 1), 0)
    kpos = w0 + lax.broadcasted_iota(jnp.int32, (1, win), 1)
    valid = jnp.abs(kpos - qpos) <= GQA_WINDOW
    lane = lax.broadcasted_iota(jnp.int32, (1, 2 * HEAD_DIM), 1)
    half = [jnp.where(lane < HEAD_DIM, 1.0, 0.0).astype(BF16), jnp.where(lane < HEAD_DIM, 0.0, 1.0).astype(BF16)]
    o_heads = []
    for j0 in range(0, GQA_HEADS, GQA_STACK):
        js = range(j0, j0 + GQA_STACK)
        qs = jnp.concatenate([q[:, (j // 2) * 128:(j // 2 + 1) * 128] * half[j % 2] for j in js], axis=0)
        s_all = _dot_nt(qs, k)
        ps, invs = [], []
        for i, j in enumerate(js):
            s = jnp.where(valid, s_all[i * tq:(i + 1) * tq], NEG)
            sink = sink_ref[j]
            m = jnp.maximum(jnp.max(s, axis=-1, keepdims=True), sink)
            p = jnp.exp(s - m)
            invs.append(1.0 / (jnp.sum(p, axis=-1, keepdims=True) + jnp.exp(sink - m)))
            ps.append(p.astype(BF16))
        o_all = _dot(jnp.concatenate(ps, axis=0), v)
        o_heads += [o_all[i * tq:(i + 1) * tq] * invs[i] for i in range(GQA_STACK)]
    outs = [jnp.where(lane < HEAD_DIM, o_heads[2 * pair], o_heads[2 * pair + 1]) for pair in range(GQA_HEADS // 2)]
    return jnp.concatenate(outs, axis=1).astype(BF16)


def _gqa(q, k, v, sink):
    Bt, T, _ = q.shape
    tq = TQ_GQA
    step = tq * GQA_SUB
    return pl.pallas_call(
        functools.partial(_gqa_kernel, tq=tq, T=T),
        grid=(Bt, T // step),
        in_specs=[
            pl.BlockSpec(memory_space=pltpu.SMEM),
            pl.BlockSpec((1, step, GQA_QW), lambda b, i: (b, i, 0)),
            pl.BlockSpec((1, T, GQA_KVW), lambda b, i: (b, 0, 0)),
            pl.BlockSpec((1, T, GQA_KVW), lambda b, i: (b, 0, 0)),
        ],
        out_specs=pl.BlockSpec((1, step, GQA_QW), lambda b, i: (b, i, 0)),
        out_shape=jax.ShapeDtypeStruct((Bt, T, GQA_QW), BF16),
        compiler_params=pltpu.CompilerParams(dimension_semantics=("parallel", "arbitrary"),
                                             vmem_limit_bytes=VMEM_LIMIT),
        name="gqa",
    )(sink, q, k, v)


def _nat_kernel(q_ref, k_ref, v_ref, bias_ref, o_ref, *, nblk):
    nq, nk = NAT_QR * GRID_W, NAT_KROWS * GRID_W
    lane_head = lax.broadcasted_iota(jnp.int32, (1, NAT_W), 1) // HEAD_DIM
    for sub in range(NAT_SUB):
        blk = pl.program_id(1) * NAT_SUB + sub
        first = jnp.clip(blk - 1, 0, nblk - NAT_KROWS // NAT_QR)
        variant = jnp.where(blk == 0, 0, jnp.where(blk == nblk - 1, 2, 1))
        off = pl.multiple_of(first * nq, nq)
        q = q_ref[0, sub * nq:(sub + 1) * nq, :]
        k = k_ref[0, pl.ds(off, nk), :]
        v = v_ref[0, pl.ds(off, nk), :]
        out = None
        for h in range(NAT_HEADS):
            s = _dot_nt(q * jnp.where(lane_head == h, 1.0, 0.0).astype(BF16), k) + bias_ref[variant, h]
            m = jnp.max(s, axis=-1, keepdims=True)
            p = jnp.exp(s - m)
            inv = 1.0 / jnp.sum(p, axis=-1, keepdims=True)
            o_h = _dot(p.astype(BF16), v) * inv
            out = o_h if out is None else jnp.where(lane_head == h, o_h, out)
        o_ref[0, sub * nq:(sub + 1) * nq, :] = out.astype(BF16)


def _nat(q, k, v, bias):
    Bt, T, _ = q.shape
    rows = T // GRID_W
    nblk = rows // NAT_QR
    assert rows % (NAT_QR * NAT_SUB) == 0 and rows >= NAT_KROWS
    nq = NAT_QR * GRID_W
    return pl.pallas_call(
        functools.partial(_nat_kernel, nblk=nblk),
        grid=(Bt, nblk // NAT_SUB),
        in_specs=[
            pl.BlockSpec((1, NAT_SUB * nq, NAT_W), lambda b, r: (b, r, 0)),
            pl.BlockSpec((1, T, NAT_W), lambda b, r: (b, 0, 0)),
            pl.BlockSpec((1, T, NAT_W), lambda b, r: (b, 0, 0)),
            _const_spec((3, NAT_HEADS, nq, NAT_KROWS * GRID_W)),
        ],
        out_specs=pl.BlockSpec((1, NAT_SUB * nq, NAT_W), lambda b, r: (b, r, 0)),
        out_shape=jax.ShapeDtypeStruct((Bt, T, NAT_W), BF16),
        compiler_params=pltpu.CompilerParams(dimension_semantics=("parallel", "arbitrary"),
                                             vmem_limit_bytes=VMEM_LIMIT),
        name="nat",
    )(q, k, v, bias)


def _nat_bias_table(rpb, rows):
    nblk = rows // NAT_QR
    c = np.arange(GRID_W)[:, None]
    kc = np.arange(GRID_W)[None, :]
    qc0 = np.clip(c - NAT_KC // 2, 0, GRID_W - NAT_KC)
    col_ok = (kc >= qc0) & (kc < qc0 + NAT_KC)
    dcol = np.clip(kc - c + (NAT_KC - 1), 0, 2 * NAT_KC - 2)
    oh_col = (dcol[None] == np.arange(2 * NAT_KC - 1)[:, None, None]) & col_ok[None]
    oh_row = np.zeros((3, NAT_QR, NAT_KROWS, 2 * NAT_KR - 1), np.float32)
    for var, blk in enumerate((0, 1, nblk - 1)):
        first = int(np.clip(blk - 1, 0, nblk - NAT_KROWS // NAT_QR))
        for ri in range(NAT_QR):
            r = blk * NAT_QR + ri
            r0 = int(np.clip(r - NAT_KR // 2, 0, rows - NAT_KR))
            for j in range(NAT_KROWS):
                kr = first * NAT_QR + j
                if r0 <= kr < r0 + NAT_KR:
                    oh_row[var, ri, j, kr - r + NAT_KR - 1] = 1.0
    hp = lax.Precision.HIGHEST
    by_col = jnp.einsum('hde,eck->hdck', rpb.astype(F32), jnp.asarray(oh_col, F32), precision=hp)
    bias = jnp.einsum('vrjd,hdck->vhrcjk', jnp.asarray(oh_row), by_col, precision=hp)
    ok = (oh_row.sum(-1) > 0)[:, None, :, None, :, None] & col_ok[None, None, None, :, None, :]
    bias = jnp.where(jnp.asarray(ok), bias, NEG)
    return bias.reshape(3, NAT_HEADS, NAT_QR * GRID_W, NAT_KROWS * GRID_W)


def _hgrn_consts(Tt, C, rev):
    nc = Tt // C
    t = np.arange(Tt)
    row, col = t[:, None], t[None, :]
    same = ((row // C) == (col // C)) & ((col >= row) if rev else (col <= row))
    at = [(nc - 1 - p) if rev else p for p in range(nc)]
    pair = np.concatenate([np.broadcast_to((t // C) == at[p - 1], (C, Tt)) for p in (1, 3)])
    masks = np.concatenate([np.tile(same, (HGRN_HEADS, 1)), np.tile(pair, (HGRN_HEADS, 1))]).astype(np.float32)
    hl = np.arange(HGRN_W) // HEAD_DIM
    bd = (hl[:, None] == hl[None, :]).astype(np.float32)
    return jnp.asarray(same.astype(np.float32), BF16), jnp.asarray(masks), jnp.asarray(bd)


def _hgrn_kernel(lf_f, k_f, v_f, q_f, lf_b, k_b, v_b, q_b, tri_f, mask_f, tri_b, mask_b, bd_ref,
                 of_ref, ob_ref, st_f, st_b, *, Tt, C):
    @pl.when(pl.program_id(1) == 0)
    def _():
        st_f[...] = jnp.zeros_like(st_f)
        st_b[...] = jnp.zeros_like(st_b)

    tiles = []
    for bi in range(HGRN_BB):
        tiles.append((of_ref, bi) + _hgrn_tile(lf_f.at[bi], k_f.at[bi], v_f.at[bi], q_f.at[bi], tri_f, mask_f, bd_ref,
                                               st_f.at[bi], rev=False, Tt=Tt, C=C))
        tiles.append((ob_ref, bi) + _hgrn_tile(lf_b.at[bi], k_b.at[bi], v_b.at[bi], q_b.at[bi], tri_b, mask_b, bd_ref,
                                               st_b.at[bi], rev=True, Tt=Tt, C=C))
    for o_ref, bi, o, _, _ in tiles:
        o_ref[bi] = o.astype(BF16)

    @pl.when(functools.reduce(jnp.minimum, [t[3] for t in tiles]) < -HGRN_SAFE)
    def _():
        for o_ref, bi, _, _, redo in tiles:
            o_ref[bi] = redo().astype(BF16)


def _hgrn_tile(lf_ref, k_ref, v_ref, q_ref, tri_ref, mask_ref, bd_ref, st_ref, *, rev, Tt, C):
    nc = Tt // C
    lf = lf_ref[...]
    lane_head = lax.broadcasted_iota(jnp.int32, (1, HGRN_W), 1) // HEAD_DIM
    head_sel = [jnp.where(lane_head == h, 1.0, 0.0).astype(BF16) for h in range(HGRN_HEADS)]

    def stack_heads(x16):
        return jnp.concatenate([x16 * m for m in head_sel], axis=0)

    tri = tri_ref[...]
    hi = lf.astype(BF16)
    lo = (lf - hi.astype(F32)).astype(BF16)
    bc = _dot(tri, hi) + _dot(tri, lo)

    cs = [slice(c * C, (c + 1) * C) for c in range(nc)]
    pos = [(nc - 1 - c) if rev else c for c in range(nc)]
    at = [pos.index(p) for p in range(nc)]
    last = 0 if rev else C - 1
    tot = [bc[c * C + last:c * C + last + 1] for c in range(nc)]
    zero = jnp.zeros((C, HGRN_W), F32)
    cat = lambda xs: jnp.concatenate(xs, axis=0).astype(BF16)

    def intra_fast():
        ka = (k * jnp.exp(-bc)).astype(BF16)
        return jnp.where(mask_ref[0:HGRN_HEADS * Tt, :] > 0.5, _dot_nt(stack_heads(cat(qa)), ka), 0.0)

    def intra_safe():
        q = q_ref[...].astype(F32)
        k = k_ref[...].astype(F32)
        t_idx = lax.broadcasted_iota(jnp.int32, (Tt, 1), 0)
        tau = (Tt - 1 - t_idx) if rev else t_idx
        row = lax.broadcasted_iota(jnp.int32, (HGRN_HEADS * Tt, Tt), 0) % Tt
        col = lax.broadcasted_iota(jnp.int32, (HGRN_HEADS * Tt, Tt), 1)
        acc = jnp.where(row == col, _dot_nt(stack_heads(q_ref[...]), k_ref[...]), 0.0)
        before = bc - lf
        end = bc
        size = 2
        while size <= C:
            half = size // 2
            upper = (tau % size) >= half
            w = jnp.exp(jnp.where(upper, bc - before, end - bc))
            qt = jnp.where(upper, q * w, 0.0).astype(BF16)
            kt = jnp.where(upper, 0.0, k * w).astype(BF16)
            acc = acc + jnp.where((row // size) == (col // size), _dot_nt(stack_heads(qt), kt), 0.0)
            if size < C:
                back, fwd = (Tt - half, half) if rev else (half, Tt - half)
                before = jnp.where(upper, pltpu.roll(before, back, 0), before)
                end = jnp.where(upper, end, pltpu.roll(end, fwd, 0))
            size *= 2
        return acc

    q = q_ref[...].astype(F32)
    k = k_ref[...].astype(F32)
    v = v_ref[...]
    qa = [q[cs[c]] * jnp.exp(bc[cs[c]]) for c in range(nc)]
    kb = [k[cs[c]] * jnp.exp(tot[c] - bc[cs[c]]) for c in range(nc)]
    q_pair = cat([qa[at[1]], qa[at[3]]])
    k_pair = cat([kb[c] if pos[c] % 2 == 0 else zero for c in range(nc)])
    a_pair = jnp.where(mask_ref[HGRN_HEADS * Tt:, :] > 0.5, _dot_nt(stack_heads(q_pair), k_pair), 0.0)
    q_half = cat([qa[at[2]], qa[at[3]] * jnp.exp(tot[at[2]])])
    k_half = cat([zero if pos[c] >= 2 else (kb[c] * jnp.exp(tot[at[1]]) if pos[c] == 0 else kb[c]) for c in range(nc)])
    a_half = _dot_nt(stack_heads(q_half), k_half)

    def add_cross(a_intra):
        blocks = []
        for h in range(HGRN_HEADS):
            for c in range(nc):
                blk = a_intra[h * Tt + c * C:h * Tt + (c + 1) * C]
                if pos[c] % 2 == 1:
                    r0 = h * 2 * C + (pos[c] // 2) * C
                    blk = blk + a_pair[r0:r0 + C]
                if pos[c] >= 2:
                    r0 = h * 2 * C + (pos[c] - 2) * C
                    blk = blk + a_half[r0:r0 + C]
                blocks.append(blk)
        return jnp.concatenate(blocks, axis=0)

    pre, suf = [None] * nc, [None] * nc
    run = jnp.zeros_like(tot[0])
    for p in range(nc):
        pre[at[p]] = run
        run = run + tot[at[p]]
    b_tile = run
    run = jnp.zeros_like(tot[0])
    for p in reversed(range(nc)):
        suf[at[p]] = run
        run = run + tot[at[p]]
    q_in = cat([qa[c] * jnp.exp(pre[c]) for c in range(nc)])
    k_out = cat([kb[c] * jnp.exp(suf[c]) for c in range(nc)])

    st = st_ref[...]
    o_inter = _dot_nt(q_in, st.astype(BF16))
    st_ref[...] = st * jnp.exp(b_tile) + jnp.where(bd_ref[...] > 0.5, _dot_tn(v, k_out), 0.0)

    def output(a_intra):
        o_heads = _dot(add_cross(a_intra).astype(BF16), v)
        o = o_inter
        for h in range(HGRN_HEADS):
            o = o + jnp.where(lane_head == h, o_heads[h * Tt:(h + 1) * Tt], 0.0)
        return o

    return output(intra_fast()), jnp.min(functools.reduce(jnp.minimum, tot)), lambda: output(intra_safe())


def _hgrn(lf, hk, hv, hq):
    Bt, T, _ = hv.shape
    Tt, C = TT_HGRN, HGRN_CHUNK
    bb = HGRN_BB
    assert Tt == 4 * C and Bt % bb == 0
    nt = T // Tt
    fwd = lambda col: pl.BlockSpec((bb, Tt, HGRN_W), lambda b, n: (b, n, col))
    bwd = lambda col: pl.BlockSpec((bb, Tt, HGRN_W), lambda b, n: (b, nt - 1 - n, col))
    tri_f, mask_f, bd = _hgrn_consts(Tt, C, False)
    tri_b, mask_b, _ = _hgrn_consts(Tt, C, True)
    tri_spec, mask_spec = _const_spec((Tt, Tt)), _const_spec((HGRN_HEADS * (Tt + 2 * C), Tt))
    out = jax.ShapeDtypeStruct((Bt, T, HGRN_W), BF16)
    return pl.pallas_call(
        functools.partial(_hgrn_kernel, Tt=Tt, C=C),
        grid=(Bt // bb, nt),
        in_specs=[fwd(0), fwd(0), fwd(0), fwd(0), bwd(1), bwd(1), bwd(0), bwd(0),
                  tri_spec, mask_spec, tri_spec, mask_spec, _const_spec((HGRN_W, HGRN_W))],
        out_specs=[fwd(0), bwd(0)],
        out_shape=[out, out],
        scratch_shapes=[pltpu.VMEM((bb, HGRN_W, HGRN_W), F32), pltpu.VMEM((bb, HGRN_W, HGRN_W), F32)],
        compiler_params=pltpu.CompilerParams(dimension_semantics=("parallel", "arbitrary"),
                                             vmem_limit_bytes=VMEM_LIMIT),
        name="hgrn",
    )(lf, hk, hv, hq, lf, hk, hv, hq, tri_f, mask_f, tri_b, mask_b, bd)


def _merge_kernel(x_ref, u_ref, ob_ref, hf_ref, hb_ref, hg_ref, hn_ref, grp_ref, od_ref, g_ref, pw_ref, ps_ref,
                  wbp_ref, wbg_ref, wbh_ref, wbn_ref, wo_ref, o_ref, *, tm, T):
    i = pl.program_id(1)
    t0 = pl.multiple_of(i * tm, tm)
    win = tm + 2 * POOL_HALO
    w0 = pl.multiple_of(jnp.clip(t0 - POOL_HALO, 0, T - win), POOL_HALO)
    uw = u_ref[0, pl.ds(w0, win), :]
    uc = u_ref[0, pl.ds(t0, tm), :].astype(F32)
    tg = t0 + lax.broadcasted_iota(jnp.int32, (tm, 1), 0)
    sg = w0 + lax.broadcasted_iota(jnp.int32, (1, win), 1)
    lane = lax.broadcasted_iota(jnp.int32, (1, POOL_WIDTH), 1)
    mixed = jnp.zeros((tm, POOL_WIDTH), F32)
    for g, w in enumerate(POOL_WINDOWS):
        lo = jnp.maximum(tg - w // 2, 0)
        hi = jnp.minimum(tg + w // 2, T)
        inside = jnp.where(sg >= lo, jnp.where(sg < hi, 1.0, 0.0), 0.0).astype(BF16)
        mean = _dot(inside, uw) / (hi - lo).astype(F32)
        mixed = jnp.where(lane // POOL_GW == g, mean, mixed)
    mixed = mixed - uc
    oa = _dot(mixed.astype(BF16), pw_ref[...]) * ps_ref[...]
    hsum = hf_ref[0].astype(F32) + hb_ref[0].astype(F32)
    msq = _dot((hsum * hsum).astype(BF16), grp_ref[...])
    oc = hsum * lax.rsqrt(msq + EPS) * hn_ref[...] * hg_ref[0].astype(F32)
    gate = lambda c: g_ref[0, :, c * D_MODEL:(c + 1) * D_MODEL].astype(F32)
    merged = (gate(0) * _dot(oa.astype(BF16), wbp_ref[...])
              + gate(1) * _dot(ob_ref[0], wbg_ref[...])
              + gate(2) * _dot(oc.astype(BF16), wbh_ref[...])
              + gate(3) * _dot(od_ref[0], wbn_ref[...]))
    o_ref[0] = x_ref[0] + _dot(merged.astype(BF16), wo_ref[...])


def _merge(x, u, ob, hf, hb, hg, hn, grp, od, gates, pw, ps, wbp, wbg, wbh, wbn, wo):
    Bt, T, _ = x.shape
    tm = TM_MERGE
    tok = lambda w: pl.BlockSpec((1, tm, w), lambda b, i: (b, i, 0))
    return pl.pallas_call(
        functools.partial(_merge_kernel, tm=tm, T=T),
        grid=(Bt, T // tm),
        in_specs=[
            tok(D_MODEL),
            pl.BlockSpec((1, T, POOL_WIDTH), lambda b, i: (b, 0, 0)),
            tok(GQA_QW), tok(HGRN_W), tok(HGRN_W), tok(HGRN_W), _const_spec((1, HGRN_W)),
            _const_spec((HGRN_W, HGRN_W)), tok(NAT_W), tok(N_BRANCH * D_MODEL),
            _const_spec((POOL_WIDTH, POOL_WIDTH)), _const_spec((1, POOL_WIDTH)),
            _const_spec((POOL_WIDTH, D_MODEL)), _const_spec((GQA_QW, D_MODEL)),
            _const_spec((HGRN_W, D_MODEL)), _const_spec((NAT_W, D_MODEL)),
            _const_spec((D_MODEL, D_MODEL)),
        ],
        out_specs=tok(D_MODEL),
        out_shape=jax.ShapeDtypeStruct((Bt, T, D_MODEL), F32),
        compiler_params=pltpu.CompilerParams(dimension_semantics=("parallel", "arbitrary"),
                                             vmem_limit_bytes=VMEM_LIMIT),
        name="merge",
    )(x, u, ob, hf, hb, hg, hn, grp, od, gates, pw, ps, wbp, wbg, wbh, wbn, wo)


def _mlp_kernel(x_ref, g_ref, wu_ref, wd_ref, o_ref):
    x = x_ref[...]
    ms = jnp.mean(x * x, axis=-1, keepdims=True)
    h = (x * lax.rsqrt(ms + EPS) * g_ref[...]).astype(BF16)
    acc = x
    for c in range(D_FF // D_MODEL):
        sl = slice(c * D_MODEL, (c + 1) * D_MODEL)
        hid = jnp.maximum(_dot(h, wu_ref[:, sl]), 0.0)
        acc = acc + _dot((hid * hid).astype(BF16), wd_ref[sl, :])
    o_ref[...] = acc


def _mlp(x2, g2, wu, wd):
    N = x2.shape[0]
    tm = TM_MLP
    return pl.pallas_call(
        _mlp_kernel,
        grid=(N // tm,),
        in_specs=[pl.BlockSpec((tm, D_MODEL), lambda i: (i, 0)), _const_spec((1, D_MODEL)),
                  _const_spec((D_MODEL, D_FF)), _const_spec((D_FF, D_MODEL))],
        out_specs=pl.BlockSpec((tm, D_MODEL), lambda i: (i, 0)),
        out_shape=jax.ShapeDtypeStruct((N, D_MODEL), F32),
        compiler_params=pltpu.CompilerParams(dimension_semantics=("parallel",), vmem_limit_bytes=VMEM_LIMIT),
        name="mlp",
    )(x2, g2, wu, wd)


def _block_diag(blocks):
    n, a, b = blocks.shape
    eye = jnp.eye(n, dtype=blocks.dtype)
    return (eye[:, None, :, None] * blocks[:, :, None, :]).reshape(n * a, n * b)


def _prepare(T, norm1_g, w_in, pool_w, pool_scale, gqa_qnorm, gqa_knorm, gqa_sink, hgrn_lb, hgrn_onorm,
             nat_qnorm, nat_knorm, nat_rpb, w_br_pool, w_br_gqa, w_br_hgrn, w_br_nat, w_o, norm2_g, w_up, w_down):
    D = D_MODEL
    half = HEAD_DIM // 2
    inv = ROPE_THETA ** (-jnp.arange(half, dtype=F32) / half)
    ang = jnp.arange(T, dtype=F32)[:, None] * inv[None, :]
    cos = jnp.tile(jnp.concatenate([jnp.cos(ang), jnp.cos(ang)], axis=1), (1, 2))
    sin = jnp.tile(jnp.concatenate([-jnp.sin(ang), jnp.sin(ang)], axis=1), (1, 2))
    sm = jax.nn.softmax(hgrn_lb.astype(F32), axis=0)
    lower = jnp.cumsum(sm, axis=0) - sm[:1]
    grp = _block_diag(jnp.full((640 // HEAD_DIM, HEAD_DIM, HEAD_DIM), 1.0 / HEAD_DIM, F32)).astype(BF16)
    scale = HEAD_DIM ** -0.5
    order = jnp.asarray(GQA_HEAD_ORDER)
    layers = []
    for l in range(w_in.shape[0]):
        w_l = w_in[l].astype(BF16)
        w_q = w_l[:, POOL_WIDTH:POOL_WIDTH + GQA_QW].reshape(D, GQA_HEADS, HEAD_DIM)[:, order, :].reshape(D, GQA_QW)
        layers.append(dict(
            g1=norm1_g[l][None, :],
            w_in=jnp.concatenate([w_l[:, :POOL_WIDTH], w_q, w_l[:, POOL_WIDTH + GQA_QW:]], axis=1),
            qk_gain=jnp.concatenate([jnp.tile(gqa_qnorm[l] * scale, GQA_HEADS),
                                     jnp.tile(gqa_knorm[l], GQA_KV_HEADS)])[None, :],
            n_gain=jnp.concatenate([jnp.tile(nat_qnorm[l] * scale, NAT_HEADS),
                                    jnp.tile(nat_knorm[l], NAT_HEADS)])[None, :],
            lower=lower[l].reshape(1, 2 * HGRN_W),
            sink=gqa_sink[l].astype(F32)[order],
            nat_bias=_nat_bias_table(nat_rpb[l], T // GRID_W),
            onorm=jnp.tile(hgrn_onorm[l], HGRN_HEADS)[None, :],
            pool_w=_block_diag(pool_w[l]).astype(BF16), pool_scale=pool_scale[l][None, :],
            w_bp=w_br_pool[l].astype(BF16),
            w_bg=w_br_gqa[l].astype(BF16).reshape(GQA_HEADS, HEAD_DIM, D)[order].reshape(GQA_QW, D),
            w_bh=w_br_hgrn[l].astype(BF16), w_bn=w_br_nat[l].astype(BF16), w_o=w_o[l].astype(BF16),
            g2=norm2_g[l][None, :], w_up=w_up[l].astype(BF16), w_down=w_down[l].astype(BF16)))
    return dict(cos=cos, sin=sin, grp=grp, layers=layers)


def _trunk(x, prep):
    Bt, T, D = x.shape
    N = Bt * T
    grp = prep["grp"]
    r3 = lambda a: a.reshape(Bt, T, a.shape[-1])
    for p in prep["layers"]:
        (up, gq, gk, gv, lf, hk, hv, hq, hg, nq, nk, nv, gates) = _inproj(
            x.reshape(N, D), p["g1"], p["w_in"], prep["cos"], prep["sin"], p["qk_gain"], p["n_gain"], p["lower"], grp, T)
        ob = _gqa(r3(gq), r3(gk), r3(gv), p["sink"])
        od = _nat(r3(nq), r3(nk), r3(nv), p["nat_bias"])
        h_f, h_b = _hgrn(r3(lf), r3(hk), r3(hv), r3(hq))
        x = _merge(x, r3(up), ob, h_f, h_b, r3(hg), p["onorm"], grp[:HGRN_W, :HGRN_W], od, r3(gates),
                   p["pool_w"], p["pool_scale"],
                   p["w_bp"], p["w_bg"], p["w_bh"], p["w_bn"], p["w_o"])
        x = _mlp(x.reshape(N, D), p["g2"], p["w_up"], p["w_down"]).reshape(Bt, T, D)
    return x


def kernel(x_prompt, x_sample, norm1_g, w_in, pool_w, pool_scale, gqa_qnorm, gqa_knorm, gqa_sink, hgrn_lb,
           hgrn_onorm, nat_qnorm, nat_knorm, nat_rpb, w_br_pool, w_br_gqa, w_br_hgrn, w_br_nat, w_o, norm2_g,
           w_up, w_down):
    assert x_prompt.shape[1] == x_sample.shape[1]
    prep = _prepare(x_prompt.shape[1], norm1_g, w_in, pool_w, pool_scale, gqa_qnorm, gqa_knorm, gqa_sink, hgrn_lb,
                    hgrn_onorm, nat_qnorm, nat_knorm, nat_rpb, w_br_pool, w_br_gqa, w_br_hgrn, w_br_nat, w_o,
                    norm2_g, w_up, w_down)
    return (_trunk(x_prompt, prep), _trunk(x_sample, prep))
```

```python
import functools

import numpy as np

import jax
import jax.numpy as jnp
from jax import lax
from jax.experimental import pallas as pl
from jax.experimental.pallas import tpu as pltpu

F32 = jnp.float32
BF16 = jnp.bfloat16

D_MODEL = 1024
GRID_W = 64
EPS = 1e-6
NEG = -1e30
TINY = 1e-30
HEAD_DIM = 64
ROPE_THETA = 10000.0
N_BRANCH = 4
D_FF = 4 * D_MODEL

POOL_WIDTH = 256
POOL_WINDOWS = (2, 4, 8, 16)
POOL_GW = 64
POOL_HALO = 16
POOL_PAD = 8

GQA_HEADS = 8
GQA_KV_HEADS = 2
GQA_GROUP = GQA_HEADS // GQA_KV_HEADS
GQA_WINDOW = 128
GQA_QW = GQA_HEADS * HEAD_DIM
GQA_KVW = GQA_KV_HEADS * HEAD_DIM

HGRN_HEADS = 4
HGRN_W = HGRN_HEADS * HEAD_DIM

NAT_HEADS = 4
NAT_KR = 8
NAT_KC = 16
NAT_W = NAT_HEADS * HEAD_DIM
NAT_QR = 4
NAT_KROWS = 12
NAT_SUB = 4
GQA_HEAD_ORDER = tuple((j % 2) * GQA_GROUP + j // 2 for j in range(GQA_HEADS))

IN_WIDTH = 7168
GATE_OFF = 3072

TM_IN = 512
TQ_GQA = 128
GQA_STACK = 8
GQA_SUB = 4
TT_HGRN = 128
HGRN_BB = 4
HGRN_CHUNK = 32
HGRN_SAFE = 60.0
TM_MERGE = 512
TM_MLP = 512
VMEM_LIMIT = 56 * 1024 * 1024


def _sigmoid(x):
    return 1.0 / (1.0 + jnp.exp(-x))


def _dot(a, b):
    return jnp.dot(a, b, preferred_element_type=F32)


def _dot_nt(a, b):
    return lax.dot_general(a, b, (((1,), (1,)), ((), ())), preferred_element_type=F32)


def _dot_tn(a, b):
    return lax.dot_general(a, b, (((0,), (0,)), ((), ())), preferred_element_type=F32)


def _const_spec(shape):
    n = len(shape)
    return pl.BlockSpec(shape, lambda *_: (0,) * n, pipeline_mode=pl.Buffered(1))


def _inproj_kernel(x_ref, g_ref, w_ref, cos_ref, sin_ref, qkg_ref, ng_ref, lb_ref, grp_ref,
                   up_ref, gq_ref, gk_ref, gv_ref, lf_ref, hk_ref, hv_ref, hq_ref, hg_ref,
                   nq_ref, nk_ref, nv_ref, gate_ref):
    x = x_ref[...]
    ms = jnp.mean(x * x, axis=-1, keepdims=True)
    h = (x * lax.rsqrt(ms + EPS) * g_ref[...]).astype(BF16)

    def headnorm(y, gain):
        w = y.shape[1]
        msq = _dot((y * y).astype(BF16), grp_ref[0:w, 0:w])
        return y * lax.rsqrt(msq + EPS) * gain

    z = _dot(h, w_ref[:, 0:1024])
    up_ref[...] = z[:, 0:POOL_WIDTH].astype(BF16)
    qkw = GQA_QW + GQA_KVW
    qk = headnorm(z[:, 256:256 + qkw], qkg_ref[...])
    reps = qkw // 128
    cos = jnp.concatenate([cos_ref[...]] * reps, axis=1)
    sin = jnp.concatenate([sin_ref[...]] * reps, axis=1)
    first = (lax.broadcasted_iota(jnp.int32, (1, qkw), 1) % HEAD_DIM) < (HEAD_DIM // 2)
    partner = jnp.where(first, pltpu.roll(qk, qkw - HEAD_DIM // 2, 1), pltpu.roll(qk, HEAD_DIM // 2, 1))
    qk = qk * cos + partner * sin
    gq_ref[...] = qk[:, 0:GQA_QW].astype(BF16)
    gk_ref[...] = qk[:, GQA_QW:qkw].astype(BF16)
    gv_ref[...] = z[:, 896:1024].astype(BF16)

    z = _dot(h, w_ref[:, 1024:2048])
    lb = lb_ref[...]
    f = lb + (1.0 - lb) * _sigmoid(z[:, 0:512])
    lf_ref[...] = jnp.log(jnp.maximum(f, TINY))
    hk_ref[...] = (1.0 - f).astype(BF16)
    hv_ref[...] = z[:, 512:768].astype(BF16)
    zq = z[:, 768:1024]
    hq_ref[...] = (zq * _sigmoid(zq)).astype(BF16)

    z = _dot(h, w_ref[:, 2048:3072])
    zg = z[:, 0:256]
    hg_ref[...] = (zg * _sigmoid(zg)).astype(BF16)
    nqk = headnorm(z[:, 256:768], ng_ref[...])
    nq_ref[...] = nqk[:, 0:256].astype(BF16)
    nk_ref[...] = nqk[:, 256:512].astype(BF16)
    nv_ref[...] = z[:, 768:1024].astype(BF16)

    for c in range(N_BRANCH):
        z = _dot(h, w_ref[:, GATE_OFF + c * D_MODEL:GATE_OFF + (c + 1) * D_MODEL])
        gate_ref[:, c * D_MODEL:(c + 1) * D_MODEL] = _sigmoid(z).astype(BF16)


def _inproj(x2, g1, w_in, cos, sin, qk_gain, n_gain, lb, grp, T):
    N = x2.shape[0]
    tm = TM_IN
    tpb = T // tm
    tok = lambda w: pl.BlockSpec((tm, w), lambda i: (i, 0))
    widths = [(POOL_WIDTH, BF16), (GQA_QW, BF16), (GQA_KVW, BF16), (GQA_KVW, BF16), (512, F32), (512, BF16),
              (256, BF16), (256, BF16), (256, BF16), (256, BF16), (256, BF16), (256, BF16), (4 * D_MODEL, BF16)]
    return pl.pallas_call(
        _inproj_kernel,
        grid=(N // tm,),
        in_specs=[
            tok(D_MODEL),
            _const_spec((1, D_MODEL)),
            _const_spec((D_MODEL, IN_WIDTH)),
            pl.BlockSpec((tm, 128), lambda i: (i % tpb, 0)),
            pl.BlockSpec((tm, 128), lambda i: (i % tpb, 0)),
            _const_spec((1, GQA_QW + GQA_KVW)),
            _const_spec((1, 2 * NAT_W)),
            _const_spec((1, 512)),
            _const_spec((640, 640)),
        ],
        out_specs=[tok(w) for w, _ in widths],
        out_shape=[jax.ShapeDtypeStruct((N, w), dt) for w, dt in widths],
        compiler_params=pltpu.CompilerParams(dimension_semantics=("parallel",), vmem_limit_bytes=VMEM_LIMIT),
        name="inproj",
    )(x2, g1, w_in, cos, sin, qk_gain, n_gain, lb, grp)


def _gqa_kernel(sink_ref, q_ref, k_ref, v_ref, o_ref, *, tq, T):
    for sub in range(GQA_SUB):
        t0 = (pl.program_id(1) * GQA_SUB + sub) * tq
        o_ref[0, sub * tq:(sub + 1) * tq, :] = _gqa_tile(sink_ref, q_ref[0, sub * tq:(sub + 1) * tq, :], k_ref, v_ref,
                                                        t0, tq, T)


def _gqa_tile(sink_ref, q, k_ref, v_ref, t0, tq, T):
    win = tq + 2 * GQA_WINDOW
    w0 = pl.multiple_of(jnp.clip(t0 - GQA_WINDOW, 0, T - win), 128)
    k = k_ref[0, pl.ds(w0, win), :]
    v = v_ref[0, pl.ds(w0, win), :]
    qpos = t0 + lax.broadcasted_iota(jnp.int32, (tq, 1), 0)
    kpos = w0 + lax.broadcasted_iota(jnp.int32, (1, win), 1)
    valid = jnp.abs(kpos - qpos) <= GQA_WINDOW
    lane = lax.broadcasted_iota(jnp.int32, (1, 2 * HEAD_DIM), 1)
    half = [jnp.where(lane < HEAD_DIM, 1.0, 0.0).astype(BF16), jnp.where(lane < HEAD_DIM, 0.0, 1.0).astype(BF16)]
    o_heads = []
    for j0 in range(0, GQA_HEADS, GQA_STACK):
        js = range(j0, j0 + GQA_STACK)
        qs = jnp.concatenate([q[:, (j // 2) * 128:(j // 2 + 1) * 128] * half[j % 2] for j in js], axis=0)
        s_all = _dot_nt(qs, k)
        ps, invs = [], []
        for i, j in enumerate(js):
            s = jnp.where(valid, s_all[i * tq:(i + 1) * tq], NEG)
            sink = sink_ref[j]
            m = jnp.maximum(jnp.max(s, axis=-1, keepdims=True), sink)
            p = jnp.exp(s - m)
            invs.append(1.0 / (jnp.sum(p, axis=-1, keepdims=True) + jnp.exp(sink - m)))
            ps.append(p.astype(BF16))
        o_all = _dot(jnp.concatenate(ps, axis=0), v)
        o_heads += [o_all[i * tq:(i + 1) * tq] * invs[i] for i in range(GQA_STACK)]
    outs = [jnp.where(lane < HEAD_DIM, o_heads[2 * pair], o_heads[2 * pair + 1]) for pair in range(GQA_HEADS // 2)]
    return jnp.concatenate(outs, axis=1).astype(BF16)


def _gqa(q, k, v, sink):
    Bt, T, _ = q.shape
    tq = TQ_GQA
    step = tq * GQA_SUB
    return pl.pallas_call(
        functools.partial(_gqa_kernel, tq=tq, T=T),
        grid=(Bt, T // step),
        in_specs=[
            pl.BlockSpec(memory_space=pltpu.SMEM),
            pl.BlockSpec((1, step, GQA_QW), lambda b, i: (b, i, 0)),
            pl.BlockSpec((1, T, GQA_KVW), lambda b, i: (b, 0, 0)),
            pl.BlockSpec((1, T, GQA_KVW), lambda b, i: (b, 0, 0)),
        ],
        out_specs=pl.BlockSpec((1, step, GQA_QW), lambda b, i: (b, i, 0)),
        out_shape=jax.ShapeDtypeStruct((Bt, T, GQA_QW), BF16),
        compiler_params=pltpu.CompilerParams(dimension_semantics=("parallel", "arbitrary"),
                                             vmem_limit_bytes=VMEM_LIMIT),
        name="gqa",
    )(sink, q, k, v)


def _nat_kernel(q_ref, k_ref, v_ref, bias_ref, o_ref, *, nblk):
    nq, nk = NAT_QR * GRID_W, NAT_KROWS * GRID_W
    lane_head = lax.broadcasted_iota(jnp.int32, (1, NAT_W), 1) // HEAD_DIM
    for sub in range(NAT_SUB):
        blk = pl.program_id(1) * NAT_SUB + sub
        first = jnp.clip(blk - 1, 0, nblk - NAT_KROWS // NAT_QR)
        variant = jnp.where(blk == 0, 0, jnp.where(blk == nblk - 1, 2, 1))
        off = pl.multiple_of(first * nq, nq)
        q = q_ref[0, sub * nq:(sub + 1) * nq, :]
        k = k_ref[0, pl.ds(off, nk), :]
        v = v_ref[0, pl.ds(off, nk), :]
        out = None
        for h in range(NAT_HEADS):
            s = _dot_nt(q * jnp.where(lane_head == h, 1.0, 0.0).astype(BF16), k) + bias_ref[variant, h]
            m = jnp.max(s, axis=-1, keepdims=True)
            p = jnp.exp(s - m)
            inv = 1.0 / jnp.sum(p, axis=-1, keepdims=True)
            o_h = _dot(p.astype(BF16), v) * inv
            out = o_h if out is None else jnp.where(lane_head == h, o_h, out)
        o_ref[0, sub * nq:(sub + 1) * nq, :] = out.astype(BF16)


def _nat(q, k, v, bias):
    Bt, T, _ = q.shape
    rows = T // GRID_W
    nblk = rows // NAT_QR
    assert rows % (NAT_QR * NAT_SUB) == 0 and rows >= NAT_KROWS
    nq = NAT_QR * GRID_W
    return pl.pallas_call(
        functools.partial(_nat_kernel, nblk=nblk),
        grid=(Bt, nblk // NAT_SUB),
        in_specs=[
            pl.BlockSpec((1, NAT_SUB * nq, NAT_W), lambda b, r: (b, r, 0)),
            pl.BlockSpec((1, T, NAT_W), lambda b, r: (b, 0, 0)),
            pl.BlockSpec((1, T, NAT_W), lambda b, r: (b, 0, 0)),
            _const_spec((3, NAT_HEADS, nq, NAT_KROWS * GRID_W)),
        ],
        out_specs=pl.BlockSpec((1, NAT_SUB * nq, NAT_W), lambda b, r: (b, r, 0)),
        out_shape=jax.ShapeDtypeStruct((Bt, T, NAT_W), BF16),
        compiler_params=pltpu.CompilerParams(dimension_semantics=("parallel", "arbitrary"),
                                             vmem_limit_bytes=VMEM_LIMIT),
        name="nat",
    )(q, k, v, bias)


def _nat_bias_table(rpb, rows):
    nblk = rows // NAT_QR
    c = np.arange(GRID_W)[:, None]
    kc = np.arange(GRID_W)[None, :]
    qc0 = np.clip(c - NAT_KC // 2, 0, GRID_W - NAT_KC)
    col_ok = (kc >= qc0) & (kc < qc0 + NAT_KC)
    dcol = np.clip(kc - c + (NAT_KC - 1), 0, 2 * NAT_KC - 2)
    oh_col = (dcol[None] == np.arange(2 * NAT_KC - 1)[:, None, None]) & col_ok[None]
    oh_row = np.zeros((3, NAT_QR, NAT_KROWS, 2 * NAT_KR - 1), np.float32)
    for var, blk in enumerate((0, 1, nblk - 1)):
        first = int(np.clip(blk - 1, 0, nblk - NAT_KROWS // NAT_QR))
        for ri in range(NAT_QR):
            r = blk * NAT_QR + ri
            r0 = int(np.clip(r - NAT_KR // 2, 0, rows - NAT_KR))
            for j in range(NAT_KROWS):
                kr = first * NAT_QR + j
                if r0 <= kr < r0 + NAT_KR:
                    oh_row[var, ri, j, kr - r + NAT_KR - 1] = 1.0
    hp = lax.Precision.HIGHEST
    by_col = jnp.einsum('hde,eck->hdck', rpb.astype(F32), jnp.asarray(oh_col, F32), precision=hp)
    bias = jnp.einsum('vrjd,hdck->vhrcjk', jnp.asarray(oh_row), by_col, precision=hp)
    ok = (oh_row.sum(-1) > 0)[:, None, :, None, :, None] & col_ok[None, None, None, :, None, :]
    bias = jnp.where(jnp.asarray(ok), bias, NEG)
    return bias.reshape(3, NAT_HEADS, NAT_QR * GRID_W, NAT_KROWS * GRID_W)


def _hgrn_consts(Tt, C, rev):
    nc = Tt // C
    t = np.arange(Tt)
    row, col = t[:, None], t[None, :]
    same = ((row // C) == (col // C)) & ((col >= row) if rev else (col <= row))
    at = [(nc - 1 - p) if rev else p for p in range(nc)]
    pair = np.concatenate([np.broadcast_to((t // C) == at[p - 1], (C, Tt)) for p in (1, 3)])
    masks = np.concatenate([np.tile(same, (HGRN_HEADS, 1)), np.tile(pair, (HGRN_HEADS, 1))]).astype(np.float32)
    hl = np.arange(HGRN_W) // HEAD_DIM
    bd = (hl[:, None] == hl[None, :]).astype(np.float32)
    return jnp.asarray(same.astype(np.float32), BF16), jnp.asarray(masks), jnp.asarray(bd)


def _hgrn_kernel(lf_f, k_f, v_f, q_f, lf_b, k_b, v_b, q_b, tri_f, mask_f, tri_b, mask_b, bd_ref,
                 of_ref, ob_ref, st_f, st_b, *, Tt, C):
    @pl.when(pl.program_id(1) == 0)
    def _():
        st_f[...] = jnp.zeros_like(st_f)
        st_b[...] = jnp.zeros_like(st_b)

    tiles = []
    for bi in range(HGRN_BB):
        tiles.append((of_ref, bi) + _hgrn_tile(lf_f.at[bi], k_f.at[bi], v_f.at[bi], q_f.at[bi], tri_f, mask_f, bd_ref,
                                               st_f.at[bi], rev=False, Tt=Tt, C=C))
        tiles.append((ob_ref, bi) + _hgrn_tile(lf_b.at[bi], k_b.at[bi], v_b.at[bi], q_b.at[bi], tri_b, mask_b, bd_ref,
                                               st_b.at[bi], rev=True, Tt=Tt, C=C))
    for o_ref, bi, o, _, _ in tiles:
        o_ref[bi] = o.astype(BF16)

    @pl.when(functools.reduce(jnp.minimum, [t[3] for t in tiles]) < -HGRN_SAFE)
    def _():
        for o_ref, bi, _, _, redo in tiles:
            o_ref[bi] = redo().astype(BF16)


def _hgrn_tile(lf_ref, k_ref, v_ref, q_ref, tri_ref, mask_ref, bd_ref, st_ref, *, rev, Tt, C):
    nc = Tt // C
    lf = lf_ref[...]
    lane_head = lax.broadcasted_iota(jnp.int32, (1, HGRN_W), 1) // HEAD_DIM
    head_sel = [jnp.where(lane_head == h, 1.0, 0.0).astype(BF16) for h in range(HGRN_HEADS)]

    def stack_heads(x16):
        return jnp.concatenate([x16 * m for m in head_sel], axis=0)

    tri = tri_ref[...]
    hi = lf.astype(BF16)
    lo = (lf - hi.astype(F32)).astype(BF16)
    bc = _dot(tri, hi) + _dot(tri, lo)

    cs = [slice(c * C, (c + 1) * C) for c in range(nc)]
    pos = [(nc - 1 - c) if rev else c for c in range(nc)]
    at = [pos.index(p) for p in range(nc)]
    last = 0 if rev else C - 1
    tot = [bc[c * C + last:c * C + last + 1] for c in range(nc)]
    zero = jnp.zeros((C, HGRN_W), F32)
    cat = lambda xs: jnp.concatenate(xs, axis=0).astype(BF16)

    def intra_fast():
        ka = (k * jnp.exp(-bc)).astype(BF16)
        return jnp.where(mask_ref[0:HGRN_HEADS * Tt, :] > 0.5, _dot_nt(stack_heads(cat(qa)), ka), 0.0)

    def intra_safe():
        q = q_ref[...].astype(F32)
        k = k_ref[...].astype(F32)
        t_idx = lax.broadcasted_iota(jnp.int32, (Tt, 1), 0)
        tau = (Tt - 1 - t_idx) if rev else t_idx
        row = lax.broadcasted_iota(jnp.int32, (HGRN_HEADS * Tt, Tt), 0) % Tt
        col = lax.broadcasted_iota(jnp.int32, (HGRN_HEADS * Tt, Tt), 1)
        acc = jnp.where(row == col, _dot_nt(stack_heads(q_ref[...]), k_ref[...]), 0.0)
        before = bc - lf
        end = bc
        size = 2
        while size <= C:
            half = size // 2
            upper = (tau % size) >= half
            w = jnp.exp(jnp.where(upper, bc - before, end - bc))
            qt = jnp.where(upper, q * w, 0.0).astype(BF16)
            kt = jnp.where(upper, 0.0, k * w).astype(BF16)
            acc = acc + jnp.where((row // size) == (col // size), _dot_nt(stack_heads(qt), kt), 0.0)
            if size < C:
                back, fwd = (Tt - half, half) if rev else (half, Tt - half)
                before = jnp.where(upper, pltpu.roll(before, back, 0), before)
                end = jnp.where(upper, end, pltpu.roll(end, fwd, 0))
            size *= 2
        return acc

    q = q_ref[...].astype(F32)
    k = k_ref[...].astype(F32)
    v = v_ref[...]
    qa = [q[cs[c]] * jnp.exp(bc[cs[c]]) for c in range(nc)]
    kb = [k[cs[c]] * jnp.exp(tot[c] - bc[cs[c]]) for c in range(nc)]
    q_pair = cat([qa[at[1]], qa[at[3]]])
    k_pair = cat([kb[c] if pos[c] % 2 == 0 else zero for c in range(nc)])
    a_pair = jnp.where(mask_ref[HGRN_HEADS * Tt:, :] > 0.5, _dot_nt(stack_heads(q_pair), k_pair), 0.0)
    q_half = cat([qa[at[2]], qa[at[3]] * jnp.exp(tot[at[2]])])
    k_half = cat([zero if pos[c] >= 2 else (kb[c] * jnp.exp(tot[at[1]]) if pos[c] == 0 else kb[c]) for c in range(nc)])
    a_half = _dot_nt(stack_heads(q_half), k_half)

    def add_cross(a_intra):
        blocks = []
        for h in range(HGRN_HEADS):
            for c in range(nc):
                blk = a_intra[h * Tt + c * C:h * Tt + (c + 1) * C]
                if pos[c] % 2 == 1:
                    r0 = h * 2 * C + (pos[c] // 2) * C
                    blk = blk + a_pair[r0:r0 + C]
                if pos[c] >= 2:
                    r0 = h * 2 * C + (pos[c] - 2) * C
                    blk = blk + a_half[r0:r0 + C]
                blocks.append(blk)
        return jnp.concatenate(blocks, axis=0)

    pre, suf = [None] * nc, [None] * nc
    run = jnp.zeros_like(tot[0])
    for p in range(nc):
        pre[at[p]] = run
        run = run + tot[at[p]]
    b_tile = run
    run = jnp.zeros_like(tot[0])
    for p in reversed(range(nc)):
        suf[at[p]] = run
        run = run + tot[at[p]]
    q_in = cat([qa[c] * jnp.exp(pre[c]) for c in range(nc)])
    k_out = cat([kb[c] * jnp.exp(suf[c]) for c in range(nc)])

    st = st_ref[...]
    o_inter = _dot_nt(q_in, st.astype(BF16))
    st_ref[...] = st * jnp.exp(b_tile) + jnp.where(bd_ref[...] > 0.5, _dot_tn(v, k_out), 0.0)

    def output(a_intra):
        o_heads = _dot(add_cross(a_intra).astype(BF16), v)
        o = o_inter
        for h in range(HGRN_HEADS):
            o = o + jnp.where(lane_head == h, o_heads[h * Tt:(h + 1) * Tt], 0.0)
        return o

    return output(intra_fast()), jnp.min(functools.reduce(jnp.minimum, tot)), lambda: output(intra_safe())


def _hgrn(lf, hk, hv, hq):
    Bt, T, _ = hv.shape
    Tt, C = TT_HGRN, HGRN_CHUNK
    bb = HGRN_BB
    assert Tt == 4 * C and Bt % bb == 0
    nt = T // Tt
    fwd = lambda col: pl.BlockSpec((bb, Tt, HGRN_W), lambda b, n: (b, n, col))
    bwd = lambda col: pl.BlockSpec((bb, Tt, HGRN_W), lambda b, n: (b, nt - 1 - n, col))
    tri_f, mask_f, bd = _hgrn_consts(Tt, C, False)
    tri_b, mask_b, _ = _hgrn_consts(Tt, C, True)
    tri_spec, mask_spec = _const_spec((Tt, Tt)), _const_spec((HGRN_HEADS * (Tt + 2 * C), Tt))
    out = jax.ShapeDtypeStruct((Bt, T, HGRN_W), BF16)
    return pl.pallas_call(
        functools.partial(_hgrn_kernel, Tt=Tt, C=C),
        grid=(Bt // bb, nt),
        in_specs=[fwd(0), fwd(0), fwd(0), fwd(0), bwd(1), bwd(1), bwd(0), bwd(0),
                  tri_spec, mask_spec, tri_spec, mask_spec, _const_spec((HGRN_W, HGRN_W))],
        out_specs=[fwd(0), bwd(0)],
        out_shape=[out, out],
        scratch_shapes=[pltpu.VMEM((bb, HGRN_W, HGRN_W), F32), pltpu.VMEM((bb, HGRN_W, HGRN_W), F32)],
        compiler_params=pltpu.CompilerParams(dimension_semantics=("parallel", "arbitrary"),
                                             vmem_limit_bytes=VMEM_LIMIT),
        name="hgrn",
    )(lf, hk, hv, hq, lf, hk, hv, hq, tri_f, mask_f, tri_b, mask_b, bd)


def _merge_kernel(x_ref, u_ref, ob_ref, hf_ref, hb_ref, hg_ref, hn_ref, grp_ref, od_ref, g_ref, pw_ref, ps_ref,
                  wbp_ref, wbg_ref, wbh_ref, wbn_ref, wo_ref, o_ref, pool_ref, *, tm, T):
    i = pl.program_id(1)
    t0 = pl.multiple_of(i * tm, tm)
    win = tm + 2 * POOL_HALO
    w0 = pl.multiple_of(jnp.clip(t0 - POOL_HALO, 0, T - win), POOL_HALO)
    pad = jnp.zeros((POOL_PAD, POOL_WIDTH), F32)
    run = jnp.concatenate([pad, u_ref[0, pl.ds(w0, win), :].astype(F32), pad], axis=0)
    rows = win + 2 * POOL_PAD
    group = lax.broadcasted_iota(jnp.int32, (1, POOL_WIDTH), 1) // POOL_GW
    sums, width = None, 1
    for g, w in enumerate(POOL_WINDOWS):
        while width < w:
            run = run + pltpu.roll(run, width, 0)
            width *= 2
        centred = run if w == 2 else pltpu.roll(run, rows - (w // 2 - 1), 0)
        sums = centred if sums is None else jnp.where(group == g, centred, sums)
    pool_ref[...] = sums
    half_w = functools.reduce(lambda acc, gw: jnp.where(group == gw[0], gw[1] // 2, acc),
                              list(enumerate(POOL_WINDOWS))[1:], POOL_WINDOWS[0] // 2)
    tg = t0 + lax.broadcasted_iota(jnp.int32, (tm, POOL_WIDTH), 0)
    cnt = (jnp.minimum(tg + half_w, T) - jnp.maximum(tg - half_w, 0)).astype(F32)
    uc = u_ref[0, pl.ds(t0, tm), :].astype(F32)
    mixed = pool_ref[pl.ds(pl.multiple_of(POOL_PAD + t0 - w0, 8), tm), :] / cnt - uc
    oa = _dot(mixed.astype(BF16), pw_ref[...]) * ps_ref[...]
    hsum = hf_ref[0].astype(F32) + hb_ref[0].astype(F32)
    msq = _dot((hsum * hsum).astype(BF16), grp_ref[...])
    oc = hsum * lax.rsqrt(msq + EPS) * hn_ref[...] * hg_ref[0].astype(F32)
    gate = lambda c: g_ref[0, :, c * D_MODEL:(c + 1) * D_MODEL].astype(F32)
    merged = (gate(0) * _dot(oa.astype(BF16), wbp_ref[...])
              + gate(1) * _dot(ob_ref[0], wbg_ref[...])
              + gate(2) * _dot(oc.astype(BF16), wbh_ref[...])
              + gate(3) * _dot(od_ref[0], wbn_ref[...]))
    o_ref[0] = x_ref[0] + _dot(merged.astype(BF16), wo_ref[...])


def _merge(x, u, ob, hf, hb, hg, hn, grp, od, gates, pw, ps, wbp, wbg, wbh, wbn, wo):
    Bt, T, _ = x.shape
    tm = TM_MERGE
    tok = lambda w: pl.BlockSpec((1, tm, w), lambda b, i: (b, i, 0))
    return pl.pallas_call(
        functools.partial(_merge_kernel, tm=tm, T=T),
        grid=(Bt, T // tm),
        in_specs=[
            tok(D_MODEL),
            pl.BlockSpec((1, T, POOL_WIDTH), lambda b, i: (b, 0, 0)),
            tok(GQA_QW), tok(HGRN_W), tok(HGRN_W), tok(HGRN_W), _const_spec((1, HGRN_W)),
            _const_spec((HGRN_W, HGRN_W)), tok(NAT_W), tok(N_BRANCH * D_MODEL),
            _const_spec((POOL_WIDTH, POOL_WIDTH)), _const_spec((1, POOL_WIDTH)),
            _const_spec((POOL_WIDTH, D_MODEL)), _const_spec((GQA_QW, D_MODEL)),
            _const_spec((HGRN_W, D_MODEL)), _const_spec((NAT_W, D_MODEL)),
            _const_spec((D_MODEL, D_MODEL)),
        ],
        out_specs=tok(D_MODEL),
        out_shape=jax.ShapeDtypeStruct((Bt, T, D_MODEL), F32),
        scratch_shapes=[pltpu.VMEM((tm + 2 * POOL_HALO + 2 * POOL_PAD, POOL_WIDTH), F32)],
        compiler_params=pltpu.CompilerParams(dimension_semantics=("parallel", "arbitrary"),
                                             vmem_limit_bytes=VMEM_LIMIT),
        name="merge",
    )(x, u, ob, hf, hb, hg, hn, grp, od, gates, pw, ps, wbp, wbg, wbh, wbn, wo)


def _mlp_kernel(x_ref, g_ref, wu_ref, wd_ref, o_ref):
    x = x_ref[...]
    ms = jnp.mean(x * x, axis=-1, keepdims=True)
    h = (x * lax.rsqrt(ms + EPS) * g_ref[...]).astype(BF16)
    acc = x
    for c in range(D_FF // D_MODEL):
        sl = slice(c * D_MODEL, (c + 1) * D_MODEL)
        hid = jnp.maximum(_dot(h, wu_ref[:, sl]), 0.0)
        acc = acc + _dot((hid * hid).astype(BF16), wd_ref[sl, :])
    o_ref[...] = acc


def _mlp(x2, g2, wu, wd):
    N = x2.shape[0]
    tm = TM_MLP
    return pl.pallas_call(
        _mlp_kernel,
        grid=(N // tm,),
        in_specs=[pl.BlockSpec((tm, D_MODEL), lambda i: (i, 0)), _const_spec((1, D_MODEL)),
                  _const_spec((D_MODEL, D_FF)), _const_spec((D_FF, D_MODEL))],
        out_specs=pl.BlockSpec((tm, D_MODEL), lambda i: (i, 0)),
        out_shape=jax.ShapeDtypeStruct((N, D_MODEL), F32),
        compiler_params=pltpu.CompilerParams(dimension_semantics=("parallel",), vmem_limit_bytes=VMEM_LIMIT),
        name="mlp",
    )(x2, g2, wu, wd)


def _block_diag(blocks):
    n, a, b = blocks.shape
    eye = jnp.eye(n, dtype=blocks.dtype)
    return (eye[:, None, :, None] * blocks[:, :, None, :]).reshape(n * a, n * b)


def _prepare(T, norm1_g, w_in, pool_w, pool_scale, gqa_qnorm, gqa_knorm, gqa_sink, hgrn_lb, hgrn_onorm,
             nat_qnorm, nat_knorm, nat_rpb, w_br_pool, w_br_gqa, w_br_hgrn, w_br_nat, w_o, norm2_g, w_up, w_down):
    D = D_MODEL
    half = HEAD_DIM // 2
    inv = ROPE_THETA ** (-jnp.arange(half, dtype=F32) / half)
    ang = jnp.arange(T, dtype=F32)[:, None] * inv[None, :]
    cos = jnp.tile(jnp.concatenate([jnp.cos(ang), jnp.cos(ang)], axis=1), (1, 2))
    sin = jnp.tile(jnp.concatenate([-jnp.sin(ang), jnp.sin(ang)], axis=1), (1, 2))
    sm = jax.nn.softmax(hgrn_lb.astype(F32), axis=0)
    lower = jnp.cumsum(sm, axis=0) - sm[:1]
    grp = _block_diag(jnp.full((640 // HEAD_DIM, HEAD_DIM, HEAD_DIM), 1.0 / HEAD_DIM, F32)).astype(BF16)
    scale = HEAD_DIM ** -0.5
    order = jnp.asarray(GQA_HEAD_ORDER)
    layers = []
    for l in range(w_in.shape[0]):
        w_l = w_in[l].astype(BF16)
        w_q = w_l[:, POOL_WIDTH:POOL_WIDTH + GQA_QW].reshape(D, GQA_HEADS, HEAD_DIM)[:, order, :].reshape(D, GQA_QW)
        layers.append(dict(
            g1=norm1_g[l][None, :],
            w_in=jnp.concatenate([w_l[:, :POOL_WIDTH], w_q, w_l[:, POOL_WIDTH + GQA_QW:]], axis=1),
            qk_gain=jnp.concatenate([jnp.tile(gqa_qnorm[l] * scale, GQA_HEADS),
                                     jnp.tile(gqa_knorm[l], GQA_KV_HEADS)])[None, :],
            n_gain=jnp.concatenate([jnp.tile(nat_qnorm[l] * scale, NAT_HEADS),
                                    jnp.tile(nat_knorm[l], NAT_HEADS)])[None, :],
            lower=lower[l].reshape(1, 2 * HGRN_W),
            sink=gqa_sink[l].astype(F32)[order],
            nat_bias=_nat_bias_table(nat_rpb[l], T // GRID_W),
            onorm=jnp.tile(hgrn_onorm[l], HGRN_HEADS)[None, :],
            pool_w=_block_diag(pool_w[l]).astype(BF16), pool_scale=pool_scale[l][None, :],
            w_bp=w_br_pool[l].astype(BF16),
            w_bg=w_br_gqa[l].astype(BF16).reshape(GQA_HEADS, HEAD_DIM, D)[order].reshape(GQA_QW, D),
            w_bh=w_br_hgrn[l].astype(BF16), w_bn=w_br_nat[l].astype(BF16), w_o=w_o[l].astype(BF16),
            g2=norm2_g[l][None, :], w_up=w_up[l].astype(BF16), w_down=w_down[l].astype(BF16)))
    return dict(cos=cos, sin=sin, grp=grp, layers=layers)


def _trunk(x, prep):
    Bt, T, D = x.shape
    N = Bt * T
    grp = prep["grp"]
    r3 = lambda a: a.reshape(Bt, T, a.shape[-1])
    for p in prep["layers"]:
        (up, gq, gk, gv, lf, hk, hv, hq, hg, nq, nk, nv, gates) = _inproj(
            x.reshape(N, D), p["g1"], p["w_in"], prep["cos"], prep["sin"], p["qk_gain"], p["n_gain"], p["lower"], grp, T)
        ob = _gqa(r3(gq), r3(gk), r3(gv), p["sink"])
        od = _nat(r3(nq), r3(nk), r3(nv), p["nat_bias"])
        h_f, h_b = _hgrn(r3(lf), r3(hk), r3(hv), r3(hq))
        x = _merge(x, r3(up), ob, h_f, h_b, r3(hg), p["onorm"], grp[:HGRN_W, :HGRN_W], od, r3(gates),
                   p["pool_w"], p["pool_scale"],
                   p["w_bp"], p["w_bg"], p["w_bh"], p["w_bn"], p["w_o"])
        x = _mlp(x.reshape(N, D), p["g2"], p["w_up"], p["w_down"]).reshape(Bt, T, D)
    return x


def kernel(x_prompt, x_sample, norm1_g, w_in, pool_w, pool_scale, gqa_qnorm, gqa_knorm, gqa_sink, hgrn_lb,
           hgrn_onorm, nat_qnorm, nat_knorm, nat_rpb, w_br_pool, w_br_gqa, w_br_hgrn, w_br_nat, w_o, norm2_g,
           w_up, w_down):
    assert x_prompt.shape[1] == x_sample.shape[1]
    prep = _prepare(x_prompt.shape[1], norm1_g, w_in, pool_w, pool_scale, gqa_qnorm, gqa_knorm, gqa_sink, hgrn_lb,
                    hgrn_onorm, nat_qnorm, nat_knorm, nat_rpb, w_br_pool, w_br_gqa, w_br_hgrn, w_br_nat, w_o,
                    norm2_g, w_up, w_down)
    return (_trunk(x_prompt, prep), _trunk(x_sample, prep))
```

```python
import functools

import numpy as np

import jax
import jax.numpy as jnp
from jax import lax
from jax.experimental import pallas as pl
from jax.experimental.pallas import tpu as pltpu

F32 = jnp.float32
BF16 = jnp.bfloat16

D_MODEL = 1024
GRID_W = 64
EPS = 1e-6
NEG = -1e30
TINY = 1e-30
HEAD_DIM = 64
ROPE_THETA = 10000.0
N_BRANCH = 4
D_FF = 4 * D_MODEL

POOL_WIDTH = 256
POOL_WINDOWS = (2, 4, 8, 16)
POOL_GW = 64
POOL_HALO = 16
POOL_PAD = 8

GQA_HEADS = 8
GQA_KV_HEADS = 2
GQA_GROUP = GQA_HEADS // GQA_KV_HEADS
GQA_WINDOW = 128
GQA_QW = GQA_HEADS * HEAD_DIM
GQA_KVW = GQA_KV_HEADS * HEAD_DIM

HGRN_HEADS = 4
HGRN_W = HGRN_HEADS * HEAD_DIM

NAT_HEADS = 4
NAT_KR = 8
NAT_KC = 16
NAT_W = NAT_HEADS * HEAD_DIM
NAT_QR = 4
NAT_KROWS = 12
NAT_SUB = 4
GQA_HEAD_ORDER = tuple((j % 2) * GQA_GROUP + j // 2 for j in range(GQA_HEADS))

_IN_GROUPS = (POOL_WIDTH, GQA_QW + GQA_KVW, GQA_KVW, 2 * HGRN_W, HGRN_W, HGRN_W, HGRN_W, 2 * NAT_W, NAT_W) \
    + (D_MODEL,) * N_BRANCH
_IN_EDGES = tuple(int(e) for e in np.cumsum((0,) + _IN_GROUPS))
IN_WIDTH = _IN_EDGES[-1]
IN_BLOCKS = tuple(tuple((a, b) for a, b in zip(_IN_EDGES[:-1], _IN_EDGES[1:]) if a // D_MODEL == blk)
                  for blk in range(IN_WIDTH // D_MODEL))

TM_IN = 512
TQ_GQA = 128
GQA_SUB = 4
TT_HGRN = 128
HGRN_BB = 4
HGRN_CHUNK = 32
HGRN_SAFE = 60.0
TM_MERGE = 512
TM_MLP = 512
V7X_VMEM_BYTES = 64 * 1024 * 1024
VMEM_LIMIT = V7X_VMEM_BYTES * 7 // 8


def _sigmoid(x):
    return 0.5 * jnp.tanh(0.5 * x) + 0.5


def _dot(a, b):
    return jnp.dot(a, b, preferred_element_type=F32)


def _dot_nt(a, b):
    return lax.dot_general(a, b, (((1,), (1,)), ((), ())), preferred_element_type=F32)


def _dot_tn(a, b):
    return lax.dot_general(a, b, (((0,), (0,)), ((), ())), preferred_element_type=F32)


def _const_spec(shape):
    n = len(shape)
    return pl.BlockSpec(shape, lambda *_: (0,) * n, pipeline_mode=pl.Buffered(1))


def _inproj_kernel(x_ref, g_ref, w_ref, cos_ref, sin_ref, qkg_ref, ng_ref, lb_ref, grp_ref,
                   up_ref, gq_ref, gk_ref, gv_ref, lf_ref, hk_ref, hv_ref, hq_ref, hg_ref,
                   nq_ref, nk_ref, nv_ref, gate_ref):
    x = x_ref[...]
    ms = jnp.mean(x * x, axis=-1, keepdims=True)
    h = (x * lax.rsqrt(ms + EPS) * g_ref[...]).astype(BF16)

    def headnorm(y, gain):
        w = y.shape[1]
        msq = _dot((y * y).astype(BF16), grp_ref[0:w, 0:w])
        return y * lax.rsqrt(msq + EPS) * gain

    def cols(block):
        z = _dot(h, w_ref[:, block * D_MODEL:(block + 1) * D_MODEL])
        return [z[:, a - block * D_MODEL:b - block * D_MODEL] for a, b in IN_BLOCKS[block]]

    zp, zqk, zv = cols(0)
    up_ref[...] = zp.astype(BF16)
    qk = headnorm(zqk, qkg_ref[...])
    qkw = GQA_QW + GQA_KVW
    reps = qkw // cos_ref.shape[1]
    cos = jnp.concatenate([cos_ref[...]] * reps, axis=1)
    sin = jnp.concatenate([sin_ref[...]] * reps, axis=1)
    first = (lax.broadcasted_iota(jnp.int32, (1, qkw), 1) % HEAD_DIM) < (HEAD_DIM // 2)
    partner = jnp.where(first, pltpu.roll(qk, qkw - HEAD_DIM // 2, 1), pltpu.roll(qk, HEAD_DIM // 2, 1))
    qk = qk * cos + partner * sin
    gq_ref[...] = qk[:, 0:GQA_QW].astype(BF16)
    gk_ref[...] = qk[:, GQA_QW:qkw].astype(BF16)
    gv_ref[...] = zv.astype(BF16)

    zf, zi, zq = cols(1)
    lb = lb_ref[...]
    f = lb + (1.0 - lb) / (1.0 + jnp.exp(-zf))
    lf_ref[...] = jnp.log(jnp.maximum(f, TINY))
    hk_ref[...] = (1.0 - f).astype(BF16)
    hv_ref[...] = zi.astype(BF16)
    hq_ref[...] = (zq * _sigmoid(zq)).astype(BF16)

    zg, znqk, znv = cols(2)
    hg_ref[...] = (zg * _sigmoid(zg)).astype(BF16)
    nqk = headnorm(znqk, ng_ref[...])
    nq_ref[...] = nqk[:, 0:NAT_W].astype(BF16)
    nk_ref[...] = nqk[:, NAT_W:2 * NAT_W].astype(BF16)
    nv_ref[...] = znv.astype(BF16)

    for c in range(N_BRANCH):
        (z,) = cols(3 + c)
        gate_ref[:, c * D_MODEL:(c + 1) * D_MODEL] = _sigmoid(z).astype(BF16)


def _inproj(x2, g1, w_in, cos, sin, qk_gain, n_gain, lb, grp, T):
    N = x2.shape[0]
    tm = TM_IN
    tpb = T // tm
    tok = lambda w: pl.BlockSpec((tm, w), lambda i: (i, 0))
    widths = [(POOL_WIDTH, BF16), (GQA_QW, BF16), (GQA_KVW, BF16), (GQA_KVW, BF16), (2 * HGRN_W, F32),
              (2 * HGRN_W, BF16), (HGRN_W, BF16), (HGRN_W, BF16), (HGRN_W, BF16), (NAT_W, BF16), (NAT_W, BF16),
              (NAT_W, BF16), (N_BRANCH * D_MODEL, BF16)]
    return pl.pallas_call(
        _inproj_kernel,
        grid=(N // tm,),
        in_specs=[
            tok(D_MODEL),
            _const_spec((1, D_MODEL)),
            _const_spec((D_MODEL, IN_WIDTH)),
            pl.BlockSpec((tm, cos.shape[1]), lambda i: (i % tpb, 0)),
            pl.BlockSpec((tm, sin.shape[1]), lambda i: (i % tpb, 0)),
            _const_spec((1, GQA_QW + GQA_KVW)),
            _const_spec((1, 2 * NAT_W)),
            _const_spec((1, 2 * HGRN_W)),
            _const_spec(grp.shape),
        ],
        out_specs=[tok(w) for w, _ in widths],
        out_shape=[jax.ShapeDtypeStruct((N, w), dt) for w, dt in widths],
        compiler_params=pltpu.CompilerParams(dimension_semantics=("parallel",), vmem_limit_bytes=VMEM_LIMIT),
        name="inproj",
    )(x2, g1, w_in, cos, sin, qk_gain, n_gain, lb, grp)


def _gqa_kernel(sink_ref, q_ref, k_ref, v_ref, o_ref, *, tq, T):
    for sub in range(GQA_SUB):
        t0 = (pl.program_id(1) * GQA_SUB + sub) * tq
        o_ref[0, sub * tq:(sub + 1) * tq, :] = _gqa_tile(sink_ref, q_ref[0, sub * tq:(sub + 1) * tq, :], k_ref, v_ref,
                                                        t0, tq, T)


def _gqa_tile(sink_ref, q, k_ref, v_ref, t0, tq, T):
    win = tq + 2 * GQA_WINDOW
    w0 = pl.multiple_of(jnp.clip(t0 - GQA_WINDOW, 0, T - win), GQA_WINDOW)
    k = k_ref[0, pl.ds(w0, win), :]
    v = v_ref[0, pl.ds(w0, win), :]
    qpos = t0 + lax.broadcasted_iota(jnp.int32, (tq, 1), 0)
    kpos = w0 + lax.broadcasted_iota(jnp.int32, (1, win), 1)
    valid = jnp.abs(kpos - qpos) <= GQA_WINDOW
    pw = 2 * HEAD_DIM
    lane = lax.broadcasted_iota(jnp.int32, (1, pw), 1)
    half = [jnp.where(lane < HEAD_DIM, 1.0, 0.0).astype(BF16), jnp.where(lane < HEAD_DIM, 0.0, 1.0).astype(BF16)]
    qs = jnp.concatenate([q[:, (j // 2) * pw:(j // 2 + 1) * pw] * half[j % 2] for j in range(GQA_HEADS)], axis=0)
    s_all = _dot_nt(qs, k)
    ps, invs = [], []
    for j in range(GQA_HEADS):
        s = jnp.where(valid, s_all[j * tq:(j + 1) * tq], NEG)
        sink = sink_ref[j]
        m = jnp.maximum(jnp.max(s, axis=-1, keepdims=True), sink)
        p = jnp.exp(s - m)
        invs.append(1.0 / (jnp.sum(p, axis=-1, keepdims=True) + jnp.exp(sink - m)))
        ps.append(p.astype(BF16))
    o_all = _dot(jnp.concatenate(ps, axis=0), v)
    o_heads = [o_all[j * tq:(j + 1) * tq] * invs[j] for j in range(GQA_HEADS)]
    outs = [jnp.where(lane < HEAD_DIM, o_heads[2 * pair], o_heads[2 * pair + 1]) for pair in range(GQA_HEADS // 2)]
    return jnp.concatenate(outs, axis=1).astype(BF16)


def _gqa(q, k, v, sink):
    Bt, T, _ = q.shape
    tq = TQ_GQA
    step = tq * GQA_SUB
    return pl.pallas_call(
        functools.partial(_gqa_kernel, tq=tq, T=T),
        grid=(Bt, T // step),
        in_specs=[
            pl.BlockSpec(memory_space=pltpu.SMEM),
            pl.BlockSpec((1, step, GQA_QW), lambda b, i: (b, i, 0)),
            pl.BlockSpec((1, T, GQA_KVW), lambda b, i: (b, 0, 0)),
            pl.BlockSpec((1, T, GQA_KVW), lambda b, i: (b, 0, 0)),
        ],
        out_specs=pl.BlockSpec((1, step, GQA_QW), lambda b, i: (b, i, 0)),
        out_shape=jax.ShapeDtypeStruct((Bt, T, GQA_QW), BF16),
        compiler_params=pltpu.CompilerParams(dimension_semantics=("parallel", "arbitrary"),
                                             vmem_limit_bytes=VMEM_LIMIT),
        name="gqa",
    )(sink, q, k, v)


def _nat_kernel(q_ref, k_ref, v_ref, bias_ref, o_ref, *, nblk):
    nq, nk = NAT_QR * GRID_W, NAT_KROWS * GRID_W
    lane_head = lax.broadcasted_iota(jnp.int32, (1, NAT_W), 1) // HEAD_DIM
    for sub in range(NAT_SUB):
        blk = pl.program_id(1) * NAT_SUB + sub
        first = jnp.clip(blk - 1, 0, nblk - NAT_KROWS // NAT_QR)
        variant = jnp.where(blk == 0, 0, jnp.where(blk == nblk - 1, 2, 1))
        off = pl.multiple_of(first * nq, nq)
        q = q_ref[0, sub * nq:(sub + 1) * nq, :]
        k = k_ref[0, pl.ds(off, nk), :]
        v = v_ref[0, pl.ds(off, nk), :]
        out = None
        for h in range(NAT_HEADS):
            s = _dot_nt(q * jnp.where(lane_head == h, 1.0, 0.0).astype(BF16), k) + bias_ref[variant, h]
            m = jnp.max(s, axis=-1, keepdims=True)
            p = jnp.exp(s - m)
            inv = 1.0 / jnp.sum(p, axis=-1, keepdims=True)
            o_h = _dot(p.astype(BF16), v) * inv
            out = o_h if out is None else jnp.where(lane_head == h, o_h, out)
        o_ref[0, sub * nq:(sub + 1) * nq, :] = out.astype(BF16)


def _nat(q, k, v, bias):
    Bt, T, _ = q.shape
    rows = T // GRID_W
    nblk = rows // NAT_QR
    assert rows % (NAT_QR * NAT_SUB) == 0 and rows >= NAT_KROWS
    nq = NAT_QR * GRID_W
    return pl.pallas_call(
        functools.partial(_nat_kernel, nblk=nblk),
        grid=(Bt, nblk // NAT_SUB),
        in_specs=[
            pl.BlockSpec((1, NAT_SUB * nq, NAT_W), lambda b, r: (b, r, 0)),
            pl.BlockSpec((1, T, NAT_W), lambda b, r: (b, 0, 0)),
            pl.BlockSpec((1, T, NAT_W), lambda b, r: (b, 0, 0)),
            _const_spec((3, NAT_HEADS, nq, NAT_KROWS * GRID_W)),
        ],
        out_specs=pl.BlockSpec((1, NAT_SUB * nq, NAT_W), lambda b, r: (b, r, 0)),
        out_shape=jax.ShapeDtypeStruct((Bt, T, NAT_W), BF16),
        compiler_params=pltpu.CompilerParams(dimension_semantics=("parallel", "arbitrary"),
                                             vmem_limit_bytes=VMEM_LIMIT),
        name="nat",
    )(q, k, v, bias)


def _nat_bias_table(rpb, rows):
    nblk = rows // NAT_QR
    c = np.arange(GRID_W)[:, None]
    kc = np.arange(GRID_W)[None, :]
    qc0 = np.clip(c - NAT_KC // 2, 0, GRID_W - NAT_KC)
    col_ok = (kc >= qc0) & (kc < qc0 + NAT_KC)
    dcol = np.clip(kc - c + (NAT_KC - 1), 0, 2 * NAT_KC - 2)
    oh_col = (dcol[None] == np.arange(2 * NAT_KC - 1)[:, None, None]) & col_ok[None]
    oh_row = np.zeros((3, NAT_QR, NAT_KROWS, 2 * NAT_KR - 1), np.float32)
    for var, blk in enumerate((0, 1, nblk - 1)):
        first = int(np.clip(blk - 1, 0, nblk - NAT_KROWS // NAT_QR))
        for ri in range(NAT_QR):
            r = blk * NAT_QR + ri
            r0 = int(np.clip(r - NAT_KR // 2, 0, rows - NAT_KR))
            for j in range(NAT_KROWS):
                kr = first * NAT_QR + j
                if r0 <= kr < r0 + NAT_KR:
                    oh_row[var, ri, j, kr - r + NAT_KR - 1] = 1.0
    hp = lax.Precision.HIGHEST
    by_col = jnp.einsum('hde,eck->hdck', rpb.astype(F32), jnp.asarray(oh_col, F32), precision=hp)
    bias = jnp.einsum('vrjd,hdck->vhrcjk', jnp.asarray(oh_row), by_col, precision=hp)
    ok = (oh_row.sum(-1) > 0)[:, None, :, None, :, None] & col_ok[None, None, None, :, None, :]
    bias = jnp.where(jnp.asarray(ok), bias, NEG)
    return bias.reshape(3, NAT_HEADS, NAT_QR * GRID_W, NAT_KROWS * GRID_W)


def _hgrn_consts(Tt, C, rev):
    nc = Tt // C
    t = np.arange(Tt)
    row, col = t[:, None], t[None, :]
    same = ((row // C) == (col // C)) & ((col >= row) if rev else (col <= row))
    at = [(nc - 1 - p) if rev else p for p in range(nc)]
    pair = np.concatenate([np.broadcast_to((t // C) == at[p - 1], (C, Tt)) for p in (1, 3)])
    masks = np.concatenate([np.tile(same, (HGRN_HEADS, 1)), np.tile(pair, (HGRN_HEADS, 1))]).astype(np.float32)
    hl = np.arange(HGRN_W) // HEAD_DIM
    bd = (hl[:, None] == hl[None, :]).astype(np.float32)
    return jnp.asarray(same.astype(np.float32), BF16), jnp.asarray(masks), jnp.asarray(bd)


def _hgrn_kernel(lf_f, k_f, v_f, q_f, lf_b, k_b, v_b, q_b, tri_f, mask_f, tri_b, mask_b, bd_ref,
                 of_ref, ob_ref, st_f, st_b, *, Tt, C):
    @pl.when(pl.program_id(1) == 0)
    def _():
        st_f[...] = jnp.zeros_like(st_f)
        st_b[...] = jnp.zeros_like(st_b)

    tiles = []
    for bi in range(HGRN_BB):
        tiles.append((of_ref, bi) + _hgrn_tile(lf_f.at[bi], k_f.at[bi], v_f.at[bi], q_f.at[bi], tri_f, mask_f, bd_ref,
                                               st_f.at[bi], rev=False, Tt=Tt, C=C))
        tiles.append((ob_ref, bi) + _hgrn_tile(lf_b.at[bi], k_b.at[bi], v_b.at[bi], q_b.at[bi], tri_b, mask_b, bd_ref,
                                               st_b.at[bi], rev=True, Tt=Tt, C=C))
    for o_ref, bi, o, _, _ in tiles:
        o_ref[bi] = o.astype(BF16)

    @pl.when(functools.reduce(jnp.minimum, [t[3] for t in tiles]) < -HGRN_SAFE)
    def _():
        for o_ref, bi, _, _, redo in tiles:
            o_ref[bi] = redo().astype(BF16)


def _hgrn_tile(lf_ref, k_ref, v_ref, q_ref, tri_ref, mask_ref, bd_ref, st_ref, *, rev, Tt, C):
    nc = Tt // C
    lf = lf_ref[...]
    lane_head = lax.broadcasted_iota(jnp.int32, (1, HGRN_W), 1) // HEAD_DIM
    head_sel = [jnp.where(lane_head == h, 1.0, 0.0).astype(BF16) for h in range(HGRN_HEADS)]

    def stack_heads(x16):
        return jnp.concatenate([x16 * m for m in head_sel], axis=0)

    tri = tri_ref[...]
    hi = lf.astype(BF16)
    lo = (lf - hi.astype(F32)).astype(BF16)
    bc = _dot(tri, hi) + _dot(tri, lo)

    cs = [slice(c * C, (c + 1) * C) for c in range(nc)]
    pos = [(nc - 1 - c) if rev else c for c in range(nc)]
    at = [pos.index(p) for p in range(nc)]
    last = 0 if rev else C - 1
    tot = [bc[c * C + last:c * C + last + 1] for c in range(nc)]
    zero = jnp.zeros((C, HGRN_W), F32)
    cat = lambda xs: jnp.concatenate(xs, axis=0).astype(BF16)

    def intra_fast():
        ka = (k * jnp.exp(-bc)).astype(BF16)
        return jnp.where(mask_ref[0:HGRN_HEADS * Tt, :] > 0.5, _dot_nt(stack_heads(cat(qa)), ka), 0.0)

    def intra_safe():
        q = q_ref[...].astype(F32)
        k = k_ref[...].astype(F32)
        t_idx = lax.broadcasted_iota(jnp.int32, (Tt, 1), 0)
        tau = (Tt - 1 - t_idx) if rev else t_idx
        row = lax.broadcasted_iota(jnp.int32, (HGRN_HEADS * Tt, Tt), 0) % Tt
        col = lax.broadcasted_iota(jnp.int32, (HGRN_HEADS * Tt, Tt), 1)
        acc = jnp.where(row == col, _dot_nt(stack_heads(q_ref[...]), k_ref[...]), 0.0)
        before = bc - lf
        end = bc
        size = 2
        while size <= C:
            half = size // 2
            upper = (tau % size) >= half
            w = jnp.exp(jnp.where(upper, bc - before, end - bc))
            qt = jnp.where(upper, q * w, 0.0).astype(BF16)
            kt = jnp.where(upper, 0.0, k * w).astype(BF16)
            acc = acc + jnp.where((row // size) == (col // size), _dot_nt(stack_heads(qt), kt), 0.0)
            if size < C:
                back, fwd = (Tt - half, half) if rev else (half, Tt - half)
                before = jnp.where(upper, pltpu.roll(before, back, 0), before)
                end = jnp.where(upper, end, pltpu.roll(end, fwd, 0))
            size *= 2
        return acc

    q = q_ref[...].astype(F32)
    k = k_ref[...].astype(F32)
    v = v_ref[...]
    qa = [q[cs[c]] * jnp.exp(bc[cs[c]]) for c in range(nc)]
    kb = [k[cs[c]] * jnp.exp(tot[c] - bc[cs[c]]) for c in range(nc)]
    q_pair = cat([qa[at[1]], qa[at[3]]])
    k_pair = cat([kb[c] if pos[c] % 2 == 0 else zero for c in range(nc)])
    a_pair = jnp.where(mask_ref[HGRN_HEADS * Tt:, :] > 0.5, _dot_nt(stack_heads(q_pair), k_pair), 0.0)
    q_half = cat([qa[at[2]], qa[at[3]] * jnp.exp(tot[at[2]])])
    k_half = cat([zero if pos[c] >= 2 else (kb[c] * jnp.exp(tot[at[1]]) if pos[c] == 0 else kb[c]) for c in range(nc)])
    a_half = _dot_nt(stack_heads(q_half), k_half)

    def add_cross(a_intra):
        blocks = []
        for h in range(HGRN_HEADS):
            for c in range(nc):
                blk = a_intra[h * Tt + c * C:h * Tt + (c + 1) * C]
                if pos[c] % 2 == 1:
                    r0 = h * 2 * C + (pos[c] // 2) * C
                    blk = blk + a_pair[r0:r0 + C]
                if pos[c] >= 2:
                    r0 = h * 2 * C + (pos[c] - 2) * C
                    blk = blk + a_half[r0:r0 + C]
                blocks.append(blk)
        return jnp.concatenate(blocks, axis=0)

    pre, suf = [None] * nc, [None] * nc
    run = jnp.zeros_like(tot[0])
    for p in range(nc):
        pre[at[p]] = run
        run = run + tot[at[p]]
    b_tile = run
    run = jnp.zeros_like(tot[0])
    for p in reversed(range(nc)):
        suf[at[p]] = run
        run = run + tot[at[p]]
    q_in = cat([qa[c] * jnp.exp(pre[c]) for c in range(nc)])
    k_out = cat([kb[c] * jnp.exp(suf[c]) for c in range(nc)])

    st = st_ref[...]
    o_inter = _dot_nt(q_in, st.astype(BF16))
    st_ref[...] = st * jnp.exp(b_tile) + jnp.where(bd_ref[...] > 0.5, _dot_tn(v, k_out), 0.0)

    def output(a_intra):
        o_heads = _dot(add_cross(a_intra).astype(BF16), v)
        o = o_inter
        for h in range(HGRN_HEADS):
            o = o + jnp.where(lane_head == h, o_heads[h * Tt:(h + 1) * Tt], 0.0)
        return o

    return output(intra_fast()), jnp.min(functools.reduce(jnp.minimum, tot)), lambda: output(intra_safe())


def _hgrn(lf, hk, hv, hq):
    Bt, T, _ = hv.shape
    Tt, C = TT_HGRN, HGRN_CHUNK
    bb = HGRN_BB
    assert Tt == 4 * C and Bt % bb == 0
    nt = T // Tt
    fwd = lambda col: pl.BlockSpec((bb, Tt, HGRN_W), lambda b, n: (b, n, col))
    bwd = lambda col: pl.BlockSpec((bb, Tt, HGRN_W), lambda b, n: (b, nt - 1 - n, col))
    tri_f, mask_f, bd = _hgrn_consts(Tt, C, False)
    tri_b, mask_b, _ = _hgrn_consts(Tt, C, True)
    tri_spec, mask_spec = _const_spec((Tt, Tt)), _const_spec((HGRN_HEADS * (Tt + 2 * C), Tt))
    out = jax.ShapeDtypeStruct((Bt, T, HGRN_W), BF16)
    return pl.pallas_call(
        functools.partial(_hgrn_kernel, Tt=Tt, C=C),
        grid=(Bt // bb, nt),
        in_specs=[fwd(0), fwd(0), fwd(0), fwd(0), bwd(1), bwd(1), bwd(0), bwd(0),
                  tri_spec, mask_spec, tri_spec, mask_spec, _const_spec((HGRN_W, HGRN_W))],
        out_specs=[fwd(0), bwd(0)],
        out_shape=[out, out],
        scratch_shapes=[pltpu.VMEM((bb, HGRN_W, HGRN_W), F32), pltpu.VMEM((bb, HGRN_W, HGRN_W), F32)],
        compiler_params=pltpu.CompilerParams(dimension_semantics=("parallel", "arbitrary"),
                                             vmem_limit_bytes=VMEM_LIMIT),
        name="hgrn",
    )(lf, hk, hv, hq, lf, hk, hv, hq, tri_f, mask_f, tri_b, mask_b, bd)


def _merge_kernel(x_ref, u_ref, ob_ref, hf_ref, hb_ref, hg_ref, hn_ref, grp_ref, od_ref, g_ref, pw_ref, ps_ref,
                  wbp_ref, wbg_ref, wbh_ref, wbn_ref, wo_ref, o_ref, pool_ref, *, tm, T):
    i = pl.program_id(1)
    t0 = pl.multiple_of(i * tm, tm)
    win = tm + 2 * POOL_HALO
    w0 = pl.multiple_of(jnp.clip(t0 - POOL_HALO, 0, T - win), POOL_HALO)
    pad = jnp.zeros((POOL_PAD, POOL_WIDTH), F32)
    run = jnp.concatenate([pad, u_ref[0, pl.ds(w0, win), :].astype(F32), pad], axis=0)
    rows = win + 2 * POOL_PAD
    group = lax.broadcasted_iota(jnp.int32, (1, POOL_WIDTH), 1) // POOL_GW
    sums, width = None, 1
    for g, w in enumerate(POOL_WINDOWS):
        while width < w:
            run = run + pltpu.roll(run, width, 0)
            width *= 2
        centred = run if w == 2 else pltpu.roll(run, rows - (w // 2 - 1), 0)
        sums = centred if sums is None else jnp.where(group == g, centred, sums)
    pool_ref[...] = sums
    half_w = functools.reduce(lambda acc, gw: jnp.where(group == gw[0], gw[1] // 2, acc),
                              list(enumerate(POOL_WINDOWS))[1:], POOL_WINDOWS[0] // 2)
    tg = t0 + lax.broadcasted_iota(jnp.int32, (tm, POOL_WIDTH), 0)
    cnt = (jnp.minimum(tg + half_w, T) - jnp.maximum(tg - half_w, 0)).astype(F32)
    uc = u_ref[0, pl.ds(t0, tm), :].astype(F32)
    mixed = pool_ref[pl.ds(pl.multiple_of(POOL_PAD + t0 - w0, 8), tm), :] / cnt - uc
    oa = _dot(mixed.astype(BF16), pw_ref[...]) * ps_ref[...]
    hsum = hf_ref[0].astype(F32) + hb_ref[0].astype(F32)
    msq = _dot((hsum * hsum).astype(BF16), grp_ref[...])
    oc = hsum * lax.rsqrt(msq + EPS) * hn_ref[...] * hg_ref[0].astype(F32)
    gate = lambda c: g_ref[0, :, c * D_MODEL:(c + 1) * D_MODEL].astype(F32)
    merged = (gate(0) * _dot(oa.astype(BF16), wbp_ref[...])
              + gate(1) * _dot(ob_ref[0], wbg_ref[...])
              + gate(2) * _dot(oc.astype(BF16), wbh_ref[...])
              + gate(3) * _dot(od_ref[0], wbn_ref[...]))
    o_ref[0] = x_ref[0] + _dot(merged.astype(BF16), wo_ref[...])


def _merge(x, u, ob, hf, hb, hg, hn, grp, od, gates, pw, ps, wbp, wbg, wbh, wbn, wo):
    Bt, T, _ = x.shape
    tm = TM_MERGE
    tok = lambda w: pl.BlockSpec((1, tm, w), lambda b, i: (b, i, 0))
    return pl.pallas_call(
        functools.partial(_merge_kernel, tm=tm, T=T),
        grid=(Bt, T // tm),
        in_specs=[
            tok(D_MODEL),
            pl.BlockSpec((1, T, POOL_WIDTH), lambda b, i: (b, 0, 0)),
            tok(GQA_QW), tok(HGRN_W), tok(HGRN_W), tok(HGRN_W), _const_spec((1, HGRN_W)),
            _const_spec((HGRN_W, HGRN_W)), tok(NAT_W), tok(N_BRANCH * D_MODEL),
            _const_spec((POOL_WIDTH, POOL_WIDTH)), _const_spec((1, POOL_WIDTH)),
            _const_spec((POOL_WIDTH, D_MODEL)), _const_spec((GQA_QW, D_MODEL)),
            _const_spec((HGRN_W, D_MODEL)), _const_spec((NAT_W, D_MODEL)),
            _const_spec((D_MODEL, D_MODEL)),
        ],
        out_specs=tok(D_MODEL),
        out_shape=jax.ShapeDtypeStruct((Bt, T, D_MODEL), F32),
        scratch_shapes=[pltpu.VMEM((tm + 2 * POOL_HALO + 2 * POOL_PAD, POOL_WIDTH), F32)],
        compiler_params=pltpu.CompilerParams(dimension_semantics=("parallel", "arbitrary"),
                                             vmem_limit_bytes=VMEM_LIMIT),
        name="merge",
    )(x, u, ob, hf, hb, hg, hn, grp, od, gates, pw, ps, wbp, wbg, wbh, wbn, wo)


def _mlp_kernel(x_ref, g_ref, wu_ref, wd_ref, o_ref):
    x = x_ref[...]
    ms = jnp.mean(x * x, axis=-1, keepdims=True)
    h = (x * lax.rsqrt(ms + EPS) * g_ref[...]).astype(BF16)
    acc = x
    for c in range(D_FF // D_MODEL):
        sl = slice(c * D_MODEL, (c + 1) * D_MODEL)
        hid = jnp.maximum(_dot(h, wu_ref[:, sl]), 0.0)
        acc = acc + _dot((hid * hid).astype(BF16), wd_ref[sl, :])
    o_ref[...] = acc


def _mlp(x2, g2, wu, wd):
    N = x2.shape[0]
    tm = TM_MLP
    return pl.pallas_call(
        _mlp_kernel,
        grid=(N // tm,),
        in_specs=[pl.BlockSpec((tm, D_MODEL), lambda i: (i, 0)), _const_spec((1, D_MODEL)),
                  _const_spec((D_MODEL, D_FF)), _const_spec((D_FF, D_MODEL))],
        out_specs=pl.BlockSpec((tm, D_MODEL), lambda i: (i, 0)),
        out_shape=jax.ShapeDtypeStruct((N, D_MODEL), F32),
        compiler_params=pltpu.CompilerParams(dimension_semantics=("parallel",), vmem_limit_bytes=VMEM_LIMIT),
        name="mlp",
    )(x2, g2, wu, wd)


def _block_diag(blocks):
    n, a, b = blocks.shape
    eye = jnp.eye(n, dtype=blocks.dtype)
    return (eye[:, None, :, None] * blocks[:, :, None, :]).reshape(n * a, n * b)


def _prepare(T, norm1_g, w_in, pool_w, pool_scale, gqa_qnorm, gqa_knorm, gqa_sink, hgrn_lb, hgrn_onorm,
             nat_qnorm, nat_knorm, nat_rpb, w_br_pool, w_br_gqa, w_br_hgrn, w_br_nat, w_o, norm2_g, w_up, w_down):
    D = D_MODEL
    half = HEAD_DIM // 2
    inv = ROPE_THETA ** (-jnp.arange(half, dtype=F32) / half)
    ang = jnp.arange(T, dtype=F32)[:, None] * inv[None, :]
    cos = jnp.tile(jnp.concatenate([jnp.cos(ang), jnp.cos(ang)], axis=1), (1, 2))
    sin = jnp.tile(jnp.concatenate([-jnp.sin(ang), jnp.sin(ang)], axis=1), (1, 2))
    sm = jax.nn.softmax(hgrn_lb.astype(F32), axis=0)
    lower = jnp.cumsum(sm, axis=0) - sm[:1]
    n_norm_heads = max(GQA_HEADS + GQA_KV_HEADS, 2 * NAT_HEADS, HGRN_HEADS)
    grp = _block_diag(jnp.full((n_norm_heads, HEAD_DIM, HEAD_DIM), 1.0 / HEAD_DIM, F32)).astype(BF16)
    scale = HEAD_DIM ** -0.5
    order = jnp.asarray(GQA_HEAD_ORDER)
    layers = []
    for l in range(w_in.shape[0]):
        w_l = w_in[l].astype(BF16)
        w_q = w_l[:, POOL_WIDTH:POOL_WIDTH + GQA_QW].reshape(D, GQA_HEADS, HEAD_DIM)[:, order, :].reshape(D, GQA_QW)
        layers.append(dict(
            g1=norm1_g[l][None, :],
            w_in=jnp.concatenate([w_l[:, :POOL_WIDTH], w_q, w_l[:, POOL_WIDTH + GQA_QW:]], axis=1),
            qk_gain=jnp.concatenate([jnp.tile(gqa_qnorm[l] * scale, GQA_HEADS),
                                     jnp.tile(gqa_knorm[l], GQA_KV_HEADS)])[None, :],
            n_gain=jnp.concatenate([jnp.tile(nat_qnorm[l] * scale, NAT_HEADS),
                                    jnp.tile(nat_knorm[l], NAT_HEADS)])[None, :],
            lower=lower[l].reshape(1, 2 * HGRN_W),
            sink=gqa_sink[l].astype(F32)[order],
            nat_bias=_nat_bias_table(nat_rpb[l], T // GRID_W),
            onorm=jnp.tile(hgrn_onorm[l], HGRN_HEADS)[None, :],
            pool_w=_block_diag(pool_w[l]).astype(BF16), pool_scale=pool_scale[l][None, :],
            w_bp=w_br_pool[l].astype(BF16),
            w_bg=w_br_gqa[l].astype(BF16).reshape(GQA_HEADS, HEAD_DIM, D)[order].reshape(GQA_QW, D),
            w_bh=w_br_hgrn[l].astype(BF16), w_bn=w_br_nat[l].astype(BF16), w_o=w_o[l].astype(BF16),
            g2=norm2_g[l][None, :], w_up=w_up[l].astype(BF16), w_down=w_down[l].astype(BF16)))
    return dict(cos=cos, sin=sin, grp=grp, layers=layers)


def _trunk(x, prep):
    Bt, T, D = x.shape
    N = Bt * T
    grp = prep["grp"]
    r3 = lambda a: a.reshape(Bt, T, a.shape[-1])
    for p in prep["layers"]:
        (up, gq, gk, gv, lf, hk, hv, hq, hg, nq, nk, nv, gates) = _inproj(
            x.reshape(N, D), p["g1"], p["w_in"], prep["cos"], prep["sin"], p["qk_gain"], p["n_gain"], p["lower"], grp, T)
        ob = _gqa(r3(gq), r3(gk), r3(gv), p["sink"])
        od = _nat(r3(nq), r3(nk), r3(nv), p["nat_bias"])
        h_f, h_b = _hgrn(r3(lf), r3(hk), r3(hv), r3(hq))
        x = _merge(x, r3(up), ob, h_f, h_b, r3(hg), p["onorm"], grp[:HGRN_W, :HGRN_W], od, r3(gates),
                   p["pool_w"], p["pool_scale"],
                   p["w_bp"], p["w_bg"], p["w_bh"], p["w_bn"], p["w_o"])
        x = _mlp(x.reshape(N, D), p["g2"], p["w_up"], p["w_down"]).reshape(Bt, T, D)
    return x


def kernel(x_prompt, x_sample, norm1_g, w_in, pool_w, pool_scale, gqa_qnorm, gqa_knorm, gqa_sink, hgrn_lb,
           hgrn_onorm, nat_qnorm, nat_knorm, nat_rpb, w_br_pool, w_br_gqa, w_br_hgrn, w_br_nat, w_o, norm2_g,
           w_up, w_down):
    assert x_prompt.shape[1] == x_sample.shape[1]
    prep = _prepare(x_prompt.shape[1], norm1_g, w_in, pool_w, pool_scale, gqa_qnorm, gqa_knorm, gqa_sink, hgrn_lb,
                    hgrn_onorm, nat_qnorm, nat_knorm, nat_rpb, w_br_pool, w_br_gqa, w_br_hgrn, w_br_nat, w_o,
                    norm2_g, w_up, w_down)
    return (_trunk(x_prompt, prep), _trunk(x_sample, prep))
```

```python
import functools

import numpy as np

import jax
import jax.numpy as jnp
from jax import lax
from jax.experimental import pallas as pl
from jax.experimental.pallas import tpu as pltpu

F32 = jnp.float32
BF16 = jnp.bfloat16

D_MODEL = 1024
GRID_W = 64
EPS = 1e-6
NEG = -1e30
TINY = 1e-30
HEAD_DIM = 64
ROPE_THETA = 10000.0
N_BRANCH = 4
D_FF = 4 * D_MODEL

POOL_WIDTH = 256
POOL_WINDOWS = (2, 4, 8, 16)
POOL_GW = 64
POOL_HALO = 16
POOL_PAD = 8

GQA_HEADS = 8
GQA_KV_HEADS = 2
GQA_GROUP = GQA_HEADS // GQA_KV_HEADS
GQA_WINDOW = 128
GQA_QW = GQA_HEADS * HEAD_DIM
GQA_KVW = GQA_KV_HEADS * HEAD_DIM

HGRN_HEADS = 4
HGRN_W = HGRN_HEADS * HEAD_DIM

NAT_HEADS = 4
NAT_KR = 8
NAT_KC = 16
NAT_W = NAT_HEADS * HEAD_DIM
NAT_QR = 4
NAT_KROWS = 12
NAT_SUB = 8
GQA_HEAD_ORDER = tuple((j % 2) * GQA_GROUP + j // 2 for j in range(GQA_HEADS))

_IN_GROUPS = (POOL_WIDTH, GQA_QW + GQA_KVW, GQA_KVW, 2 * HGRN_W, HGRN_W, HGRN_W, HGRN_W, 2 * NAT_W, NAT_W) \
    + (D_MODEL,) * N_BRANCH
_IN_EDGES = tuple(int(e) for e in np.cumsum((0,) + _IN_GROUPS))
IN_WIDTH = _IN_EDGES[-1]
IN_BLOCKS = tuple(tuple((a, b) for a, b in zip(_IN_EDGES[:-1], _IN_EDGES[1:]) if a // D_MODEL == blk)
                  for blk in range(IN_WIDTH // D_MODEL))

TM_IN = 512
TQ_GQA = 128
GQA_SUB = 8
TT_HGRN = 128
HGRN_BB = 4
HGRN_CHUNK = 32
HGRN_SAFE = 60.0
TM_MERGE = 1024
MERGE_SUB = 2
TM_MLP = 1024
V7X_VMEM_BYTES = 64 * 1024 * 1024
VMEM_LIMIT = V7X_VMEM_BYTES * 7 // 8


def _sigmoid(x):
    return 0.5 * jnp.tanh(0.5 * x) + 0.5


def _dot(a, b):
    return jnp.dot(a, b, preferred_element_type=F32)


def _dot_nt(a, b):
    return lax.dot_general(a, b, (((1,), (1,)), ((), ())), preferred_element_type=F32)


def _dot_tn(a, b):
    return lax.dot_general(a, b, (((0,), (0,)), ((), ())), preferred_element_type=F32)


def _const_spec(shape):
    n = len(shape)
    return pl.BlockSpec(shape, lambda *_: (0,) * n, pipeline_mode=pl.Buffered(1))


def _inproj_kernel(x_ref, g_ref, w_ref, cos_ref, sin_ref, qkg_ref, ng_ref, lb_ref, grp_ref,
                   up_ref, gq_ref, gk_ref, gv_ref, lf_ref, hk_ref, hv_ref, hq_ref, hg_ref,
                   nq_ref, nk_ref, nv_ref, gate_ref):
    x = x_ref[...]
    ms = jnp.mean(x * x, axis=-1, keepdims=True)
    h = (x * lax.rsqrt(ms + EPS) * g_ref[...]).astype(BF16)

    def headnorm(y, gain):
        w = y.shape[1]
        msq = _dot((y * y).astype(BF16), grp_ref[0:w, 0:w])
        return y * lax.rsqrt(msq + EPS) * gain

    def cols(block):
        z = _dot(h, w_ref[:, block * D_MODEL:(block + 1) * D_MODEL])
        return [z[:, a - block * D_MODEL:b - block * D_MODEL] for a, b in IN_BLOCKS[block]]

    zp, zqk, zv = cols(0)
    up_ref[...] = zp.astype(BF16)
    qk = headnorm(zqk, qkg_ref[...])
    qkw = GQA_QW + GQA_KVW
    reps = qkw // cos_ref.shape[1]
    cos = jnp.concatenate([cos_ref[...]] * reps, axis=1)
    sin = jnp.concatenate([sin_ref[...]] * reps, axis=1)
    first = (lax.broadcasted_iota(jnp.int32, (1, qkw), 1) % HEAD_DIM) < (HEAD_DIM // 2)
    partner = jnp.where(first, pltpu.roll(qk, qkw - HEAD_DIM // 2, 1), pltpu.roll(qk, HEAD_DIM // 2, 1))
    qk = qk * cos + partner * sin
    gq_ref[...] = qk[:, 0:GQA_QW].astype(BF16)
    gk_ref[...] = qk[:, GQA_QW:qkw].astype(BF16)
    gv_ref[...] = zv.astype(BF16)

    zf, zi, zq = cols(1)
    lb = lb_ref[...]
    f = lb + (1.0 - lb) / (1.0 + jnp.exp(-zf))
    lf_ref[...] = jnp.log(jnp.maximum(f, TINY))
    hk_ref[...] = (1.0 - f).astype(BF16)
    hv_ref[...] = zi.astype(BF16)
    hq_ref[...] = (zq * _sigmoid(zq)).astype(BF16)

    zg, znqk, znv = cols(2)
    hg_ref[...] = (zg * _sigmoid(zg)).astype(BF16)
    nqk = headnorm(znqk, ng_ref[...])
    nq_ref[...] = nqk[:, 0:NAT_W].astype(BF16)
    nk_ref[...] = nqk[:, NAT_W:2 * NAT_W].astype(BF16)
    nv_ref[...] = znv.astype(BF16)

    for c in range(N_BRANCH):
        (z,) = cols(3 + c)
        gate_ref[:, c * D_MODEL:(c + 1) * D_MODEL] = _sigmoid(z).astype(BF16)


def _inproj(x2, g1, w_in, cos, sin, qk_gain, n_gain, lb, grp, T):
    N = x2.shape[0]
    tm = TM_IN
    tpb = T // tm
    tok = lambda w: pl.BlockSpec((tm, w), lambda i: (i, 0))
    widths = [(POOL_WIDTH, BF16), (GQA_QW, BF16), (GQA_KVW, BF16), (GQA_KVW, BF16), (2 * HGRN_W, F32),
              (2 * HGRN_W, BF16), (HGRN_W, BF16), (HGRN_W, BF16), (HGRN_W, BF16), (NAT_W, BF16), (NAT_W, BF16),
              (NAT_W, BF16), (N_BRANCH * D_MODEL, BF16)]
    return pl.pallas_call(
        _inproj_kernel,
        grid=(N // tm,),
        in_specs=[
            tok(D_MODEL),
            _const_spec((1, D_MODEL)),
            _const_spec((D_MODEL, IN_WIDTH)),
            pl.BlockSpec((tm, cos.shape[1]), lambda i: (i % tpb, 0)),
            pl.BlockSpec((tm, sin.shape[1]), lambda i: (i % tpb, 0)),
            _const_spec((1, GQA_QW + GQA_KVW)),
            _const_spec((1, 2 * NAT_W)),
            _const_spec((1, 2 * HGRN_W)),
            _const_spec(grp.shape),
        ],
        out_specs=[tok(w) for w, _ in widths],
        out_shape=[jax.ShapeDtypeStruct((N, w), dt) for w, dt in widths],
        compiler_params=pltpu.CompilerParams(dimension_semantics=("parallel",), vmem_limit_bytes=VMEM_LIMIT),
        name="inproj",
    )(x2, g1, w_in, cos, sin, qk_gain, n_gain, lb, grp)


def _gqa_kernel(sink_ref, q_ref, k_ref, v_ref, o_ref, *, tq, T):
    for sub in range(GQA_SUB):
        t0 = (pl.program_id(1) * GQA_SUB + sub) * tq
        o_ref[0, sub * tq:(sub + 1) * tq, :] = _gqa_tile(sink_ref, q_ref[0, sub * tq:(sub + 1) * tq, :], k_ref, v_ref,
                                                        t0, tq, T)


def _gqa_tile(sink_ref, q, k_ref, v_ref, t0, tq, T):
    win = tq + 2 * GQA_WINDOW
    w0 = pl.multiple_of(jnp.clip(t0 - GQA_WINDOW, 0, T - win), GQA_WINDOW)
    k = k_ref[0, pl.ds(w0, win), :]
    v = v_ref[0, pl.ds(w0, win), :]
    qpos = t0 + lax.broadcasted_iota(jnp.int32, (tq, 1), 0)
    kpos = w0 + lax.broadcasted_iota(jnp.int32, (1, win), 1)
    valid = jnp.abs(kpos - qpos) <= GQA_WINDOW
    pw = 2 * HEAD_DIM
    lane = lax.broadcasted_iota(jnp.int32, (1, pw), 1)
    half = [jnp.where(lane < HEAD_DIM, 1.0, 0.0).astype(BF16), jnp.where(lane < HEAD_DIM, 0.0, 1.0).astype(BF16)]
    qs = jnp.concatenate([q[:, (j // 2) * pw:(j // 2 + 1) * pw] * half[j % 2] for j in range(GQA_HEADS)], axis=0)
    s_all = _dot_nt(qs, k)
    ps, invs = [], []
    for j in range(GQA_HEADS):
        s = jnp.where(valid, s_all[j * tq:(j + 1) * tq], NEG)
        sink = sink_ref[j]
        m = jnp.maximum(jnp.max(s, axis=-1, keepdims=True), sink)
        p = jnp.exp(s - m)
        invs.append(1.0 / (jnp.sum(p, axis=-1, keepdims=True) + jnp.exp(sink - m)))
        ps.append(p.astype(BF16))
    o_all = _dot(jnp.concatenate(ps, axis=0), v)
    o_heads = [o_all[j * tq:(j + 1) * tq] * invs[j] for j in range(GQA_HEADS)]
    outs = [jnp.where(lane < HEAD_DIM, o_heads[2 * pair], o_heads[2 * pair + 1]) for pair in range(GQA_HEADS // 2)]
    return jnp.concatenate(outs, axis=1).astype(BF16)


def _gqa(q, k, v, sink):
    Bt, T, _ = q.shape
    tq = TQ_GQA
    step = tq * GQA_SUB
    return pl.pallas_call(
        functools.partial(_gqa_kernel, tq=tq, T=T),
        grid=(Bt, T // step),
        in_specs=[
            pl.BlockSpec(memory_space=pltpu.SMEM),
            pl.BlockSpec((1, step, GQA_QW), lambda b, i: (b, i, 0)),
            pl.BlockSpec((1, T, GQA_KVW), lambda b, i: (b, 0, 0)),
            pl.BlockSpec((1, T, GQA_KVW), lambda b, i: (b, 0, 0)),
        ],
        out_specs=pl.BlockSpec((1, step, GQA_QW), lambda b, i: (b, i, 0)),
        out_shape=jax.ShapeDtypeStruct((Bt, T, GQA_QW), BF16),
        compiler_params=pltpu.CompilerParams(dimension_semantics=("parallel", "arbitrary"),
                                             vmem_limit_bytes=VMEM_LIMIT),
        name="gqa",
    )(sink, q, k, v)


def _nat_kernel(q_ref, k_ref, v_ref, bias_ref, o_ref, *, nblk):
    nq, nk = NAT_QR * GRID_W, NAT_KROWS * GRID_W
    lane_head = lax.broadcasted_iota(jnp.int32, (1, NAT_W), 1) // HEAD_DIM
    for sub in range(NAT_SUB):
        blk = pl.program_id(1) * NAT_SUB + sub
        first = jnp.clip(blk - 1, 0, nblk - NAT_KROWS // NAT_QR)
        variant = jnp.where(blk == 0, 0, jnp.where(blk == nblk - 1, 2, 1))
        off = pl.multiple_of(first * nq, nq)
        q = q_ref[0, sub * nq:(sub + 1) * nq, :]
        k = k_ref[0, pl.ds(off, nk), :]
        v = v_ref[0, pl.ds(off, nk), :]
        out = None
        for h in range(NAT_HEADS):
            s = _dot_nt(q * jnp.where(lane_head == h, 1.0, 0.0).astype(BF16), k) + bias_ref[variant, h]
            m = jnp.max(s, axis=-1, keepdims=True)
            p = jnp.exp(s - m)
            inv = 1.0 / jnp.sum(p, axis=-1, keepdims=True)
            o_h = _dot(p.astype(BF16), v) * inv
            out = o_h if out is None else jnp.where(lane_head == h, o_h, out)
        o_ref[0, sub * nq:(sub + 1) * nq, :] = out.astype(BF16)


def _nat(q, k, v, bias):
    Bt, T, _ = q.shape
    rows = T // GRID_W
    nblk = rows // NAT_QR
    assert rows % (NAT_QR * NAT_SUB) == 0 and rows >= NAT_KROWS
    nq = NAT_QR * GRID_W
    return pl.pallas_call(
        functools.partial(_nat_kernel, nblk=nblk),
        grid=(Bt, nblk // NAT_SUB),
        in_specs=[
            pl.BlockSpec((1, NAT_SUB * nq, NAT_W), lambda b, r: (b, r, 0)),
            pl.BlockSpec((1, T, NAT_W), lambda b, r: (b, 0, 0)),
            pl.BlockSpec((1, T, NAT_W), lambda b, r: (b, 0, 0)),
            _const_spec((3, NAT_HEADS, nq, NAT_KROWS * GRID_W)),
        ],
        out_specs=pl.BlockSpec((1, NAT_SUB * nq, NAT_W), lambda b, r: (b, r, 0)),
        out_shape=jax.ShapeDtypeStruct((Bt, T, NAT_W), BF16),
        compiler_params=pltpu.CompilerParams(dimension_semantics=("parallel", "arbitrary"),
                                             vmem_limit_bytes=VMEM_LIMIT),
        name="nat",
    )(q, k, v, bias)


def _nat_bias_table(rpb, rows):
    nblk = rows // NAT_QR
    c = np.arange(GRID_W)[:, None]
    kc = np.arange(GRID_W)[None, :]
    qc0 = np.clip(c - NAT_KC // 2, 0, GRID_W - NAT_KC)
    col_ok = (kc >= qc0) & (kc < qc0 + NAT_KC)
    dcol = np.clip(kc - c + (NAT_KC - 1), 0, 2 * NAT_KC - 2)
    oh_col = (dcol[None] == np.arange(2 * NAT_KC - 1)[:, None, None]) & col_ok[None]
    oh_row = np.zeros((3, NAT_QR, NAT_KROWS, 2 * NAT_KR - 1), np.float32)
    for var, blk in enumerate((0, 1, nblk - 1)):
        first = int(np.clip(blk - 1, 0, nblk - NAT_KROWS // NAT_QR))
        for ri in range(NAT_QR):
            r = blk * NAT_QR + ri
            r0 = int(np.clip(r - NAT_KR // 2, 0, rows - NAT_KR))
            for j in range(NAT_KROWS):
                kr = first * NAT_QR + j
                if r0 <= kr < r0 + NAT_KR:
                    oh_row[var, ri, j, kr - r + NAT_KR - 1] = 1.0
    hp = lax.Precision.HIGHEST
    by_col = jnp.einsum('hde,eck->hdck', rpb.astype(F32), jnp.asarray(oh_col, F32), precision=hp)
    bias = jnp.einsum('vrjd,hdck->vhrcjk', jnp.asarray(oh_row), by_col, precision=hp)
    ok = (oh_row.sum(-1) > 0)[:, None, :, None, :, None] & col_ok[None, None, None, :, None, :]
    bias = jnp.where(jnp.asarray(ok), bias, NEG)
    return bias.reshape(3, NAT_HEADS, NAT_QR * GRID_W, NAT_KROWS * GRID_W)


def _hgrn_consts(Tt, C, rev):
    nc = Tt // C
    t = np.arange(Tt)
    row, col = t[:, None], t[None, :]
    same = ((row // C) == (col // C)) & ((col >= row) if rev else (col <= row))
    at = [(nc - 1 - p) if rev else p for p in range(nc)]
    pair = np.concatenate([np.broadcast_to((t // C) == at[p - 1], (C, Tt)) for p in (1, 3)])
    masks = np.concatenate([np.tile(same, (HGRN_HEADS, 1)), np.tile(pair, (HGRN_HEADS, 1))]).astype(np.float32)
    hl = np.arange(HGRN_W) // HEAD_DIM
    bd = (hl[:, None] == hl[None, :]).astype(np.float32)
    return jnp.asarray(same.astype(np.float32), BF16), jnp.asarray(masks), jnp.asarray(bd)


def _hgrn_kernel(lf_f, k_f, v_f, q_f, lf_b, k_b, v_b, q_b, tri_f, mask_f, tri_b, mask_b, bd_ref,
                 of_ref, ob_ref, st_f, st_b, *, Tt, C):
    @pl.when(pl.program_id(1) == 0)
    def _():
        st_f[...] = jnp.zeros_like(st_f)
        st_b[...] = jnp.zeros_like(st_b)

    tiles = []
    for bi in range(HGRN_BB):
        tiles.append((of_ref, bi) + _hgrn_tile(lf_f.at[bi], k_f.at[bi], v_f.at[bi], q_f.at[bi], tri_f, mask_f, bd_ref,
                                               st_f.at[bi], rev=False, Tt=Tt, C=C))
        tiles.append((ob_ref, bi) + _hgrn_tile(lf_b.at[bi], k_b.at[bi], v_b.at[bi], q_b.at[bi], tri_b, mask_b, bd_ref,
                                               st_b.at[bi], rev=True, Tt=Tt, C=C))
    for o_ref, bi, o, _, _ in tiles:
        o_ref[bi] = o.astype(BF16)

    @pl.when(functools.reduce(jnp.minimum, [t[3] for t in tiles]) < -HGRN_SAFE)
    def _():
        for o_ref, bi, _, _, redo in tiles:
            o_ref[bi] = redo().astype(BF16)


def _hgrn_tile(lf_ref, k_ref, v_ref, q_ref, tri_ref, mask_ref, bd_ref, st_ref, *, rev, Tt, C):
    nc = Tt // C
    lf = lf_ref[...]
    lane_head = lax.broadcasted_iota(jnp.int32, (1, HGRN_W), 1) // HEAD_DIM
    head_sel = [jnp.where(lane_head == h, 1.0, 0.0).astype(BF16) for h in range(HGRN_HEADS)]

    def stack_heads(x16):
        return jnp.concatenate([x16 * m for m in head_sel], axis=0)

    tri = tri_ref[...]
    hi = lf.astype(BF16)
    lo = (lf - hi.astype(F32)).astype(BF16)
    bc = _dot(tri, hi) + _dot(tri, lo)

    cs = [slice(c * C, (c + 1) * C) for c in range(nc)]
    pos = [(nc - 1 - c) if rev else c for c in range(nc)]
    at = [pos.index(p) for p in range(nc)]
    last = 0 if rev else C - 1
    tot = [bc[c * C + last:c * C + last + 1] for c in range(nc)]
    zero = jnp.zeros((C, HGRN_W), F32)
    cat = lambda xs: jnp.concatenate(xs, axis=0).astype(BF16)

    def intra_fast():
        ka = (k * jnp.exp(-bc)).astype(BF16)
        return jnp.where(mask_ref[0:HGRN_HEADS * Tt, :] > 0.5, _dot_nt(stack_heads(cat(qa)), ka), 0.0)

    def intra_safe():
        q = q_ref[...].astype(F32)
        k = k_ref[...].astype(F32)
        t_idx = lax.broadcasted_iota(jnp.int32, (Tt, 1), 0)
        tau = (Tt - 1 - t_idx) if rev else t_idx
        row = lax.broadcasted_iota(jnp.int32, (HGRN_HEADS * Tt, Tt), 0) % Tt
        col = lax.broadcasted_iota(jnp.int32, (HGRN_HEADS * Tt, Tt), 1)
        acc = jnp.where(row == col, _dot_nt(stack_heads(q_ref[...]), k_ref[...]), 0.0)
        before = bc - lf
        end = bc
        size = 2
        while size <= C:
            half = size // 2
            upper = (tau % size) >= half
            w = jnp.exp(jnp.where(upper, bc - before, end - bc))
            qt = jnp.where(upper, q * w, 0.0).astype(BF16)
            kt = jnp.where(upper, 0.0, k * w).astype(BF16)
            acc = acc + jnp.where((row // size) == (col // size), _dot_nt(stack_heads(qt), kt), 0.0)
            if size < C:
                back, fwd = (Tt - half, half) if rev else (half, Tt - half)
                before = jnp.where(upper, pltpu.roll(before, back, 0), before)
                end = jnp.where(upper, end, pltpu.roll(end, fwd, 0))
            size *= 2
        return acc

    q = q_ref[...].astype(F32)
    k = k_ref[...].astype(F32)
    v = v_ref[...]
    qa = [q[cs[c]] * jnp.exp(bc[cs[c]]) for c in range(nc)]
    kb = [k[cs[c]] * jnp.exp(tot[c] - bc[cs[c]]) for c in range(nc)]
    q_pair = cat([qa[at[1]], qa[at[3]]])
    k_pair = cat([kb[c] if pos[c] % 2 == 0 else zero for c in range(nc)])
    a_pair = jnp.where(mask_ref[HGRN_HEADS * Tt:, :] > 0.5, _dot_nt(stack_heads(q_pair), k_pair), 0.0)
    q_half = cat([qa[at[2]], qa[at[3]] * jnp.exp(tot[at[2]])])
    k_half = cat([zero if pos[c] >= 2 else (kb[c] * jnp.exp(tot[at[1]]) if pos[c] == 0 else kb[c]) for c in range(nc)])
    a_half = _dot_nt(stack_heads(q_half), k_half)

    def add_cross(a_intra):
        blocks = []
        for h in range(HGRN_HEADS):
            for c in range(nc):
                blk = a_intra[h * Tt + c * C:h * Tt + (c + 1) * C]
                if pos[c] % 2 == 1:
                    r0 = h * 2 * C + (pos[c] // 2) * C
                    blk = blk + a_pair[r0:r0 + C]
                if pos[c] >= 2:
                    r0 = h * 2 * C + (pos[c] - 2) * C
                    blk = blk + a_half[r0:r0 + C]
                blocks.append(blk)
        return jnp.concatenate(blocks, axis=0)

    pre, suf = [None] * nc, [None] * nc
    run = jnp.zeros_like(tot[0])
    for p in range(nc):
        pre[at[p]] = run
        run = run + tot[at[p]]
    b_tile = run
    run = jnp.zeros_like(tot[0])
    for p in reversed(range(nc)):
        suf[at[p]] = run
        run = run + tot[at[p]]
    q_in = cat([qa[c] * jnp.exp(pre[c]) for c in range(nc)])
    k_out = cat([kb[c] * jnp.exp(suf[c]) for c in range(nc)])

    st = st_ref[...]
    o_inter = _dot_nt(q_in, st.astype(BF16))
    st_ref[...] = st * jnp.exp(b_tile) + jnp.where(bd_ref[...] > 0.5, _dot_tn(v, k_out), 0.0)

    def output(a_intra):
        o_heads = _dot(add_cross(a_intra).astype(BF16), v)
        o = o_inter
        for h in range(HGRN_HEADS):
            o = o + jnp.where(lane_head == h, o_heads[h * Tt:(h + 1) * Tt], 0.0)
        return o

    return output(intra_fast()), jnp.min(functools.reduce(jnp.minimum, tot)), lambda: output(intra_safe())


def _hgrn(lf, hk, hv, hq):
    Bt, T, _ = hv.shape
    Tt, C = TT_HGRN, HGRN_CHUNK
    bb = HGRN_BB
    assert Tt == 4 * C and Bt % bb == 0
    nt = T // Tt
    fwd = lambda col: pl.BlockSpec((bb, Tt, HGRN_W), lambda b, n: (b, n, col))
    bwd = lambda col: pl.BlockSpec((bb, Tt, HGRN_W), lambda b, n: (b, nt - 1 - n, col))
    tri_f, mask_f, bd = _hgrn_consts(Tt, C, False)
    tri_b, mask_b, _ = _hgrn_consts(Tt, C, True)
    tri_spec, mask_spec = _const_spec((Tt, Tt)), _const_spec((HGRN_HEADS * (Tt + 2 * C), Tt))
    out = jax.ShapeDtypeStruct((Bt, T, HGRN_W), BF16)
    return pl.pallas_call(
        functools.partial(_hgrn_kernel, Tt=Tt, C=C),
        grid=(Bt // bb, nt),
        in_specs=[fwd(0), fwd(0), fwd(0), fwd(0), bwd(1), bwd(1), bwd(0), bwd(0),
                  tri_spec, mask_spec, tri_spec, mask_spec, _const_spec((HGRN_W, HGRN_W))],
        out_specs=[fwd(0), bwd(0)],
        out_shape=[out, out],
        scratch_shapes=[pltpu.VMEM((bb, HGRN_W, HGRN_W), F32), pltpu.VMEM((bb, HGRN_W, HGRN_W), F32)],
        compiler_params=pltpu.CompilerParams(dimension_semantics=("parallel", "arbitrary"),
                                             vmem_limit_bytes=VMEM_LIMIT),
        name="hgrn",
    )(lf, hk, hv, hq, lf, hk, hv, hq, tri_f, mask_f, tri_b, mask_b, bd)


def _merge_kernel(x_ref, u_ref, ob_ref, hf_ref, hb_ref, hg_ref, hn_ref, grp_ref, od_ref, g_ref, pw_ref, ps_ref,
                  wbp_ref, wbg_ref, wbh_ref, wbn_ref, wo_ref, o_ref, pool_ref, *, tm, T):
    i = pl.program_id(1)
    t0 = pl.multiple_of(i * tm, tm)
    win = tm + 2 * POOL_HALO
    w0 = pl.multiple_of(jnp.clip(t0 - POOL_HALO, 0, T - win), POOL_HALO)
    pad = jnp.zeros((POOL_PAD, POOL_WIDTH), F32)
    run = jnp.concatenate([pad, u_ref[0, pl.ds(w0, win), :].astype(F32), pad], axis=0)
    rows = win + 2 * POOL_PAD
    group = lax.broadcasted_iota(jnp.int32, (1, POOL_WIDTH), 1) // POOL_GW
    sums, width = None, 1
    for g, w in enumerate(POOL_WINDOWS):
        while width < w:
            run = run + pltpu.roll(run, width, 0)
            width *= 2
        centred = run if w == 2 else pltpu.roll(run, rows - (w // 2 - 1), 0)
        sums = centred if sums is None else jnp.where(group == g, centred, sums)
    pool_ref[...] = sums
    half_w = functools.reduce(lambda acc, gw: jnp.where(group == gw[0], gw[1] // 2, acc),
                              list(enumerate(POOL_WINDOWS))[1:], POOL_WINDOWS[0] // 2)
    tg = t0 + lax.broadcasted_iota(jnp.int32, (tm, POOL_WIDTH), 0)
    cnt = (jnp.minimum(tg + half_w, T) - jnp.maximum(tg - half_w, 0)).astype(F32)
    uc = u_ref[0, pl.ds(t0, tm), :].astype(F32)
    mixed = pool_ref[pl.ds(pl.multiple_of(POOL_PAD + t0 - w0, 8), tm), :] / cnt - uc
    sub = tm // MERGE_SUB
    for r in range(MERGE_SUB):
        rs = slice(r * sub, (r + 1) * sub)
        oa = _dot(mixed[rs].astype(BF16), pw_ref[...]) * ps_ref[...]
        hsum = hf_ref[0, rs, :].astype(F32) + hb_ref[0, rs, :].astype(F32)
        msq = _dot((hsum * hsum).astype(BF16), grp_ref[...])
        oc = hsum * lax.rsqrt(msq + EPS) * hn_ref[...] * hg_ref[0, rs, :].astype(F32)
        gate = lambda c: g_ref[0, rs, c * D_MODEL:(c + 1) * D_MODEL].astype(F32)
        merged = (gate(0) * _dot(oa.astype(BF16), wbp_ref[...])
                  + gate(1) * _dot(ob_ref[0, rs, :], wbg_ref[...])
                  + gate(2) * _dot(oc.astype(BF16), wbh_ref[...])
                  + gate(3) * _dot(od_ref[0, rs, :], wbn_ref[...]))
        o_ref[0, rs, :] = x_ref[0, rs, :] + _dot(merged.astype(BF16), wo_ref[...])


def _merge(x, u, ob, hf, hb, hg, hn, grp, od, gates, pw, ps, wbp, wbg, wbh, wbn, wo):
    Bt, T, _ = x.shape
    tm = TM_MERGE
    tok = lambda w: pl.BlockSpec((1, tm, w), lambda b, i: (b, i, 0))
    return pl.pallas_call(
        functools.partial(_merge_kernel, tm=tm, T=T),
        grid=(Bt, T // tm),
        in_specs=[
            tok(D_MODEL),
            pl.BlockSpec((1, T, POOL_WIDTH), lambda b, i: (b, 0, 0)),
            tok(GQA_QW), tok(HGRN_W), tok(HGRN_W), tok(HGRN_W), _const_spec((1, HGRN_W)),
            _const_spec((HGRN_W, HGRN_W)), tok(NAT_W), tok(N_BRANCH * D_MODEL),
            _const_spec((POOL_WIDTH, POOL_WIDTH)), _const_spec((1, POOL_WIDTH)),
            _const_spec((POOL_WIDTH, D_MODEL)), _const_spec((GQA_QW, D_MODEL)),
            _const_spec((HGRN_W, D_MODEL)), _const_spec((NAT_W, D_MODEL)),
            _const_spec((D_MODEL, D_MODEL)),
        ],
        out_specs=tok(D_MODEL),
        out_shape=jax.ShapeDtypeStruct((Bt, T, D_MODEL), F32),
        scratch_shapes=[pltpu.VMEM((tm + 2 * POOL_HALO + 2 * POOL_PAD, POOL_WIDTH), F32)],
        compiler_params=pltpu.CompilerParams(dimension_semantics=("parallel", "arbitrary"),
                                             vmem_limit_bytes=VMEM_LIMIT),
        name="merge",
    )(x, u, ob, hf, hb, hg, hn, grp, od, gates, pw, ps, wbp, wbg, wbh, wbn, wo)


def _mlp_kernel(x_ref, g_ref, wu_ref, wd_ref, o_ref):
    x = x_ref[...]
    ms = jnp.mean(x * x, axis=-1, keepdims=True)
    h = (x * lax.rsqrt(ms + EPS) * g_ref[...]).astype(BF16)
    acc = x
    for c in range(D_FF // D_MODEL):
        sl = slice(c * D_MODEL, (c + 1) * D_MODEL)
        hid = jnp.maximum(_dot(h, wu_ref[:, sl]), 0.0)
        acc = acc + _dot((hid * hid).astype(BF16), wd_ref[sl, :])
    o_ref[...] = acc


def _mlp(x2, g2, wu, wd):
    N = x2.shape[0]
    tm = TM_MLP
    return pl.pallas_call(
        _mlp_kernel,
        grid=(N // tm,),
        in_specs=[pl.BlockSpec((tm, D_MODEL), lambda i: (i, 0)), _const_spec((1, D_MODEL)),
                  _const_spec((D_MODEL, D_FF)), _const_spec((D_FF, D_MODEL))],
        out_specs=pl.BlockSpec((tm, D_MODEL), lambda i: (i, 0)),
        out_shape=jax.ShapeDtypeStruct((N, D_MODEL), F32),
        compiler_params=pltpu.CompilerParams(dimension_semantics=("parallel",), vmem_limit_bytes=VMEM_LIMIT),
        name="mlp",
    )(x2, g2, wu, wd)


def _block_diag(blocks):
    n, a, b = blocks.shape
    eye = jnp.eye(n, dtype=blocks.dtype)
    return (eye[:, None, :, None] * blocks[:, :, None, :]).reshape(n * a, n * b)


def _prepare(T, norm1_g, w_in, pool_w, pool_scale, gqa_qnorm, gqa_knorm, gqa_sink, hgrn_lb, hgrn_onorm,
             nat_qnorm, nat_knorm, nat_rpb, w_br_pool, w_br_gqa, w_br_hgrn, w_br_nat, w_o, norm2_g, w_up, w_down):
    D = D_MODEL
    half = HEAD_DIM // 2
    inv = ROPE_THETA ** (-jnp.arange(half, dtype=F32) / half)
    ang = jnp.arange(T, dtype=F32)[:, None] * inv[None, :]
    cos = jnp.tile(jnp.concatenate([jnp.cos(ang), jnp.cos(ang)], axis=1), (1, 2))
    sin = jnp.tile(jnp.concatenate([-jnp.sin(ang), jnp.sin(ang)], axis=1), (1, 2))
    sm = jax.nn.softmax(hgrn_lb.astype(F32), axis=0)
    lower = jnp.cumsum(sm, axis=0) - sm[:1]
    n_norm_heads = max(GQA_HEADS + GQA_KV_HEADS, 2 * NAT_HEADS, HGRN_HEADS)
    grp = _block_diag(jnp.full((n_norm_heads, HEAD_DIM, HEAD_DIM), 1.0 / HEAD_DIM, F32)).astype(BF16)
    scale = HEAD_DIM ** -0.5
    order = jnp.asarray(GQA_HEAD_ORDER)
    layers = []
    for l in range(w_in.shape[0]):
        w_l = w_in[l].astype(BF16)
        w_q = w_l[:, POOL_WIDTH:POOL_WIDTH + GQA_QW].reshape(D, GQA_HEADS, HEAD_DIM)[:, order, :].reshape(D, GQA_QW)
        layers.append(dict(
            g1=norm1_g[l][None, :],
            w_in=jnp.concatenate([w_l[:, :POOL_WIDTH], w_q, w_l[:, POOL_WIDTH + GQA_QW:]], axis=1),
            qk_gain=jnp.concatenate([jnp.tile(gqa_qnorm[l] * scale, GQA_HEADS),
                                     jnp.tile(gqa_knorm[l], GQA_KV_HEADS)])[None, :],
            n_gain=jnp.concatenate([jnp.tile(nat_qnorm[l] * scale, NAT_HEADS),
                                    jnp.tile(nat_knorm[l], NAT_HEADS)])[None, :],
            lower=lower[l].reshape(1, 2 * HGRN_W),
            sink=gqa_sink[l].astype(F32)[order],
            nat_bias=_nat_bias_table(nat_rpb[l], T // GRID_W),
            onorm=jnp.tile(hgrn_onorm[l], HGRN_HEADS)[None, :],
            pool_w=_block_diag(pool_w[l]).astype(BF16), pool_scale=pool_scale[l][None, :],
            w_bp=w_br_pool[l].astype(BF16),
            w_bg=w_br_gqa[l].astype(BF16).reshape(GQA_HEADS, HEAD_DIM, D)[order].reshape(GQA_QW, D),
            w_bh=w_br_hgrn[l].astype(BF16), w_bn=w_br_nat[l].astype(BF16), w_o=w_o[l].astype(BF16),
            g2=norm2_g[l][None, :], w_up=w_up[l].astype(BF16), w_down=w_down[l].astype(BF16)))
    return dict(cos=cos, sin=sin, grp=grp, layers=layers)


def _trunk(x, prep):
    Bt, T, D = x.shape
    N = Bt * T
    grp = prep["grp"]
    r3 = lambda a: a.reshape(Bt, T, a.shape[-1])
    for p in prep["layers"]:
        (up, gq, gk, gv, lf, hk, hv, hq, hg, nq, nk, nv, gates) = _inproj(
            x.reshape(N, D), p["g1"], p["w_in"], prep["cos"], prep["sin"], p["qk_gain"], p["n_gain"], p["lower"], grp, T)
        ob = _gqa(r3(gq), r3(gk), r3(gv), p["sink"])
        od = _nat(r3(nq), r3(nk), r3(nv), p["nat_bias"])
        h_f, h_b = _hgrn(r3(lf), r3(hk), r3(hv), r3(hq))
        x = _merge(x, r3(up), ob, h_f, h_b, r3(hg), p["onorm"], grp[:HGRN_W, :HGRN_W], od, r3(gates),
                   p["pool_w"], p["pool_scale"],
                   p["w_bp"], p["w_bg"], p["w_bh"], p["w_bn"], p["w_o"])
        x = _mlp(x.reshape(N, D), p["g2"], p["w_up"], p["w_down"]).reshape(Bt, T, D)
    return x


def kernel(x_prompt, x_sample, norm1_g, w_in, pool_w, pool_scale, gqa_qnorm, gqa_knorm, gqa_sink, hgrn_lb,
           hgrn_onorm, nat_qnorm, nat_knorm, nat_rpb, w_br_pool, w_br_gqa, w_br_hgrn, w_br_nat, w_o, norm2_g,
           w_up, w_down):
    assert x_prompt.shape[1] == x_sample.shape[1]
    prep = _prepare(x_prompt.shape[1], norm1_g, w_in, pool_w, pool_scale, gqa_qnorm, gqa_knorm, gqa_sink, hgrn_lb,
                    hgrn_onorm, nat_qnorm, nat_knorm, nat_rpb, w_br_pool, w_br_gqa, w_br_hgrn, w_br_nat, w_o,
                    norm2_g, w_up, w_down)
    return (_trunk(x_prompt, prep), _trunk(x_sample, prep))
```

```python
import functools

import numpy as np

import jax
import jax.numpy as jnp
from jax import lax
from jax.experimental import pallas as pl
from jax.experimental.pallas import tpu as pltpu

F32 = jnp.float32
BF16 = jnp.bfloat16

D_MODEL = 1024
GRID_W = 64
EPS = 1e-6
NEG = -1e30
TINY = 1e-30
HEAD_DIM = 64
ROPE_THETA = 10000.0
LOG2E = 1.4426950408889634
N_BRANCH = 4
D_FF = 4 * D_MODEL

POOL_WIDTH = 256
POOL_WINDOWS = (2, 4, 8, 16)
POOL_GW = 64
POOL_HALO = 16
POOL_PAD = 8

GQA_HEADS = 8
GQA_KV_HEADS = 2
GQA_GROUP = GQA_HEADS // GQA_KV_HEADS
GQA_WINDOW = 128
GQA_QW = GQA_HEADS * HEAD_DIM
GQA_KVW = GQA_KV_HEADS * HEAD_DIM

HGRN_HEADS = 4
HGRN_W = HGRN_HEADS * HEAD_DIM

NAT_HEADS = 4
NAT_KR = 8
NAT_KC = 16
NAT_W = NAT_HEADS * HEAD_DIM
NAT_QR = 4
NAT_KROWS = 12
NAT_SUB = 8
GQA_HEAD_ORDER = tuple((j % 2) * GQA_GROUP + j // 2 for j in range(GQA_HEADS))

_IN_GROUPS = (POOL_WIDTH, GQA_QW + GQA_KVW, GQA_KVW, 2 * HGRN_W, HGRN_W, HGRN_W, HGRN_W, 2 * NAT_W, NAT_W) \
    + (D_MODEL,) * N_BRANCH
_IN_EDGES = tuple(int(e) for e in np.cumsum((0,) + _IN_GROUPS))
IN_WIDTH = _IN_EDGES[-1]
IN_BLOCKS = tuple(tuple((a, b) for a, b in zip(_IN_EDGES[:-1], _IN_EDGES[1:]) if a // D_MODEL == blk)
                  for blk in range(IN_WIDTH // D_MODEL))

TM_IN = 512
TQ_GQA = 128
GQA_SUB = 8
TT_HGRN = 128
HGRN_BB = 4
HGRN_CHUNK = 32
HGRN_SAFE = 86.0
TM_MERGE = 1024
MERGE_SUB = 2
TM_MLP = 1024
V7X_VMEM_BYTES = 64 * 1024 * 1024
VMEM_LIMIT = V7X_VMEM_BYTES * 7 // 8


def _sigmoid(x):
    return 0.5 * jnp.tanh(0.5 * x) + 0.5


def _dot(a, b):
    return jnp.dot(a, b, preferred_element_type=F32)


def _dot_nt(a, b):
    return lax.dot_general(a, b, (((1,), (1,)), ((), ())), preferred_element_type=F32)


def _dot_tn(a, b):
    return lax.dot_general(a, b, (((0,), (0,)), ((), ())), preferred_element_type=F32)


def _const_spec(shape):
    n = len(shape)
    return pl.BlockSpec(shape, lambda *_: (0,) * n, pipeline_mode=pl.Buffered(1))


def _inproj_kernel(x_ref, g_ref, w_ref, cos_ref, sin_ref, qkg_ref, ng_ref, lb_ref, grp_ref,
                   up_ref, gq_ref, gk_ref, gv_ref, lf_ref, hk_ref, hv_ref, hq_ref, hg_ref,
                   nq_ref, nk_ref, nv_ref, gate_ref):
    x = x_ref[...]
    ms = jnp.mean(x * x, axis=-1, keepdims=True)
    h = (x * lax.rsqrt(ms + EPS) * g_ref[...]).astype(BF16)

    def headnorm(y, gain):
        w = y.shape[1]
        msq = _dot((y * y).astype(BF16), grp_ref[0:w, 0:w])
        return y * lax.rsqrt(msq + EPS) * gain

    def cols(block):
        z = _dot(h, w_ref[:, block * D_MODEL:(block + 1) * D_MODEL])
        return [z[:, a - block * D_MODEL:b - block * D_MODEL] for a, b in IN_BLOCKS[block]]

    zp, zqk, zv = cols(0)
    up_ref[...] = zp.astype(BF16)
    qk = headnorm(zqk, qkg_ref[...])
    qkw = GQA_QW + GQA_KVW
    reps = qkw // cos_ref.shape[1]
    cos = jnp.concatenate([cos_ref[...]] * reps, axis=1)
    sin = jnp.concatenate([sin_ref[...]] * reps, axis=1)
    first = (lax.broadcasted_iota(jnp.int32, (1, qkw), 1) % HEAD_DIM) < (HEAD_DIM // 2)
    partner = jnp.where(first, pltpu.roll(qk, qkw - HEAD_DIM // 2, 1), pltpu.roll(qk, HEAD_DIM // 2, 1))
    qk = qk * cos + partner * sin
    gq_ref[...] = qk[:, 0:GQA_QW].astype(BF16)
    gk_ref[...] = qk[:, GQA_QW:qkw].astype(BF16)
    gv_ref[...] = zv.astype(BF16)

    zf, zi, zq = cols(1)
    lb = lb_ref[...]
    f = lb + (1.0 - lb) / (1.0 + jnp.exp(-zf))
    lf_ref[...] = jnp.log2(jnp.maximum(f, TINY))
    hk_ref[...] = (1.0 - f).astype(BF16)
    hv_ref[...] = zi.astype(BF16)
    hq_ref[...] = (zq * _sigmoid(zq)).astype(BF16)

    zg, znqk, znv = cols(2)
    hg_ref[...] = (zg * _sigmoid(zg)).astype(BF16)
    nqk = headnorm(znqk, ng_ref[...])
    nq_ref[...] = nqk[:, 0:NAT_W].astype(BF16)
    nk_ref[...] = nqk[:, NAT_W:2 * NAT_W].astype(BF16)
    nv_ref[...] = znv.astype(BF16)

    for c in range(N_BRANCH):
        (z,) = cols(3 + c)
        gate_ref[:, c * D_MODEL:(c + 1) * D_MODEL] = _sigmoid(z).astype(BF16)


def _inproj(x2, g1, w_in, cos, sin, qk_gain, n_gain, lb, grp, T):
    N = x2.shape[0]
    tm = TM_IN
    tpb = T // tm
    tok = lambda w: pl.BlockSpec((tm, w), lambda i: (i, 0))
    widths = [(POOL_WIDTH, BF16), (GQA_QW, BF16), (GQA_KVW, BF16), (GQA_KVW, BF16), (2 * HGRN_W, F32),
              (2 * HGRN_W, BF16), (HGRN_W, BF16), (HGRN_W, BF16), (HGRN_W, BF16), (NAT_W, BF16), (NAT_W, BF16),
              (NAT_W, BF16), (N_BRANCH * D_MODEL, BF16)]
    return pl.pallas_call(
        _inproj_kernel,
        grid=(N // tm,),
        in_specs=[
            tok(D_MODEL),
            _const_spec((1, D_MODEL)),
            _const_spec((D_MODEL, IN_WIDTH)),
            pl.BlockSpec((tm, cos.shape[1]), lambda i: (i % tpb, 0)),
            pl.BlockSpec((tm, sin.shape[1]), lambda i: (i % tpb, 0)),
            _const_spec((1, GQA_QW + GQA_KVW)),
            _const_spec((1, 2 * NAT_W)),
            _const_spec((1, 2 * HGRN_W)),
            _const_spec(grp.shape),
        ],
        out_specs=[tok(w) for w, _ in widths],
        out_shape=[jax.ShapeDtypeStruct((N, w), dt) for w, dt in widths],
        compiler_params=pltpu.CompilerParams(dimension_semantics=("parallel",), vmem_limit_bytes=VMEM_LIMIT),
        name="inproj",
    )(x2, g1, w_in, cos, sin, qk_gain, n_gain, lb, grp)


def _gqa_kernel(sink_ref, q_ref, k_ref, v_ref, o_ref, *, tq, T):
    for sub in range(GQA_SUB):
        t0 = (pl.program_id(1) * GQA_SUB + sub) * tq
        o_ref[0, sub * tq:(sub + 1) * tq, :] = _gqa_tile(sink_ref, q_ref[0, sub * tq:(sub + 1) * tq, :], k_ref, v_ref,
                                                        t0, tq, T)


def _gqa_tile(sink_ref, q, k_ref, v_ref, t0, tq, T):
    win = tq + 2 * GQA_WINDOW
    w0 = pl.multiple_of(jnp.clip(t0 - GQA_WINDOW, 0, T - win), GQA_WINDOW)
    k = k_ref[0, pl.ds(w0, win), :]
    v = v_ref[0, pl.ds(w0, win), :]
    qpos = t0 + lax.broadcasted_iota(jnp.int32, (tq, 1), 0)
    kpos = w0 + lax.broadcasted_iota(jnp.int32, (1, win), 1)
    valid = jnp.abs(kpos - qpos) <= GQA_WINDOW
    pw = 2 * HEAD_DIM
    lane = lax.broadcasted_iota(jnp.int32, (1, pw), 1)
    half = [jnp.where(lane < HEAD_DIM, 1.0, 0.0).astype(BF16), jnp.where(lane < HEAD_DIM, 0.0, 1.0).astype(BF16)]
    qs = jnp.concatenate([q[:, (j // 2) * pw:(j // 2 + 1) * pw] * half[j % 2] for j in range(GQA_HEADS)], axis=0)
    s_all = _dot_nt(qs, k)
    ps, invs = [], []
    for j in range(GQA_HEADS):
        s = jnp.where(valid, s_all[j * tq:(j + 1) * tq], NEG)
        sink = sink_ref[j]
        m = jnp.maximum(jnp.max(s, axis=-1, keepdims=True), sink)
        p = jnp.exp2(s - m)
        invs.append(1.0 / (jnp.sum(p, axis=-1, keepdims=True) + jnp.exp2(sink - m)))
        ps.append(p.astype(BF16))
    o_all = _dot(jnp.concatenate(ps, axis=0), v)
    o_heads = [o_all[j * tq:(j + 1) * tq] * invs[j] for j in range(GQA_HEADS)]
    outs = [jnp.where(lane < HEAD_DIM, o_heads[2 * pair], o_heads[2 * pair + 1]) for pair in range(GQA_HEADS // 2)]
    return jnp.concatenate(outs, axis=1).astype(BF16)


def _gqa(q, k, v, sink):
    Bt, T, _ = q.shape
    tq = TQ_GQA
    step = tq * GQA_SUB
    return pl.pallas_call(
        functools.partial(_gqa_kernel, tq=tq, T=T),
        grid=(Bt, T // step),
        in_specs=[
            pl.BlockSpec(memory_space=pltpu.SMEM),
            pl.BlockSpec((1, step, GQA_QW), lambda b, i: (b, i, 0)),
            pl.BlockSpec((1, T, GQA_KVW), lambda b, i: (b, 0, 0)),
            pl.BlockSpec((1, T, GQA_KVW), lambda b, i: (b, 0, 0)),
        ],
        out_specs=pl.BlockSpec((1, step, GQA_QW), lambda b, i: (b, i, 0)),
        out_shape=jax.ShapeDtypeStruct((Bt, T, GQA_QW), BF16),
        compiler_params=pltpu.CompilerParams(dimension_semantics=("parallel", "arbitrary"),
                                             vmem_limit_bytes=VMEM_LIMIT),
        name="gqa",
    )(sink, q, k, v)


def _nat_kernel(q_ref, k_ref, v_ref, bias_ref, o_ref, *, nblk):
    nq, nk = NAT_QR * GRID_W, NAT_KROWS * GRID_W
    lane_head = lax.broadcasted_iota(jnp.int32, (1, NAT_W), 1) // HEAD_DIM
    for sub in range(NAT_SUB):
        blk = pl.program_id(1) * NAT_SUB + sub
        first = jnp.clip(blk - 1, 0, nblk - NAT_KROWS // NAT_QR)
        variant = jnp.where(blk == 0, 0, jnp.where(blk == nblk - 1, 2, 1))
        off = pl.multiple_of(first * nq, nq)
        q = q_ref[0, sub * nq:(sub + 1) * nq, :]
        k = k_ref[0, pl.ds(off, nk), :]
        v = v_ref[0, pl.ds(off, nk), :]
        out = None
        for h in range(NAT_HEADS):
            s = _dot_nt(q * jnp.where(lane_head == h, 1.0, 0.0).astype(BF16), k) + bias_ref[variant, h]
            m = jnp.max(s, axis=-1, keepdims=True)
            p = jnp.exp2(s - m)
            inv = 1.0 / jnp.sum(p, axis=-1, keepdims=True)
            o_h = _dot(p.astype(BF16), v) * inv
            out = o_h if out is None else jnp.where(lane_head == h, o_h, out)
        o_ref[0, sub * nq:(sub + 1) * nq, :] = out.astype(BF16)


def _nat(q, k, v, bias):
    Bt, T, _ = q.shape
    rows = T // GRID_W
    nblk = rows // NAT_QR
    assert rows % (NAT_QR * NAT_SUB) == 0 and rows >= NAT_KROWS
    nq = NAT_QR * GRID_W
    return pl.pallas_call(
        functools.partial(_nat_kernel, nblk=nblk),
        grid=(Bt, nblk // NAT_SUB),
        in_specs=[
            pl.BlockSpec((1, NAT_SUB * nq, NAT_W), lambda b, r: (b, r, 0)),
            pl.BlockSpec((1, T, NAT_W), lambda b, r: (b, 0, 0)),
            pl.BlockSpec((1, T, NAT_W), lambda b, r: (b, 0, 0)),
            _const_spec((3, NAT_HEADS, nq, NAT_KROWS * GRID_W)),
        ],
        out_specs=pl.BlockSpec((1, NAT_SUB * nq, NAT_W), lambda b, r: (b, r, 0)),
        out_shape=jax.ShapeDtypeStruct((Bt, T, NAT_W), BF16),
        compiler_params=pltpu.CompilerParams(dimension_semantics=("parallel", "arbitrary"),
                                             vmem_limit_bytes=VMEM_LIMIT),
        name="nat",
    )(q, k, v, bias)


def _nat_bias_table(rpb, rows):
    nblk = rows // NAT_QR
    c = np.arange(GRID_W)[:, None]
    kc = np.arange(GRID_W)[None, :]
    qc0 = np.clip(c - NAT_KC // 2, 0, GRID_W - NAT_KC)
    col_ok = (kc >= qc0) & (kc < qc0 + NAT_KC)
    dcol = np.clip(kc - c + (NAT_KC - 1), 0, 2 * NAT_KC - 2)
    oh_col = (dcol[None] == np.arange(2 * NAT_KC - 1)[:, None, None]) & col_ok[None]
    oh_row = np.zeros((3, NAT_QR, NAT_KROWS, 2 * NAT_KR - 1), np.float32)
    for var, blk in enumerate((0, 1, nblk - 1)):
        first = int(np.clip(blk - 1, 0, nblk - NAT_KROWS // NAT_QR))
        for ri in range(NAT_QR):
            r = blk * NAT_QR + ri
            r0 = int(np.clip(r - NAT_KR // 2, 0, rows - NAT_KR))
            for j in range(NAT_KROWS):
                kr = first * NAT_QR + j
                if r0 <= kr < r0 + NAT_KR:
                    oh_row[var, ri, j, kr - r + NAT_KR - 1] = 1.0
    hp = lax.Precision.HIGHEST
    by_col = jnp.einsum('hde,eck->hdck', rpb.astype(F32), jnp.asarray(oh_col, F32), precision=hp)
    bias = jnp.einsum('vrjd,hdck->vhrcjk', jnp.asarray(oh_row), by_col, precision=hp)
    ok = (oh_row.sum(-1) > 0)[:, None, :, None, :, None] & col_ok[None, None, None, :, None, :]
    bias = jnp.where(jnp.asarray(ok), bias * LOG2E, NEG)
    return bias.reshape(3, NAT_HEADS, NAT_QR * GRID_W, NAT_KROWS * GRID_W)


def _hgrn_consts(Tt, C, rev):
    nc = Tt // C
    t = np.arange(Tt)
    row, col = t[:, None], t[None, :]
    same = ((row // C) == (col // C)) & ((col >= row) if rev else (col <= row))
    at = [(nc - 1 - p) if rev else p for p in range(nc)]
    pair = np.concatenate([np.broadcast_to((t // C) == at[p - 1], (C, Tt)) for p in (1, 3)])
    masks = np.concatenate([np.tile(same, (HGRN_HEADS, 1)), np.tile(pair, (HGRN_HEADS, 1))]).astype(np.float32)
    hl = np.arange(HGRN_W) // HEAD_DIM
    bd = (hl[:, None] == hl[None, :]).astype(np.float32)
    return jnp.asarray(same.astype(np.float32), BF16), jnp.asarray(masks), jnp.asarray(bd)


def _hgrn_kernel(lf_f, k_f, v_f, q_f, lf_b, k_b, v_b, q_b, tri_f, mask_f, tri_b, mask_b, bd_ref,
                 of_ref, ob_ref, st_f, st_b, *, Tt, C):
    @pl.when(pl.program_id(1) == 0)
    def _():
        st_f[...] = jnp.zeros_like(st_f)
        st_b[...] = jnp.zeros_like(st_b)

    tiles = []
    for bi in range(HGRN_BB):
        tiles.append((of_ref, bi) + _hgrn_tile(lf_f.at[bi], k_f.at[bi], v_f.at[bi], q_f.at[bi], tri_f, mask_f, bd_ref,
                                               st_f.at[bi], rev=False, Tt=Tt, C=C))
        tiles.append((ob_ref, bi) + _hgrn_tile(lf_b.at[bi], k_b.at[bi], v_b.at[bi], q_b.at[bi], tri_b, mask_b, bd_ref,
                                               st_b.at[bi], rev=True, Tt=Tt, C=C))
    for o_ref, bi, o, _, _ in tiles:
        o_ref[bi] = o.astype(BF16)

    @pl.when(functools.reduce(jnp.minimum, [t[3] for t in tiles]) < -HGRN_SAFE)
    def _():
        for o_ref, bi, _, _, redo in tiles:
            o_ref[bi] = redo().astype(BF16)


def _hgrn_tile(lf_ref, k_ref, v_ref, q_ref, tri_ref, mask_ref, bd_ref, st_ref, *, rev, Tt, C):
    nc = Tt // C
    lf = lf_ref[...]
    lane_head = lax.broadcasted_iota(jnp.int32, (1, HGRN_W), 1) // HEAD_DIM
    head_sel = [jnp.where(lane_head == h, 1.0, 0.0).astype(BF16) for h in range(HGRN_HEADS)]

    def stack_heads(x16):
        return jnp.concatenate([x16 * m for m in head_sel], axis=0)

    tri = tri_ref[...]
    hi = lf.astype(BF16)
    lo = (lf - hi.astype(F32)).astype(BF16)
    bc = _dot(tri, hi) + _dot(tri, lo)

    cs = [slice(c * C, (c + 1) * C) for c in range(nc)]
    pos = [(nc - 1 - c) if rev else c for c in range(nc)]
    at = [pos.index(p) for p in range(nc)]
    last = 0 if rev else C - 1
    tot = [bc[c * C + last:c * C + last + 1] for c in range(nc)]
    zero = jnp.zeros((C, HGRN_W), F32)
    cat = lambda xs: jnp.concatenate(xs, axis=0).astype(BF16)

    def intra_fast():
        ka = (k * jnp.exp2(-bc)).astype(BF16)
        return jnp.where(mask_ref[0:HGRN_HEADS * Tt, :] > 0.5, _dot_nt(stack_heads(cat(qa)), ka), 0.0)

    def intra_safe():
        q = q_ref[...].astype(F32)
        k = k_ref[...].astype(F32)
        t_idx = lax.broadcasted_iota(jnp.int32, (Tt, 1), 0)
        tau = (Tt - 1 - t_idx) if rev else t_idx
        row = lax.broadcasted_iota(jnp.int32, (HGRN_HEADS * Tt, Tt), 0) % Tt
        col = lax.broadcasted_iota(jnp.int32, (HGRN_HEADS * Tt, Tt), 1)
        acc = jnp.where(row == col, _dot_nt(stack_heads(q_ref[...]), k_ref[...]), 0.0)
        before = bc - lf
        end = bc
        size = 2
        while size <= C:
            half = size // 2
            upper = (tau % size) >= half
            w = jnp.exp2(jnp.where(upper, bc - before, end - bc))
            qt = jnp.where(upper, q * w, 0.0).astype(BF16)
            kt = jnp.where(upper, 0.0, k * w).astype(BF16)
            acc = acc + jnp.where((row // size) == (col // size), _dot_nt(stack_heads(qt), kt), 0.0)
            if size < C:
                back, fwd = (Tt - half, half) if rev else (half, Tt - half)
                before = jnp.where(upper, pltpu.roll(before, back, 0), before)
                end = jnp.where(upper, end, pltpu.roll(end, fwd, 0))
            size *= 2
        return acc

    q = q_ref[...].astype(F32)
    k = k_ref[...].astype(F32)
    v = v_ref[...]
    qa = [q[cs[c]] * jnp.exp2(bc[cs[c]]) for c in range(nc)]
    kb = [k[cs[c]] * jnp.exp2(tot[c] - bc[cs[c]]) for c in range(nc)]
    q_pair = cat([qa[at[1]], qa[at[3]]])
    k_pair = cat([kb[c] if pos[c] % 2 == 0 else zero for c in range(nc)])
    a_pair = _dot_nt(stack_heads(q_pair), k_pair) * mask_ref[HGRN_HEADS * Tt:, :]
    q_half = cat([qa[at[2]], qa[at[3]] * jnp.exp2(tot[at[2]])])
    k_half = cat([zero if pos[c] >= 2 else (kb[c] * jnp.exp2(tot[at[1]]) if pos[c] == 0 else kb[c]) for c in range(nc)])
    a_half = _dot_nt(stack_heads(q_half), k_half)

    def add_cross(a_intra):
        blocks = []
        for h in range(HGRN_HEADS):
            for c in range(nc):
                blk = a_intra[h * Tt + c * C:h * Tt + (c + 1) * C]
                if pos[c] % 2 == 1:
                    r0 = h * 2 * C + (pos[c] // 2) * C
                    blk = blk + a_pair[r0:r0 + C]
                if pos[c] >= 2:
                    r0 = h * 2 * C + (pos[c] - 2) * C
                    blk = blk + a_half[r0:r0 + C]
                blocks.append(blk)
        return jnp.concatenate(blocks, axis=0)

    pre, suf = [None] * nc, [None] * nc
    run = jnp.zeros_like(tot[0])
    for p in range(nc):
        pre[at[p]] = run
        run = run + tot[at[p]]
    b_tile = run
    run = jnp.zeros_like(tot[0])
    for p in reversed(range(nc)):
        suf[at[p]] = run
        run = run + tot[at[p]]
    q_in = cat([qa[c] * jnp.exp2(pre[c]) for c in range(nc)])
    k_out = cat([kb[c] * jnp.exp2(suf[c]) for c in range(nc)])

    st = st_ref[...]
    o_inter = _dot_nt(q_in, st.astype(BF16))
    st_ref[...] = st * jnp.exp2(b_tile) + _dot_tn(v, k_out) * bd_ref[...]

    def output(a_intra):
        o_heads = _dot(add_cross(a_intra).astype(BF16), v)
        o = o_inter
        for h in range(HGRN_HEADS):
            o = o + jnp.where(lane_head == h, o_heads[h * Tt:(h + 1) * Tt], 0.0)
        return o

    return output(intra_fast()), jnp.min(functools.reduce(jnp.minimum, tot)), lambda: output(intra_safe())


def _hgrn(lf, hk, hv, hq):
    Bt, T, _ = hv.shape
    Tt, C = TT_HGRN, HGRN_CHUNK
    bb = HGRN_BB
    assert Tt == 4 * C and Bt % bb == 0
    nt = T // Tt
    fwd = lambda col: pl.BlockSpec((bb, Tt, HGRN_W), lambda b, n: (b, n, col))
    bwd = lambda col: pl.BlockSpec((bb, Tt, HGRN_W), lambda b, n: (b, nt - 1 - n, col))
    tri_f, mask_f, bd = _hgrn_consts(Tt, C, False)
    tri_b, mask_b, _ = _hgrn_consts(Tt, C, True)
    tri_spec, mask_spec = _const_spec((Tt, Tt)), _const_spec((HGRN_HEADS * (Tt + 2 * C), Tt))
    out = jax.ShapeDtypeStruct((Bt, T, HGRN_W), BF16)
    return pl.pallas_call(
        functools.partial(_hgrn_kernel, Tt=Tt, C=C),
        grid=(Bt // bb, nt),
        in_specs=[fwd(0), fwd(0), fwd(0), fwd(0), bwd(1), bwd(1), bwd(0), bwd(0),
                  tri_spec, mask_spec, tri_spec, mask_spec, _const_spec((HGRN_W, HGRN_W))],
        out_specs=[fwd(0), bwd(0)],
        out_shape=[out, out],
        scratch_shapes=[pltpu.VMEM((bb, HGRN_W, HGRN_W), F32), pltpu.VMEM((bb, HGRN_W, HGRN_W), F32)],
        compiler_params=pltpu.CompilerParams(dimension_semantics=("parallel", "arbitrary"),
                                             vmem_limit_bytes=VMEM_LIMIT),
        name="hgrn",
    )(lf, hk, hv, hq, lf, hk, hv, hq, tri_f, mask_f, tri_b, mask_b, bd)


def _merge_kernel(x_ref, u_ref, ob_ref, hf_ref, hb_ref, hg_ref, hn_ref, grp_ref, od_ref, g_ref, pw_ref, ps_ref,
                  wbp_ref, wbg_ref, wbh_ref, wbn_ref, wo_ref, o_ref, pool_ref, *, tm, T):
    i = pl.program_id(1)
    t0 = pl.multiple_of(i * tm, tm)
    win = tm + 2 * POOL_HALO
    w0 = pl.multiple_of(jnp.clip(t0 - POOL_HALO, 0, T - win), POOL_HALO)
    pad = jnp.zeros((POOL_PAD, POOL_WIDTH), F32)
    run = jnp.concatenate([pad, u_ref[0, pl.ds(w0, win), :].astype(F32), pad], axis=0)
    rows = win + 2 * POOL_PAD
    group = lax.broadcasted_iota(jnp.int32, (1, POOL_WIDTH), 1) // POOL_GW
    sums, width = None, 1
    for g, w in enumerate(POOL_WINDOWS):
        while width < w:
            run = run + pltpu.roll(run, width, 0)
            width *= 2
        centred = run if w == 2 else pltpu.roll(run, rows - (w // 2 - 1), 0)
        sums = centred if sums is None else jnp.where(group == g, centred, sums)
    pool_ref[...] = sums
    half_w = functools.reduce(lambda acc, gw: jnp.where(group == gw[0], gw[1] // 2, acc),
                              list(enumerate(POOL_WINDOWS))[1:], POOL_WINDOWS[0] // 2)
    tg = t0 + lax.broadcasted_iota(jnp.int32, (tm, POOL_WIDTH), 0)
    cnt = (jnp.minimum(tg + half_w, T) - jnp.maximum(tg - half_w, 0)).astype(F32)
    uc = u_ref[0, pl.ds(t0, tm), :].astype(F32)
    mixed = pool_ref[pl.ds(pl.multiple_of(POOL_PAD + t0 - w0, 8), tm), :] / cnt - uc
    sub = tm // MERGE_SUB
    for r in range(MERGE_SUB):
        rs = slice(r * sub, (r + 1) * sub)
        oa = _dot(mixed[rs].astype(BF16), pw_ref[...]) * ps_ref[...]
        hsum = hf_ref[0, rs, :].astype(F32) + hb_ref[0, rs, :].astype(F32)
        msq = _dot((hsum * hsum).astype(BF16), grp_ref[...])
        oc = hsum * lax.rsqrt(msq + EPS) * hn_ref[...] * hg_ref[0, rs, :].astype(F32)
        gate = lambda c: g_ref[0, rs, c * D_MODEL:(c + 1) * D_MODEL].astype(F32)
        merged = (gate(0) * _dot(oa.astype(BF16), wbp_ref[...])
                  + gate(1) * _dot(ob_ref[0, rs, :], wbg_ref[...])
                  + gate(2) * _dot(oc.astype(BF16), wbh_ref[...])
                  + gate(3) * _dot(od_ref[0, rs, :], wbn_ref[...]))
        o_ref[0, rs, :] = x_ref[0, rs, :] + _dot(merged.astype(BF16), wo_ref[...])


def _merge(x, u, ob, hf, hb, hg, hn, grp, od, gates, pw, ps, wbp, wbg, wbh, wbn, wo):
    Bt, T, _ = x.shape
    tm = TM_MERGE
    tok = lambda w: pl.BlockSpec((1, tm, w), lambda b, i: (b, i, 0))
    return pl.pallas_call(
        functools.partial(_merge_kernel, tm=tm, T=T),
        grid=(Bt, T // tm),
        in_specs=[
            tok(D_MODEL),
            pl.BlockSpec((1, T, POOL_WIDTH), lambda b, i: (b, 0, 0)),
            tok(GQA_QW), tok(HGRN_W), tok(HGRN_W), tok(HGRN_W), _const_spec((1, HGRN_W)),
            _const_spec((HGRN_W, HGRN_W)), tok(NAT_W), tok(N_BRANCH * D_MODEL),
            _const_spec((POOL_WIDTH, POOL_WIDTH)), _const_spec((1, POOL_WIDTH)),
            _const_spec((POOL_WIDTH, D_MODEL)), _const_spec((GQA_QW, D_MODEL)),
            _const_spec((HGRN_W, D_MODEL)), _const_spec((NAT_W, D_MODEL)),
            _const_spec((D_MODEL, D_MODEL)),
        ],
        out_specs=tok(D_MODEL),
        out_shape=jax.ShapeDtypeStruct((Bt, T, D_MODEL), F32),
        scratch_shapes=[pltpu.VMEM((tm + 2 * POOL_HALO + 2 * POOL_PAD, POOL_WIDTH), F32)],
        compiler_params=pltpu.CompilerParams(dimension_semantics=("parallel", "arbitrary"),
                                             vmem_limit_bytes=VMEM_LIMIT),
        name="merge",
    )(x, u, ob, hf, hb, hg, hn, grp, od, gates, pw, ps, wbp, wbg, wbh, wbn, wo)


def _mlp_kernel(x_ref, g_ref, wu_ref, wd_ref, o_ref):
    x = x_ref[...]
    ms = jnp.mean(x * x, axis=-1, keepdims=True)
    h = (x * lax.rsqrt(ms + EPS) * g_ref[...]).astype(BF16)
    acc = x
    for c in range(D_FF // D_MODEL):
        sl = slice(c * D_MODEL, (c + 1) * D_MODEL)
        hid = jnp.maximum(_dot(h, wu_ref[:, sl]), 0.0)
        acc = acc + _dot((hid * hid).astype(BF16), wd_ref[sl, :])
    o_ref[...] = acc


def _mlp(x2, g2, wu, wd):
    N = x2.shape[0]
    tm = TM_MLP
    return pl.pallas_call(
        _mlp_kernel,
        grid=(N // tm,),
        in_specs=[pl.BlockSpec((tm, D_MODEL), lambda i: (i, 0)), _const_spec((1, D_MODEL)),
                  _const_spec((D_MODEL, D_FF)), _const_spec((D_FF, D_MODEL))],
        out_specs=pl.BlockSpec((tm, D_MODEL), lambda i: (i, 0)),
        out_shape=jax.ShapeDtypeStruct((N, D_MODEL), F32),
        compiler_params=pltpu.CompilerParams(dimension_semantics=("parallel",), vmem_limit_bytes=VMEM_LIMIT),
        name="mlp",
    )(x2, g2, wu, wd)


def _block_diag(blocks):
    n, a, b = blocks.shape
    eye = jnp.eye(n, dtype=blocks.dtype)
    return (eye[:, None, :, None] * blocks[:, :, None, :]).reshape(n * a, n * b)


def _prepare(T, norm1_g, w_in, pool_w, pool_scale, gqa_qnorm, gqa_knorm, gqa_sink, hgrn_lb, hgrn_onorm,
             nat_qnorm, nat_knorm, nat_rpb, w_br_pool, w_br_gqa, w_br_hgrn, w_br_nat, w_o, norm2_g, w_up, w_down):
    D = D_MODEL
    half = HEAD_DIM // 2
    inv = ROPE_THETA ** (-jnp.arange(half, dtype=F32) / half)
    ang = jnp.arange(T, dtype=F32)[:, None] * inv[None, :]
    cos = jnp.tile(jnp.concatenate([jnp.cos(ang), jnp.cos(ang)], axis=1), (1, 2))
    sin = jnp.tile(jnp.concatenate([-jnp.sin(ang), jnp.sin(ang)], axis=1), (1, 2))
    sm = jax.nn.softmax(hgrn_lb.astype(F32), axis=0)
    lower = jnp.cumsum(sm, axis=0) - sm[:1]
    n_norm_heads = max(GQA_HEADS + GQA_KV_HEADS, 2 * NAT_HEADS, HGRN_HEADS)
    grp = _block_diag(jnp.full((n_norm_heads, HEAD_DIM, HEAD_DIM), 1.0 / HEAD_DIM, F32)).astype(BF16)
    scale = HEAD_DIM ** -0.5 * LOG2E
    order = jnp.asarray(GQA_HEAD_ORDER)
    layers = []
    for l in range(w_in.shape[0]):
        w_l = w_in[l].astype(BF16)
        w_q = w_l[:, POOL_WIDTH:POOL_WIDTH + GQA_QW].reshape(D, GQA_HEADS, HEAD_DIM)[:, order, :].reshape(D, GQA_QW)
        layers.append(dict(
            g1=norm1_g[l][None, :],
            w_in=jnp.concatenate([w_l[:, :POOL_WIDTH], w_q, w_l[:, POOL_WIDTH + GQA_QW:]], axis=1),
            qk_gain=jnp.concatenate([jnp.tile(gqa_qnorm[l] * scale, GQA_HEADS),
                                     jnp.tile(gqa_knorm[l], GQA_KV_HEADS)])[None, :],
            n_gain=jnp.concatenate([jnp.tile(nat_qnorm[l] * scale, NAT_HEADS),
                                    jnp.tile(nat_knorm[l], NAT_HEADS)])[None, :],
            lower=lower[l].reshape(1, 2 * HGRN_W),
            sink=gqa_sink[l].astype(F32)[order] * LOG2E,
            nat_bias=_nat_bias_table(nat_rpb[l], T // GRID_W),
            onorm=jnp.tile(hgrn_onorm[l], HGRN_HEADS)[None, :],
            pool_w=_block_diag(pool_w[l]).astype(BF16), pool_scale=pool_scale[l][None, :],
            w_bp=w_br_pool[l].astype(BF16),
            w_bg=w_br_gqa[l].astype(BF16).reshape(GQA_HEADS, HEAD_DIM, D)[order].reshape(GQA_QW, D),
            w_bh=w_br_hgrn[l].astype(BF16), w_bn=w_br_nat[l].astype(BF16), w_o=w_o[l].astype(BF16),
            g2=norm2_g[l][None, :], w_up=w_up[l].astype(BF16), w_down=w_down[l].astype(BF16)))
    return dict(cos=cos, sin=sin, grp=grp, layers=layers)


def _trunk(x, prep):
    Bt, T, D = x.shape
    N = Bt * T
    grp = prep["grp"]
    r3 = lambda a: a.reshape(Bt, T, a.shape[-1])
    for p in prep["layers"]:
        (up, gq, gk, gv, lf, hk, hv, hq, hg, nq, nk, nv, gates) = _inproj(
            x.reshape(N, D), p["g1"], p["w_in"], prep["cos"], prep["sin"], p["qk_gain"], p["n_gain"], p["lower"], grp, T)
        ob = _gqa(r3(gq), r3(gk), r3(gv), p["sink"])
        od = _nat(r3(nq), r3(nk), r3(nv), p["nat_bias"])
        h_f, h_b = _hgrn(r3(lf), r3(hk), r3(hv), r3(hq))
        x = _merge(x, r3(up), ob, h_f, h_b, r3(hg), p["onorm"], grp[:HGRN_W, :HGRN_W], od, r3(gates),
                   p["pool_w"], p["pool_scale"],
                   p["w_bp"], p["w_bg"], p["w_bh"], p["w_bn"], p["w_o"])
        x = _mlp(x.reshape(N, D), p["g2"], p["w_up"], p["w_down"]).reshape(Bt, T, D)
    return x


def kernel(x_prompt, x_sample, norm1_g, w_in, pool_w, pool_scale, gqa_qnorm, gqa_knorm, gqa_sink, hgrn_lb,
           hgrn_onorm, nat_qnorm, nat_knorm, nat_rpb, w_br_pool, w_br_gqa, w_br_hgrn, w_br_nat, w_o, norm2_g,
           w_up, w_down):
    assert x_prompt.shape[1] == x_sample.shape[1]
    prep = _prepare(x_prompt.shape[1], norm1_g, w_in, pool_w, pool_scale, gqa_qnorm, gqa_knorm, gqa_sink, hgrn_lb,
                    hgrn_onorm, nat_qnorm, nat_knorm, nat_rpb, w_br_pool, w_br_gqa, w_br_hgrn, w_br_nat, w_o,
                    norm2_g, w_up, w_down)
    return (_trunk(x_prompt, prep), _trunk(x_sample, prep))
```

```python
import functools

import numpy as np

import jax
import jax.numpy as jnp
from jax import lax
from jax.experimental import pallas as pl
from jax.experimental.pallas import tpu as pltpu

F32 = jnp.float32
BF16 = jnp.bfloat16

D_MODEL = 1024
GRID_W = 64
EPS = 1e-6
NEG = -1e30
TINY = 1e-30
HEAD_DIM = 64
ROPE_THETA = 10000.0
LOG2E = 1.4426950408889634
N_BRANCH = 4
D_FF = 4 * D_MODEL

POOL_WIDTH = 256
POOL_WINDOWS = (2, 4, 8, 16)
POOL_GW = 64
POOL_HALO = 16
POOL_PAD = 8

GQA_HEADS = 8
GQA_KV_HEADS = 2
GQA_GROUP = GQA_HEADS // GQA_KV_HEADS
GQA_WINDOW = 128
GQA_QW = GQA_HEADS * HEAD_DIM
GQA_KVW = GQA_KV_HEADS * HEAD_DIM

HGRN_HEADS = 4
HGRN_W = HGRN_HEADS * HEAD_DIM

NAT_HEADS = 4
NAT_KR = 8
NAT_KC = 16
NAT_W = NAT_HEADS * HEAD_DIM
NAT_QR = 4
NAT_KROWS = 12
NAT_SUB = 8
GQA_HEAD_ORDER = tuple((j % 2) * GQA_GROUP + j // 2 for j in range(GQA_HEADS))

_IN_GROUPS = (POOL_WIDTH, GQA_QW + GQA_KVW, GQA_KVW, 2 * HGRN_W, HGRN_W, HGRN_W, HGRN_W, 2 * NAT_W, NAT_W) \
    + (D_MODEL,) * N_BRANCH
_IN_EDGES = tuple(int(e) for e in np.cumsum((0,) + _IN_GROUPS))
IN_WIDTH = _IN_EDGES[-1]
IN_BLOCKS = tuple(tuple((a, b) for a, b in zip(_IN_EDGES[:-1], _IN_EDGES[1:]) if a // D_MODEL == blk)
                  for blk in range(IN_WIDTH // D_MODEL))

TM_IN = 512
TQ_GQA = 128
GQA_SUB = 8
TT_HGRN = 128
HGRN_BB = 4
HGRN_CHUNK = 32
HGRN_SAFE = 86.0
TM_MERGE = 1024
MERGE_SUB = 2
TM_MLP = 1024
V7X_VMEM_BYTES = 64 * 1024 * 1024
VMEM_LIMIT = V7X_VMEM_BYTES * 7 // 8


def _sigmoid(x):
    return 0.5 * jnp.tanh(0.5 * x) + 0.5


def _dot(a, b):
    return jnp.dot(a, b, preferred_element_type=F32)


def _dot_nt(a, b):
    return lax.dot_general(a, b, (((1,), (1,)), ((), ())), preferred_element_type=F32)


def _dot_tn(a, b):
    return lax.dot_general(a, b, (((0,), (0,)), ((), ())), preferred_element_type=F32)


def _const_spec(shape):
    n = len(shape)
    return pl.BlockSpec(shape, lambda *_: (0,) * n, pipeline_mode=pl.Buffered(1))


def _inproj_kernel(x_ref, g_ref, w_ref, cos_ref, sin_ref, qkg_ref, ng_ref, lb_ref, grp_ref,
                   up_ref, gq_ref, gk_ref, gv_ref, lf_ref, hk_ref, hv_ref, hq_ref, hg_ref,
                   nq_ref, nk_ref, nv_ref, gate_ref):
    x = x_ref[...]
    ms = jnp.mean(x * x, axis=-1, keepdims=True)
    h = (x * lax.rsqrt(ms + EPS) * g_ref[...]).astype(BF16)

    def headnorm(y, gain):
        w = y.shape[1]
        msq = _dot((y * y).astype(BF16), grp_ref[0:w, 0:w])
        return y * lax.rsqrt(msq + EPS) * gain

    def cols(block):
        z = _dot(h, w_ref[:, block * D_MODEL:(block + 1) * D_MODEL])
        return [z[:, a - block * D_MODEL:b - block * D_MODEL] for a, b in IN_BLOCKS[block]]

    zp, zqk, zv = cols(0)
    up_ref[...] = zp.astype(BF16)
    qk = headnorm(zqk, qkg_ref[...])
    qkw = GQA_QW + GQA_KVW
    reps = qkw // cos_ref.shape[1]
    cos = jnp.concatenate([cos_ref[...]] * reps, axis=1)
    sin = jnp.concatenate([sin_ref[...]] * reps, axis=1)
    first = (lax.broadcasted_iota(jnp.int32, (1, qkw), 1) % HEAD_DIM) < (HEAD_DIM // 2)
    partner = jnp.where(first, pltpu.roll(qk, qkw - HEAD_DIM // 2, 1), pltpu.roll(qk, HEAD_DIM // 2, 1))
    qk = qk * cos + partner * sin
    gq_ref[...] = qk[:, 0:GQA_QW].astype(BF16)
    gk_ref[...] = qk[:, GQA_QW:qkw].astype(BF16)
    gv_ref[...] = zv.astype(BF16)

    zf, zi, zq = cols(1)
    lb = lb_ref[...]
    f = lb + (1.0 - lb) / (1.0 + jnp.exp(-zf))
    lf_ref[...] = jnp.log2(jnp.maximum(f, TINY))
    hk_ref[...] = (1.0 - f).astype(BF16)
    hv_ref[...] = zi.astype(BF16)
    hq_ref[...] = (zq * _sigmoid(zq)).astype(BF16)

    zg, znqk, znv = cols(2)
    hg_ref[...] = (zg * _sigmoid(zg)).astype(BF16)
    nqk = headnorm(znqk, ng_ref[...])
    nq_ref[...] = nqk[:, 0:NAT_W].astype(BF16)
    nk_ref[...] = nqk[:, NAT_W:2 * NAT_W].astype(BF16)
    nv_ref[...] = znv.astype(BF16)

    for c in range(N_BRANCH):
        (z,) = cols(3 + c)
        gate_ref[:, c * D_MODEL:(c + 1) * D_MODEL] = _sigmoid(z).astype(BF16)


def _inproj(x2, g1, w_in, cos, sin, qk_gain, n_gain, lb, grp, T):
    N = x2.shape[0]
    tm = TM_IN
    tpb = T // tm
    tok = lambda w: pl.BlockSpec((tm, w), lambda i: (i, 0))
    widths = [(POOL_WIDTH, BF16), (GQA_QW, BF16), (GQA_KVW, BF16), (GQA_KVW, BF16), (2 * HGRN_W, F32),
              (2 * HGRN_W, BF16), (HGRN_W, BF16), (HGRN_W, BF16), (HGRN_W, BF16), (NAT_W, BF16), (NAT_W, BF16),
              (NAT_W, BF16), (N_BRANCH * D_MODEL, BF16)]
    return pl.pallas_call(
        _inproj_kernel,
        grid=(N // tm,),
        in_specs=[
            tok(D_MODEL),
            _const_spec((1, D_MODEL)),
            _const_spec((D_MODEL, IN_WIDTH)),
            pl.BlockSpec((tm, cos.shape[1]), lambda i: (i % tpb, 0)),
            pl.BlockSpec((tm, sin.shape[1]), lambda i: (i % tpb, 0)),
            _const_spec((1, GQA_QW + GQA_KVW)),
            _const_spec((1, 2 * NAT_W)),
            _const_spec((1, 2 * HGRN_W)),
            _const_spec(grp.shape),
        ],
        out_specs=[tok(w) for w, _ in widths],
        out_shape=[jax.ShapeDtypeStruct((N, w), dt) for w, dt in widths],
        compiler_params=pltpu.CompilerParams(dimension_semantics=("parallel",), vmem_limit_bytes=VMEM_LIMIT),
        name="inproj",
    )(x2, g1, w_in, cos, sin, qk_gain, n_gain, lb, grp)


def _gqa_kernel(sink_ref, q_ref, k_ref, v_ref, o_ref, *, tq, T):
    for sub in range(GQA_SUB):
        t0 = (pl.program_id(1) * GQA_SUB + sub) * tq
        o_ref[0, sub * tq:(sub + 1) * tq, :] = _gqa_tile(sink_ref, q_ref[0, sub * tq:(sub + 1) * tq, :], k_ref, v_ref,
                                                        t0, tq, T)


def _gqa_tile(sink_ref, q, k_ref, v_ref, t0, tq, T):
    win = tq + 2 * GQA_WINDOW
    w0 = pl.multiple_of(jnp.clip(t0 - GQA_WINDOW, 0, T - win), GQA_WINDOW)
    k = k_ref[0, pl.ds(w0, win), :]
    v = v_ref[0, pl.ds(w0, win), :]
    qpos = t0 + lax.broadcasted_iota(jnp.int32, (tq, 1), 0)
    kpos = w0 + lax.broadcasted_iota(jnp.int32, (1, win), 1)
    valid = jnp.abs(kpos - qpos) <= GQA_WINDOW
    pw = 2 * HEAD_DIM
    lane = lax.broadcasted_iota(jnp.int32, (1, pw), 1)
    half = [jnp.where(lane < HEAD_DIM, 1.0, 0.0).astype(BF16), jnp.where(lane < HEAD_DIM, 0.0, 1.0).astype(BF16)]
    qs = jnp.concatenate([q[:, (j // 2) * pw:(j // 2 + 1) * pw] * half[j % 2] for j in range(GQA_HEADS)], axis=0)
    s_all = _dot_nt(qs, k)
    ps, invs = [], []
    for j in range(GQA_HEADS):
        s = jnp.where(valid, s_all[j * tq:(j + 1) * tq], NEG)
        sink = sink_ref[j]
        m = jnp.maximum(jnp.max(s, axis=-1, keepdims=True), sink)
        p = jnp.exp2(s - m)
        invs.append(1.0 / (jnp.sum(p, axis=-1, keepdims=True) + jnp.exp2(sink - m)))
        ps.append(p.astype(BF16))
    o_all = _dot(jnp.concatenate(ps, axis=0), v)
    o_heads = [o_all[j * tq:(j + 1) * tq] * invs[j] for j in range(GQA_HEADS)]
    outs = [jnp.where(lane < HEAD_DIM, o_heads[2 * pair], o_heads[2 * pair + 1]) for pair in range(GQA_HEADS // 2)]
    return jnp.concatenate(outs, axis=1).astype(BF16)


def _gqa(q, k, v, sink):
    Bt, T, _ = q.shape
    tq = TQ_GQA
    step = tq * GQA_SUB
    return pl.pallas_call(
        functools.partial(_gqa_kernel, tq=tq, T=T),
        grid=(Bt, T // step),
        in_specs=[
            pl.BlockSpec(memory_space=pltpu.SMEM),
            pl.BlockSpec((1, step, GQA_QW), lambda b, i: (b, i, 0)),
            pl.BlockSpec((1, T, GQA_KVW), lambda b, i: (b, 0, 0)),
            pl.BlockSpec((1, T, GQA_KVW), lambda b, i: (b, 0, 0)),
        ],
        out_specs=pl.BlockSpec((1, step, GQA_QW), lambda b, i: (b, i, 0)),
        out_shape=jax.ShapeDtypeStruct((Bt, T, GQA_QW), BF16),
        compiler_params=pltpu.CompilerParams(dimension_semantics=("parallel", "arbitrary"),
                                             vmem_limit_bytes=VMEM_LIMIT),
        name="gqa",
    )(sink, q, k, v)


def _nat_kernel(q_ref, k_ref, v_ref, bias_ref, o_ref, *, nblk):
    nq, nk = NAT_QR * GRID_W, NAT_KROWS * GRID_W
    lane_head = lax.broadcasted_iota(jnp.int32, (1, NAT_W), 1) // HEAD_DIM
    for sub in range(NAT_SUB):
        blk = pl.program_id(1) * NAT_SUB + sub
        first = jnp.clip(blk - 1, 0, nblk - NAT_KROWS // NAT_QR)
        variant = jnp.where(blk == 0, 0, jnp.where(blk == nblk - 1, 2, 1))
        off = pl.multiple_of(first * nq, nq)
        q = q_ref[0, sub * nq:(sub + 1) * nq, :]
        k = k_ref[0, pl.ds(off, nk), :]
        v = v_ref[0, pl.ds(off, nk), :]
        out = None
        for h in range(NAT_HEADS):
            s = _dot_nt(q * jnp.where(lane_head == h, 1.0, 0.0).astype(BF16), k) + bias_ref[variant, h]
            m = jnp.max(s, axis=-1, keepdims=True)
            p = jnp.exp2(s - m)
            inv = 1.0 / jnp.sum(p, axis=-1, keepdims=True)
            o_h = _dot(p.astype(BF16), v) * inv
            out = o_h if out is None else jnp.where(lane_head == h, o_h, out)
        o_ref[0, sub * nq:(sub + 1) * nq, :] = out.astype(BF16)


def _nat(q, k, v, bias):
    Bt, T, _ = q.shape
    rows = T // GRID_W
    nblk = rows // NAT_QR
    assert rows % (NAT_QR * NAT_SUB) == 0 and rows >= NAT_KROWS
    nq = NAT_QR * GRID_W
    return pl.pallas_call(
        functools.partial(_nat_kernel, nblk=nblk),
        grid=(Bt, nblk // NAT_SUB),
        in_specs=[
            pl.BlockSpec((1, NAT_SUB * nq, NAT_W), lambda b, r: (b, r, 0)),
            pl.BlockSpec((1, T, NAT_W), lambda b, r: (b, 0, 0)),
            pl.BlockSpec((1, T, NAT_W), lambda b, r: (b, 0, 0)),
            _const_spec((3, NAT_HEADS, nq, NAT_KROWS * GRID_W)),
        ],
        out_specs=pl.BlockSpec((1, NAT_SUB * nq, NAT_W), lambda b, r: (b, r, 0)),
        out_shape=jax.ShapeDtypeStruct((Bt, T, NAT_W), BF16),
        compiler_params=pltpu.CompilerParams(dimension_semantics=("parallel", "arbitrary"),
                                             vmem_limit_bytes=VMEM_LIMIT),
        name="nat",
    )(q, k, v, bias)


def _nat_bias_table(rpb, rows):
    nblk = rows // NAT_QR
    c = np.arange(GRID_W)[:, None]
    kc = np.arange(GRID_W)[None, :]
    qc0 = np.clip(c - NAT_KC // 2, 0, GRID_W - NAT_KC)
    col_ok = (kc >= qc0) & (kc < qc0 + NAT_KC)
    dcol = np.clip(kc - c + (NAT_KC - 1), 0, 2 * NAT_KC - 2)
    oh_col = (dcol[None] == np.arange(2 * NAT_KC - 1)[:, None, None]) & col_ok[None]
    oh_row = np.zeros((3, NAT_QR, NAT_KROWS, 2 * NAT_KR - 1), np.float32)
    for var, blk in enumerate((0, 1, nblk - 1)):
        first = int(np.clip(blk - 1, 0, nblk - NAT_KROWS // NAT_QR))
        for ri in range(NAT_QR):
            r = blk * NAT_QR + ri
            r0 = int(np.clip(r - NAT_KR // 2, 0, rows - NAT_KR))
            for j in range(NAT_KROWS):
                kr = first * NAT_QR + j
                if r0 <= kr < r0 + NAT_KR:
                    oh_row[var, ri, j, kr - r + NAT_KR - 1] = 1.0
    oh_row = np.concatenate([oh_row, 1.0 - oh_row.sum(-1, keepdims=True)], axis=-1)
    hp = lax.Precision.HIGHEST
    by_col = jnp.einsum('hde,eck->hdck', rpb.astype(F32), jnp.asarray(oh_col, F32), precision=hp)
    by_col = jnp.where(jnp.asarray(col_ok)[None, None], by_col * LOG2E, NEG)
    by_col = jnp.concatenate([by_col, jnp.full_like(by_col[:, :1], NEG)], axis=1)
    bias = jnp.einsum('vrjd,hdck->vhrcjk', jnp.asarray(oh_row), by_col, precision=hp)
    return bias.reshape(3, NAT_HEADS, NAT_QR * GRID_W, NAT_KROWS * GRID_W)


def _hgrn_consts(Tt, C, rev):
    nc = Tt // C
    t = np.arange(Tt)
    row, col = t[:, None], t[None, :]
    same = ((row // C) == (col // C)) & ((col >= row) if rev else (col <= row))
    at = [(nc - 1 - p) if rev else p for p in range(nc)]
    pair = np.concatenate([np.broadcast_to((t // C) == at[p - 1], (C, Tt)) for p in (1, 3)])
    masks = np.concatenate([np.tile(same, (HGRN_HEADS, 1)), np.tile(pair, (HGRN_HEADS, 1))]).astype(np.float32)
    hl = np.arange(HGRN_W) // HEAD_DIM
    bd = (hl[:, None] == hl[None, :]).astype(np.float32)
    return jnp.asarray(same.astype(np.float32), BF16), jnp.asarray(masks), jnp.asarray(bd)


def _hgrn_kernel(lf_f, k_f, v_f, q_f, lf_b, k_b, v_b, q_b, tri_f, mask_f, tri_b, mask_b, bd_ref,
                 of_ref, ob_ref, st_f, st_b, *, Tt, C):
    @pl.when(pl.program_id(1) == 0)
    def _():
        st_f[...] = jnp.zeros_like(st_f)
        st_b[...] = jnp.zeros_like(st_b)

    tiles = []
    for bi in range(HGRN_BB):
        tiles.append((of_ref, bi) + _hgrn_tile(lf_f.at[bi], k_f.at[bi], v_f.at[bi], q_f.at[bi], tri_f, mask_f, bd_ref,
                                               st_f.at[bi], rev=False, Tt=Tt, C=C))
        tiles.append((ob_ref, bi) + _hgrn_tile(lf_b.at[bi], k_b.at[bi], v_b.at[bi], q_b.at[bi], tri_b, mask_b, bd_ref,
                                               st_b.at[bi], rev=True, Tt=Tt, C=C))
    for o_ref, bi, o, _, _ in tiles:
        o_ref[bi] = o.astype(BF16)

    @pl.when(functools.reduce(jnp.minimum, [t[3] for t in tiles]) < -HGRN_SAFE)
    def _():
        for o_ref, bi, _, _, redo in tiles:
            o_ref[bi] = redo().astype(BF16)


def _hgrn_tile(lf_ref, k_ref, v_ref, q_ref, tri_ref, mask_ref, bd_ref, st_ref, *, rev, Tt, C):
    nc = Tt // C
    lf = lf_ref[...]
    lane_head = lax.broadcasted_iota(jnp.int32, (1, HGRN_W), 1) // HEAD_DIM
    head_sel = [jnp.where(lane_head == h, 1.0, 0.0).astype(BF16) for h in range(HGRN_HEADS)]

    def stack_heads(x16):
        return jnp.concatenate([x16 * m for m in head_sel], axis=0)

    tri = tri_ref[...]
    hi = lf.astype(BF16)
    lo = (lf - hi.astype(F32)).astype(BF16)
    bc = _dot(tri, hi) + _dot(tri, lo)

    cs = [slice(c * C, (c + 1) * C) for c in range(nc)]
    pos = [(nc - 1 - c) if rev else c for c in range(nc)]
    at = [pos.index(p) for p in range(nc)]
    last = 0 if rev else C - 1
    tot = [bc[c * C + last:c * C + last + 1] for c in range(nc)]
    zero = jnp.zeros((C, HGRN_W), F32)
    cat = lambda xs: jnp.concatenate(xs, axis=0).astype(BF16)

    def intra_fast():
        ka = (k * jnp.exp2(-bc)).astype(BF16)
        return jnp.where(mask_ref[0:HGRN_HEADS * Tt, :] > 0.5, _dot_nt(stack_heads(cat(qa)), ka), 0.0)

    def intra_safe():
        q = q_ref[...].astype(F32)
        k = k_ref[...].astype(F32)
        t_idx = lax.broadcasted_iota(jnp.int32, (Tt, 1), 0)
        tau = (Tt - 1 - t_idx) if rev else t_idx
        row = lax.broadcasted_iota(jnp.int32, (HGRN_HEADS * Tt, Tt), 0) % Tt
        col = lax.broadcasted_iota(jnp.int32, (HGRN_HEADS * Tt, Tt), 1)
        acc = jnp.where(row == col, _dot_nt(stack_heads(q_ref[...]), k_ref[...]), 0.0)
        before = bc - lf
        end = bc
        size = 2
        while size <= C:
            half = size // 2
            upper = (tau % size) >= half
            w = jnp.exp2(jnp.where(upper, bc - before, end - bc))
            qt = jnp.where(upper, q * w, 0.0).astype(BF16)
            kt = jnp.where(upper, 0.0, k * w).astype(BF16)
            acc = acc + jnp.where((row // size) == (col // size), _dot_nt(stack_heads(qt), kt), 0.0)
            if size < C:
                back, fwd = (Tt - half, half) if rev else (half, Tt - half)
                before = jnp.where(upper, pltpu.roll(before, back, 0), before)
                end = jnp.where(upper, end, pltpu.roll(end, fwd, 0))
            size *= 2
        return acc

    q = q_ref[...].astype(F32)
    k = k_ref[...].astype(F32)
    v = v_ref[...]
    qa = [q[cs[c]] * jnp.exp2(bc[cs[c]]) for c in range(nc)]
    kb = [k[cs[c]] * jnp.exp2(tot[c] - bc[cs[c]]) for c in range(nc)]
    q_pair = cat([qa[at[1]], qa[at[3]]])
    k_pair = cat([kb[c] if pos[c] % 2 == 0 else zero for c in range(nc)])
    a_pair = _dot_nt(stack_heads(q_pair), k_pair) * mask_ref[HGRN_HEADS * Tt:, :]
    q_half = cat([qa[at[2]], qa[at[3]] * jnp.exp2(tot[at[2]])])
    k_half = cat([zero if pos[c] >= 2 else (kb[c] * jnp.exp2(tot[at[1]]) if pos[c] == 0 else kb[c]) for c in range(nc)])
    a_half = _dot_nt(stack_heads(q_half), k_half)

    def add_cross(a_intra):
        blocks = []
        for h in range(HGRN_HEADS):
            for c in range(nc):
                blk = a_intra[h * Tt + c * C:h * Tt + (c + 1) * C]
                if pos[c] % 2 == 1:
                    r0 = h * 2 * C + (pos[c] // 2) * C
                    blk = blk + a_pair[r0:r0 + C]
                if pos[c] >= 2:
                    r0 = h * 2 * C + (pos[c] - 2) * C
                    blk = blk + a_half[r0:r0 + C]
                blocks.append(blk)
        return jnp.concatenate(blocks, axis=0)

    pre, suf = [None] * nc, [None] * nc
    run = jnp.zeros_like(tot[0])
    for p in range(nc):
        pre[at[p]] = run
        run = run + tot[at[p]]
    b_tile = run
    run = jnp.zeros_like(tot[0])
    for p in reversed(range(nc)):
        suf[at[p]] = run
        run = run + tot[at[p]]
    q_in = cat([qa[c] * jnp.exp2(pre[c]) for c in range(nc)])
    k_out = cat([kb[c] * jnp.exp2(suf[c]) for c in range(nc)])

    st = st_ref[...]
    o_inter = _dot_nt(q_in, st.astype(BF16))
    st_ref[...] = st * jnp.exp2(b_tile) + _dot_tn(v, k_out) * bd_ref[...]

    def output(a_intra):
        o_heads = _dot(add_cross(a_intra).astype(BF16), v)
        o = o_inter
        for h in range(HGRN_HEADS):
            o = o + jnp.where(lane_head == h, o_heads[h * Tt:(h + 1) * Tt], 0.0)
        return o

    return output(intra_fast()), jnp.min(functools.reduce(jnp.minimum, tot)), lambda: output(intra_safe())


def _hgrn(lf, hk, hv, hq):
    Bt, T, _ = hv.shape
    Tt, C = TT_HGRN, HGRN_CHUNK
    bb = HGRN_BB
    assert Tt == 4 * C and Bt % bb == 0
    nt = T // Tt
    fwd = lambda col: pl.BlockSpec((bb, Tt, HGRN_W), lambda b, n: (b, n, col))
    bwd = lambda col: pl.BlockSpec((bb, Tt, HGRN_W), lambda b, n: (b, nt - 1 - n, col))
    tri_f, mask_f, bd = _hgrn_consts(Tt, C, False)
    tri_b, mask_b, _ = _hgrn_consts(Tt, C, True)
    tri_spec, mask_spec = _const_spec((Tt, Tt)), _const_spec((HGRN_HEADS * (Tt + 2 * C), Tt))
    out = jax.ShapeDtypeStruct((Bt, T, HGRN_W), BF16)
    return pl.pallas_call(
        functools.partial(_hgrn_kernel, Tt=Tt, C=C),
        grid=(Bt // bb, nt),
        in_specs=[fwd(0), fwd(0), fwd(0), fwd(0), bwd(1), bwd(1), bwd(0), bwd(0),
                  tri_spec, mask_spec, tri_spec, mask_spec, _const_spec((HGRN_W, HGRN_W))],
        out_specs=[fwd(0), bwd(0)],
        out_shape=[out, out],
        scratch_shapes=[pltpu.VMEM((bb, HGRN_W, HGRN_W), F32), pltpu.VMEM((bb, HGRN_W, HGRN_W), F32)],
        compiler_params=pltpu.CompilerParams(dimension_semantics=("parallel", "arbitrary"),
                                             vmem_limit_bytes=VMEM_LIMIT),
        name="hgrn",
    )(lf, hk, hv, hq, lf, hk, hv, hq, tri_f, mask_f, tri_b, mask_b, bd)


def _merge_kernel(x_ref, u_ref, ob_ref, hf_ref, hb_ref, hg_ref, hn_ref, grp_ref, od_ref, g_ref, pw_ref, ps_ref,
                  wbp_ref, wbg_ref, wbh_ref, wbn_ref, wo_ref, o_ref, pool_ref, *, tm, T):
    i = pl.program_id(1)
    t0 = pl.multiple_of(i * tm, tm)
    win = tm + 2 * POOL_HALO
    w0 = pl.multiple_of(jnp.clip(t0 - POOL_HALO, 0, T - win), POOL_HALO)
    pad = jnp.zeros((POOL_PAD, POOL_WIDTH), F32)
    run = jnp.concatenate([pad, u_ref[0, pl.ds(w0, win), :].astype(F32), pad], axis=0)
    rows = win + 2 * POOL_PAD
    group = lax.broadcasted_iota(jnp.int32, (1, POOL_WIDTH), 1) // POOL_GW
    sums, width = None, 1
    for g, w in enumerate(POOL_WINDOWS):
        while width < w:
            run = run + pltpu.roll(run, width, 0)
            width *= 2
        centred = run if w == 2 else pltpu.roll(run, rows - (w // 2 - 1), 0)
        sums = centred if sums is None else jnp.where(group == g, centred, sums)
    pool_ref[...] = sums
    half_w = functools.reduce(lambda acc, gw: jnp.where(group == gw[0], gw[1] // 2, acc),
                              list(enumerate(POOL_WINDOWS))[1:], POOL_WINDOWS[0] // 2)
    tg = t0 + lax.broadcasted_iota(jnp.int32, (tm, POOL_WIDTH), 0)
    cnt = (jnp.minimum(tg + half_w, T) - jnp.maximum(tg - half_w, 0)).astype(F32)
    uc = u_ref[0, pl.ds(t0, tm), :].astype(F32)
    mixed = pool_ref[pl.ds(pl.multiple_of(POOL_PAD + t0 - w0, 8), tm), :] / cnt - uc
    sub = tm // MERGE_SUB
    for r in range(MERGE_SUB):
        rs = slice(r * sub, (r + 1) * sub)
        oa = _dot(mixed[rs].astype(BF16), pw_ref[...]) * ps_ref[...]
        hsum = hf_ref[0, rs, :].astype(F32) + hb_ref[0, rs, :].astype(F32)
        msq = _dot((hsum * hsum).astype(BF16), grp_ref[...])
        oc = hsum * lax.rsqrt(msq + EPS) * hn_ref[...] * hg_ref[0, rs, :].astype(F32)
        gate = lambda c: g_ref[0, rs, c * D_MODEL:(c + 1) * D_MODEL].astype(F32)
        merged = (gate(0) * _dot(oa.astype(BF16), wbp_ref[...])
                  + gate(1) * _dot(ob_ref[0, rs, :], wbg_ref[...])
                  + gate(2) * _dot(oc.astype(BF16), wbh_ref[...])
                  + gate(3) * _dot(od_ref[0, rs, :], wbn_ref[...]))
        o_ref[0, rs, :] = x_ref[0, rs, :] + _dot(merged.astype(BF16), wo_ref[...])


def _merge(x, u, ob, hf, hb, hg, hn, grp, od, gates, pw, ps, wbp, wbg, wbh, wbn, wo):
    Bt, T, _ = x.shape
    tm = TM_MERGE
    tok = lambda w: pl.BlockSpec((1, tm, w), lambda b, i: (b, i, 0))
    return pl.pallas_call(
        functools.partial(_merge_kernel, tm=tm, T=T),
        grid=(Bt, T // tm),
        in_specs=[
            tok(D_MODEL),
            pl.BlockSpec((1, T, POOL_WIDTH), lambda b, i: (b, 0, 0)),
            tok(GQA_QW), tok(HGRN_W), tok(HGRN_W), tok(HGRN_W), _const_spec((1, HGRN_W)),
            _const_spec((HGRN_W, HGRN_W)), tok(NAT_W), tok(N_BRANCH * D_MODEL),
            _const_spec((POOL_WIDTH, POOL_WIDTH)), _const_spec((1, POOL_WIDTH)),
            _const_spec((POOL_WIDTH, D_MODEL)), _const_spec((GQA_QW, D_MODEL)),
            _const_spec((HGRN_W, D_MODEL)), _const_spec((NAT_W, D_MODEL)),
            _const_spec((D_MODEL, D_MODEL)),
        ],
        out_specs=tok(D_MODEL),
        out_shape=jax.ShapeDtypeStruct((Bt, T, D_MODEL), F32),
        scratch_shapes=[pltpu.VMEM((tm + 2 * POOL_HALO + 2 * POOL_PAD, POOL_WIDTH), F32)],
        compiler_params=pltpu.CompilerParams(dimension_semantics=("parallel", "arbitrary"),
                                             vmem_limit_bytes=VMEM_LIMIT),
        name="merge",
    )(x, u, ob, hf, hb, hg, hn, grp, od, gates, pw, ps, wbp, wbg, wbh, wbn, wo)


def _mlp_kernel(x_ref, g_ref, wu_ref, wd_ref, o_ref):
    x = x_ref[...]
    ms = jnp.mean(x * x, axis=-1, keepdims=True)
    h = (x * lax.rsqrt(ms + EPS) * g_ref[...]).astype(BF16)
    acc = x
    for c in range(D_FF // D_MODEL):
        sl = slice(c * D_MODEL, (c + 1) * D_MODEL)
        hid = jnp.maximum(_dot(h, wu_ref[:, sl]), 0.0)
        acc = acc + _dot((hid * hid).astype(BF16), wd_ref[sl, :])
    o_ref[...] = acc


def _mlp(x2, g2, wu, wd):
    N = x2.shape[0]
    tm = TM_MLP
    return pl.pallas_call(
        _mlp_kernel,
        grid=(N // tm,),
        in_specs=[pl.BlockSpec((tm, D_MODEL), lambda i: (i, 0)), _const_spec((1, D_MODEL)),
                  _const_spec((D_MODEL, D_FF)), _const_spec((D_FF, D_MODEL))],
        out_specs=pl.BlockSpec((tm, D_MODEL), lambda i: (i, 0)),
        out_shape=jax.ShapeDtypeStruct((N, D_MODEL), F32),
        compiler_params=pltpu.CompilerParams(dimension_semantics=("parallel",), vmem_limit_bytes=VMEM_LIMIT),
        name="mlp",
    )(x2, g2, wu, wd)


def _block_diag(blocks):
    n, a, b = blocks.shape
    eye = jnp.eye(n, dtype=blocks.dtype)
    return (eye[:, None, :, None] * blocks[:, :, None, :]).reshape(n * a, n * b)


def _prepare(T, norm1_g, w_in, pool_w, pool_scale, gqa_qnorm, gqa_knorm, gqa_sink, hgrn_lb, hgrn_onorm,
             nat_qnorm, nat_knorm, nat_rpb, w_br_pool, w_br_gqa, w_br_hgrn, w_br_nat, w_o, norm2_g, w_up, w_down):
    D = D_MODEL
    half = HEAD_DIM // 2
    inv = ROPE_THETA ** (-jnp.arange(half, dtype=F32) / half)
    ang = jnp.arange(T, dtype=F32)[:, None] * inv[None, :]
    cos = jnp.tile(jnp.concatenate([jnp.cos(ang), jnp.cos(ang)], axis=1), (1, 2))
    sin = jnp.tile(jnp.concatenate([-jnp.sin(ang), jnp.sin(ang)], axis=1), (1, 2))
    sm = jax.nn.softmax(hgrn_lb.astype(F32), axis=0)
    lower = jnp.cumsum(sm, axis=0) - sm[:1]
    n_norm_heads = max(GQA_HEADS + GQA_KV_HEADS, 2 * NAT_HEADS, HGRN_HEADS)
    grp = _block_diag(jnp.full((n_norm_heads, HEAD_DIM, HEAD_DIM), 1.0 / HEAD_DIM, F32)).astype(BF16)
    scale = HEAD_DIM ** -0.5 * LOG2E
    order = jnp.asarray(GQA_HEAD_ORDER)
    layers = []
    for l in range(w_in.shape[0]):
        q_cols = [w_in[l][:, POOL_WIDTH + h * HEAD_DIM:POOL_WIDTH + (h + 1) * HEAD_DIM] for h in GQA_HEAD_ORDER]
        layers.append(dict(
            g1=norm1_g[l][None, :],
            w_in=jnp.concatenate([w_in[l][:, :POOL_WIDTH]] + q_cols + [w_in[l][:, POOL_WIDTH + GQA_QW:]],
                                 axis=1).astype(BF16),
            qk_gain=jnp.concatenate([jnp.tile(gqa_qnorm[l] * scale, GQA_HEADS),
                                     jnp.tile(gqa_knorm[l], GQA_KV_HEADS)])[None, :],
            n_gain=jnp.concatenate([jnp.tile(nat_qnorm[l] * scale, NAT_HEADS),
                                    jnp.tile(nat_knorm[l], NAT_HEADS)])[None, :],
            lower=lower[l].reshape(1, 2 * HGRN_W),
            sink=gqa_sink[l].astype(F32)[order] * LOG2E,
            nat_bias=_nat_bias_table(nat_rpb[l], T // GRID_W),
            onorm=jnp.tile(hgrn_onorm[l], HGRN_HEADS)[None, :],
            pool_w=_block_diag(pool_w[l]).astype(BF16), pool_scale=pool_scale[l][None, :],
            w_bp=w_br_pool[l].astype(BF16),
            w_bg=jnp.concatenate([w_br_gqa[l][h * HEAD_DIM:(h + 1) * HEAD_DIM] for h in GQA_HEAD_ORDER],
                                 axis=0).astype(BF16),
            w_bh=w_br_hgrn[l].astype(BF16), w_bn=w_br_nat[l].astype(BF16), w_o=w_o[l].astype(BF16),
            g2=norm2_g[l][None, :], w_up=w_up[l].astype(BF16), w_down=w_down[l].astype(BF16)))
    return dict(cos=cos, sin=sin, grp=grp, layers=layers)


def _trunk(x, prep):
    Bt, T, D = x.shape
    N = Bt * T
    grp = prep["grp"]
    r3 = lambda a: a.reshape(Bt, T, a.shape[-1])
    for p in prep["layers"]:
        (up, gq, gk, gv, lf, hk, hv, hq, hg, nq, nk, nv, gates) = _inproj(
            x.reshape(N, D), p["g1"], p["w_in"], prep["cos"], prep["sin"], p["qk_gain"], p["n_gain"], p["lower"], grp, T)
        ob = _gqa(r3(gq), r3(gk), r3(gv), p["sink"])
        od = _nat(r3(nq), r3(nk), r3(nv), p["nat_bias"])
        h_f, h_b = _hgrn(r3(lf), r3(hk), r3(hv), r3(hq))
        x = _merge(x, r3(up), ob, h_f, h_b, r3(hg), p["onorm"], grp[:HGRN_W, :HGRN_W], od, r3(gates),
                   p["pool_w"], p["pool_scale"],
                   p["w_bp"], p["w_bg"], p["w_bh"], p["w_bn"], p["w_o"])
        x = _mlp(x.reshape(N, D), p["g2"], p["w_up"], p["w_down"]).reshape(Bt, T, D)
    return x


def kernel(x_prompt, x_sample, norm1_g, w_in, pool_w, pool_scale, gqa_qnorm, gqa_knorm, gqa_sink, hgrn_lb,
           hgrn_onorm, nat_qnorm, nat_knorm, nat_rpb, w_br_pool, w_br_gqa, w_br_hgrn, w_br_nat, w_o, norm2_g,
           w_up, w_down):
    assert x_prompt.shape[1] == x_sample.shape[1]
    prep = _prepare(x_prompt.shape[1], norm1_g, w_in, pool_w, pool_scale, gqa_qnorm, gqa_knorm, gqa_sink, hgrn_lb,
                    hgrn_onorm, nat_qnorm, nat_knorm, nat_rpb, w_br_pool, w_br_gqa, w_br_hgrn, w_br_nat, w_o,
                    norm2_g, w_up, w_down)
    return (_trunk(x_prompt, prep), _trunk(x_sample, prep))
```

```python
import functools

import numpy as np

import jax
import jax.numpy as jnp
from jax import lax
from jax.experimental import pallas as pl
from jax.experimental.pallas import tpu as pltpu

F32 = jnp.float32
BF16 = jnp.bfloat16

D_MODEL = 1024
GRID_W = 64
EPS = 1e-6
NEG = -1e30
TINY = 1e-30
HEAD_DIM = 64
ROPE_THETA = 10000.0
LOG2E = 1.4426950408889634
N_BRANCH = 4
D_FF = 4 * D_MODEL

POOL_WIDTH = 256
POOL_WINDOWS = (2, 4, 8, 16)
POOL_GW = 64
POOL_HALO = 16
POOL_PAD = 8

GQA_HEADS = 8
GQA_KV_HEADS = 2
GQA_GROUP = GQA_HEADS // GQA_KV_HEADS
GQA_WINDOW = 128
GQA_QW = GQA_HEADS * HEAD_DIM
GQA_KVW = GQA_KV_HEADS * HEAD_DIM

HGRN_HEADS = 4
HGRN_W = HGRN_HEADS * HEAD_DIM

NAT_HEADS = 4
NAT_KR = 8
NAT_KC = 16
NAT_W = NAT_HEADS * HEAD_DIM
NAT_QR = 4
NAT_KROWS = 12
NAT_SUB = 8
GQA_HEAD_ORDER = tuple((j % 2) * GQA_GROUP + j // 2 for j in range(GQA_HEADS))

_IN_GROUPS = (POOL_WIDTH, GQA_QW + GQA_KVW, GQA_KVW, 2 * HGRN_W, HGRN_W, HGRN_W, HGRN_W, 2 * NAT_W, NAT_W) \
    + (D_MODEL,) * N_BRANCH
_IN_EDGES = tuple(int(e) for e in np.cumsum((0,) + _IN_GROUPS))
IN_WIDTH = _IN_EDGES[-1]
IN_BLOCKS = tuple(tuple((a, b) for a, b in zip(_IN_EDGES[:-1], _IN_EDGES[1:]) if a // D_MODEL == blk)
                  for blk in range(IN_WIDTH // D_MODEL))

TM_IN = 512
TQ_GQA = 128
GQA_SUB = 8
TT_HGRN = 128
HGRN_BB = 4
HGRN_CHUNK = 32
HGRN_SAFE = 86.0
TM_MERGE = 1024
MERGE_SUB = 2
TM_MLP = 1024
V7X_VMEM_BYTES = 64 * 1024 * 1024
V7X_MXU_DIM = 256
VMEM_LIMIT = V7X_VMEM_BYTES * 7 // 8


def _sigmoid(x):
    return 0.5 * jnp.tanh(0.5 * x) + 0.5


def _dot(a, b):
    return jnp.dot(a, b, preferred_element_type=F32)


def _dot_nt(a, b):
    return lax.dot_general(a, b, (((1,), (1,)), ((), ())), preferred_element_type=F32)


def _dot_tn(a, b):
    return lax.dot_general(a, b, (((0,), (0,)), ((), ())), preferred_element_type=F32)


def _const_spec(shape):
    n = len(shape)
    return pl.BlockSpec(shape, lambda *_: (0,) * n, pipeline_mode=pl.Buffered(1))


def _inproj_kernel(x_ref, g_ref, w_ref, cos_ref, sin_ref, qkg_ref, ng_ref, lb_ref, grp_ref,
                   up_ref, gq_ref, gk_ref, gv_ref, lf_ref, hk_ref, hv_ref, hq_ref, hg_ref,
                   nq_ref, nk_ref, nv_ref, gate_ref):
    x = x_ref[...]
    ms = jnp.mean(x * x, axis=-1, keepdims=True)
    h = (x * lax.rsqrt(ms + EPS) * g_ref[...]).astype(BF16)

    def headnorm(y, gain):
        y2 = (y * y).astype(BF16)
        tile = grp_ref.shape[0]
        msq = jnp.concatenate([_dot(y2[:, a:min(a + tile, y.shape[1])], grp_ref[0:min(tile, y.shape[1] - a),
                                                                                0:min(tile, y.shape[1] - a)])
                               for a in range(0, y.shape[1], tile)], axis=1)
        return y * lax.rsqrt(msq + EPS) * gain

    def cols(block):
        z = _dot(h, w_ref[:, block * D_MODEL:(block + 1) * D_MODEL])
        return [z[:, a - block * D_MODEL:b - block * D_MODEL] for a, b in IN_BLOCKS[block]]

    zp, zqk, zv = cols(0)
    up_ref[...] = zp.astype(BF16)
    qk = headnorm(zqk, qkg_ref[...])
    qkw = GQA_QW + GQA_KVW
    reps = qkw // cos_ref.shape[1]
    cos = jnp.concatenate([cos_ref[...]] * reps, axis=1)
    sin = jnp.concatenate([sin_ref[...]] * reps, axis=1)
    first = (lax.broadcasted_iota(jnp.int32, (1, qkw), 1) % HEAD_DIM) < (HEAD_DIM // 2)
    partner = jnp.where(first, pltpu.roll(qk, qkw - HEAD_DIM // 2, 1), pltpu.roll(qk, HEAD_DIM // 2, 1))
    qk = qk * cos + partner * sin
    gq_ref[...] = qk[:, 0:GQA_QW].astype(BF16)
    gk_ref[...] = qk[:, GQA_QW:qkw].astype(BF16)
    gv_ref[...] = zv.astype(BF16)

    zf, zi, zq = cols(1)
    lb = lb_ref[...]
    f = lb + (1.0 - lb) / (1.0 + jnp.exp(-zf))
    lf_ref[...] = jnp.log2(jnp.maximum(f, TINY))
    hk_ref[...] = (1.0 - f).astype(BF16)
    hv_ref[...] = zi.astype(BF16)
    hq_ref[...] = (zq * _sigmoid(zq)).astype(BF16)

    zg, znqk, znv = cols(2)
    hg_ref[...] = (zg * _sigmoid(zg)).astype(BF16)
    nqk = headnorm(znqk, ng_ref[...])
    nq_ref[...] = nqk[:, 0:NAT_W].astype(BF16)
    nk_ref[...] = nqk[:, NAT_W:2 * NAT_W].astype(BF16)
    nv_ref[...] = znv.astype(BF16)

    for c in range(N_BRANCH):
        (z,) = cols(3 + c)
        gate_ref[:, c * D_MODEL:(c + 1) * D_MODEL] = _sigmoid(z).astype(BF16)


def _inproj(x2, g1, w_in, cos, sin, qk_gain, n_gain, lb, grp, T):
    N = x2.shape[0]
    tm = TM_IN
    tpb = T // tm
    tok = lambda w: pl.BlockSpec((tm, w), lambda i: (i, 0))
    widths = [(POOL_WIDTH, BF16), (GQA_QW, BF16), (GQA_KVW, BF16), (GQA_KVW, BF16), (2 * HGRN_W, F32),
              (2 * HGRN_W, BF16), (HGRN_W, BF16), (HGRN_W, BF16), (HGRN_W, BF16), (NAT_W, BF16), (NAT_W, BF16),
              (NAT_W, BF16), (N_BRANCH * D_MODEL, BF16)]
    return pl.pallas_call(
        _inproj_kernel,
        grid=(N // tm,),
        in_specs=[
            tok(D_MODEL),
            _const_spec((1, D_MODEL)),
            _const_spec((D_MODEL, IN_WIDTH)),
            pl.BlockSpec((tm, cos.shape[1]), lambda i: (i % tpb, 0)),
            pl.BlockSpec((tm, sin.shape[1]), lambda i: (i % tpb, 0)),
            _const_spec((1, GQA_QW + GQA_KVW)),
            _const_spec((1, 2 * NAT_W)),
            _const_spec((1, 2 * HGRN_W)),
            _const_spec(grp.shape),
        ],
        out_specs=[tok(w) for w, _ in widths],
        out_shape=[jax.ShapeDtypeStruct((N, w), dt) for w, dt in widths],
        compiler_params=pltpu.CompilerParams(dimension_semantics=("parallel",), vmem_limit_bytes=VMEM_LIMIT),
        name="inproj",
    )(x2, g1, w_in, cos, sin, qk_gain, n_gain, lb, grp)


def _gqa_kernel(sink_ref, q_ref, k_ref, v_ref, o_ref, *, tq, T):
    for sub in range(GQA_SUB):
        t0 = (pl.program_id(1) * GQA_SUB + sub) * tq
        o_ref[0, sub * tq:(sub + 1) * tq, :] = _gqa_tile(sink_ref, q_ref[0, sub * tq:(sub + 1) * tq, :], k_ref, v_ref,
                                                        t0, tq, T)


def _gqa_tile(sink_ref, q, k_ref, v_ref, t0, tq, T):
    win = tq + 2 * GQA_WINDOW
    w0 = pl.multiple_of(jnp.clip(t0 - GQA_WINDOW, 0, T - win), GQA_WINDOW)
    k = k_ref[0, pl.ds(w0, win), :]
    v = v_ref[0, pl.ds(w0, win), :]
    qpos = t0 + lax.broadcasted_iota(jnp.int32, (tq, 1), 0)
    kpos = w0 + lax.broadcasted_iota(jnp.int32, (1, win), 1)
    valid = jnp.abs(kpos - qpos) <= GQA_WINDOW
    pw = 2 * HEAD_DIM
    lane = lax.broadcasted_iota(jnp.int32, (1, pw), 1)
    half = [jnp.where(lane < HEAD_DIM, 1.0, 0.0).astype(BF16), jnp.where(lane < HEAD_DIM, 0.0, 1.0).astype(BF16)]
    qs = jnp.concatenate([q[:, (j // 2) * pw:(j // 2 + 1) * pw] * half[j % 2] for j in range(GQA_HEADS)], axis=0)
    s_all = _dot_nt(qs, k)
    ps, invs = [], []
    for j in range(GQA_HEADS):
        s = jnp.where(valid, s_all[j * tq:(j + 1) * tq], NEG)
        sink = sink_ref[j]
        m = jnp.maximum(jnp.max(s, axis=-1, keepdims=True), sink)
        p = jnp.exp2(s - m)
        invs.append(1.0 / (jnp.sum(p, axis=-1, keepdims=True) + jnp.exp2(sink - m)))
        ps.append(p.astype(BF16))
    o_all = _dot(jnp.concatenate(ps, axis=0), v)
    o_heads = [o_all[j * tq:(j + 1) * tq] * invs[j] for j in range(GQA_HEADS)]
    outs = [jnp.where(lane < HEAD_DIM, o_heads[2 * pair], o_heads[2 * pair + 1]) for pair in range(GQA_HEADS // 2)]
    return jnp.concatenate(outs, axis=1).astype(BF16)


def _gqa(q, k, v, sink):
    Bt, T, _ = q.shape
    tq = TQ_GQA
    step = tq * GQA_SUB
    return pl.pallas_call(
        functools.partial(_gqa_kernel, tq=tq, T=T),
        grid=(Bt, T // step),
        in_specs=[
            pl.BlockSpec(memory_space=pltpu.SMEM),
            pl.BlockSpec((1, step, GQA_QW), lambda b, i: (b, i, 0)),
            pl.BlockSpec((1, T, GQA_KVW), lambda b, i: (b, 0, 0)),
            pl.BlockSpec((1, T, GQA_KVW), lambda b, i: (b, 0, 0)),
        ],
        out_specs=pl.BlockSpec((1, step, GQA_QW), lambda b, i: (b, i, 0)),
        out_shape=jax.ShapeDtypeStruct((Bt, T, GQA_QW), BF16),
        compiler_params=pltpu.CompilerParams(dimension_semantics=("parallel", "arbitrary"),
                                             vmem_limit_bytes=VMEM_LIMIT),
        name="gqa",
    )(sink, q, k, v)


def _nat_kernel(q_ref, k_ref, v_ref, bias_ref, o_ref, *, nblk):
    nq, nk = NAT_QR * GRID_W, NAT_KROWS * GRID_W
    lane_head = lax.broadcasted_iota(jnp.int32, (1, NAT_W), 1) // HEAD_DIM
    for sub in range(NAT_SUB):
        blk = pl.program_id(1) * NAT_SUB + sub
        first = jnp.clip(blk - 1, 0, nblk - NAT_KROWS // NAT_QR)
        variant = jnp.where(blk == 0, 0, jnp.where(blk == nblk - 1, 2, 1))
        off = pl.multiple_of(first * nq, nq)
        q = q_ref[0, sub * nq:(sub + 1) * nq, :]
        k = k_ref[0, pl.ds(off, nk), :]
        v = v_ref[0, pl.ds(off, nk), :]
        out = None
        for h in range(NAT_HEADS):
            s = _dot_nt(q * jnp.where(lane_head == h, 1.0, 0.0).astype(BF16), k) + bias_ref[variant, h]
            m = jnp.max(s, axis=-1, keepdims=True)
            p = jnp.exp2(s - m)
            inv = 1.0 / jnp.sum(p, axis=-1, keepdims=True)
            o_h = _dot(p.astype(BF16), v) * inv
            out = o_h if out is None else jnp.where(lane_head == h, o_h, out)
        o_ref[0, sub * nq:(sub + 1) * nq, :] = out.astype(BF16)


def _nat(q, k, v, bias):
    Bt, T, _ = q.shape
    rows = T // GRID_W
    nblk = rows // NAT_QR
    assert rows % (NAT_QR * NAT_SUB) == 0 and rows >= NAT_KROWS
    nq = NAT_QR * GRID_W
    return pl.pallas_call(
        functools.partial(_nat_kernel, nblk=nblk),
        grid=(Bt, nblk // NAT_SUB),
        in_specs=[
            pl.BlockSpec((1, NAT_SUB * nq, NAT_W), lambda b, r: (b, r, 0)),
            pl.BlockSpec((1, T, NAT_W), lambda b, r: (b, 0, 0)),
            pl.BlockSpec((1, T, NAT_W), lambda b, r: (b, 0, 0)),
            _const_spec((3, NAT_HEADS, nq, NAT_KROWS * GRID_W)),
        ],
        out_specs=pl.BlockSpec((1, NAT_SUB * nq, NAT_W), lambda b, r: (b, r, 0)),
        out_shape=jax.ShapeDtypeStruct((Bt, T, NAT_W), BF16),
        compiler_params=pltpu.CompilerParams(dimension_semantics=("parallel", "arbitrary"),
                                             vmem_limit_bytes=VMEM_LIMIT),
        name="nat",
    )(q, k, v, bias)


def _nat_bias_table(rpb, rows):
    nblk = rows // NAT_QR
    c = np.arange(GRID_W)[:, None]
    kc = np.arange(GRID_W)[None, :]
    qc0 = np.clip(c - NAT_KC // 2, 0, GRID_W - NAT_KC)
    col_ok = (kc >= qc0) & (kc < qc0 + NAT_KC)
    dcol = np.clip(kc - c + (NAT_KC - 1), 0, 2 * NAT_KC - 2)
    oh_col = (dcol[None] == np.arange(2 * NAT_KC - 1)[:, None, None]) & col_ok[None]
    oh_row = np.zeros((3, NAT_QR, NAT_KROWS, 2 * NAT_KR - 1), np.float32)
    for var, blk in enumerate((0, 1, nblk - 1)):
        first = int(np.clip(blk - 1, 0, nblk - NAT_KROWS // NAT_QR))
        for ri in range(NAT_QR):
            r = blk * NAT_QR + ri
            r0 = int(np.clip(r - NAT_KR // 2, 0, rows - NAT_KR))
            for j in range(NAT_KROWS):
                kr = first * NAT_QR + j
                if r0 <= kr < r0 + NAT_KR:
                    oh_row[var, ri, j, kr - r + NAT_KR - 1] = 1.0
    oh_row = np.concatenate([oh_row, 1.0 - oh_row.sum(-1, keepdims=True)], axis=-1)
    hp = lax.Precision.HIGHEST
    by_col = jnp.einsum('hde,eck->hdck', rpb.astype(F32), jnp.asarray(oh_col, F32), precision=hp)
    by_col = jnp.where(jnp.asarray(col_ok)[None, None], by_col * LOG2E, NEG)
    by_col = jnp.concatenate([by_col, jnp.full_like(by_col[:, :1], NEG)], axis=1)
    bias = jnp.einsum('vrjd,hdck->vhrcjk', jnp.asarray(oh_row), by_col, precision=hp)
    return bias.reshape(3, NAT_HEADS, NAT_QR * GRID_W, NAT_KROWS * GRID_W)


def _hgrn_consts(Tt, C, rev):
    nc = Tt // C
    t = np.arange(Tt)
    row, col = t[:, None], t[None, :]
    same = ((row // C) == (col // C)) & ((col >= row) if rev else (col <= row))
    at = [(nc - 1 - p) if rev else p for p in range(nc)]
    pair = np.concatenate([np.broadcast_to((t // C) == at[p - 1], (C, Tt)) for p in (1, 3)])
    masks = np.concatenate([np.tile(same, (HGRN_HEADS, 1)), np.tile(pair, (HGRN_HEADS, 1))]).astype(np.float32)
    hl = np.arange(HGRN_W) // HEAD_DIM
    bd = (hl[:, None] == hl[None, :]).astype(np.float32)
    return jnp.asarray(same.astype(np.float32), BF16), jnp.asarray(masks), jnp.asarray(bd)


def _hgrn_kernel(lf_f, k_f, v_f, q_f, lf_b, k_b, v_b, q_b, tri_f, mask_f, tri_b, mask_b, bd_ref,
                 of_ref, ob_ref, st_f, st_b, *, Tt, C):
    @pl.when(pl.program_id(1) == 0)
    def _():
        st_f[...] = jnp.zeros_like(st_f)
        st_b[...] = jnp.zeros_like(st_b)

    tiles = []
    for bi in range(HGRN_BB):
        tiles.append((of_ref, bi) + _hgrn_tile(lf_f.at[bi], k_f.at[bi], v_f.at[bi], q_f.at[bi], tri_f, mask_f, bd_ref,
                                               st_f.at[bi], rev=False, Tt=Tt, C=C))
        tiles.append((ob_ref, bi) + _hgrn_tile(lf_b.at[bi], k_b.at[bi], v_b.at[bi], q_b.at[bi], tri_b, mask_b, bd_ref,
                                               st_b.at[bi], rev=True, Tt=Tt, C=C))
    for o_ref, bi, o, _, _ in tiles:
        o_ref[bi] = o.astype(BF16)

    @pl.when(functools.reduce(jnp.minimum, [t[3] for t in tiles]) < -HGRN_SAFE)
    def _():
        for o_ref, bi, _, _, redo in tiles:
            o_ref[bi] = redo().astype(BF16)


def _hgrn_tile(lf_ref, k_ref, v_ref, q_ref, tri_ref, mask_ref, bd_ref, st_ref, *, rev, Tt, C):
    nc = Tt // C
    lf = lf_ref[...]
    lane_head = lax.broadcasted_iota(jnp.int32, (1, HGRN_W), 1) // HEAD_DIM
    head_sel = [jnp.where(lane_head == h, 1.0, 0.0).astype(BF16) for h in range(HGRN_HEADS)]

    def stack_heads(x16):
        return jnp.concatenate([x16 * m for m in head_sel], axis=0)

    tri = tri_ref[...]
    hi = lf.astype(BF16)
    lo = (lf - hi.astype(F32)).astype(BF16)
    bc = _dot(tri, hi) + _dot(tri, lo)

    cs = [slice(c * C, (c + 1) * C) for c in range(nc)]
    pos = [(nc - 1 - c) if rev else c for c in range(nc)]
    at = [pos.index(p) for p in range(nc)]
    last = 0 if rev else C - 1
    tot = [bc[c * C + last:c * C + last + 1] for c in range(nc)]
    zero = jnp.zeros((C, HGRN_W), F32)
    cat = lambda xs: jnp.concatenate(xs, axis=0).astype(BF16)

    def intra_fast():
        ka = (k * jnp.exp2(-bc)).astype(BF16)
        return jnp.where(mask_ref[0:HGRN_HEADS * Tt, :] > 0.5, _dot_nt(stack_heads(cat(qa)), ka), 0.0)

    def intra_safe():
        q = q_ref[...].astype(F32)
        k = k_ref[...].astype(F32)
        t_idx = lax.broadcasted_iota(jnp.int32, (Tt, 1), 0)
        tau = (Tt - 1 - t_idx) if rev else t_idx
        row = lax.broadcasted_iota(jnp.int32, (HGRN_HEADS * Tt, Tt), 0) % Tt
        col = lax.broadcasted_iota(jnp.int32, (HGRN_HEADS * Tt, Tt), 1)
        acc = jnp.where(row == col, _dot_nt(stack_heads(q_ref[...]), k_ref[...]), 0.0)
        before = bc - lf
        end = bc
        size = 2
        while size <= C:
            half = size // 2
            upper = (tau % size) >= half
            w = jnp.exp2(jnp.where(upper, bc - before, end - bc))
            qt = jnp.where(upper, q * w, 0.0).astype(BF16)
            kt = jnp.where(upper, 0.0, k * w).astype(BF16)
            acc = acc + jnp.where((row // size) == (col // size), _dot_nt(stack_heads(qt), kt), 0.0)
            if size < C:
                back, fwd = (Tt - half, half) if rev else (half, Tt - half)
                before = jnp.where(upper, pltpu.roll(before, back, 0), before)
                end = jnp.where(upper, end, pltpu.roll(end, fwd, 0))
            size *= 2
        return acc

    q = q_ref[...].astype(F32)
    k = k_ref[...].astype(F32)
    v = v_ref[...]
    qa = [q[cs[c]] * jnp.exp2(bc[cs[c]]) for c in range(nc)]
    kb = [k[cs[c]] * jnp.exp2(tot[c] - bc[cs[c]]) for c in range(nc)]
    q_pair = cat([qa[at[1]], qa[at[3]]])
    k_pair = cat([kb[c] if pos[c] % 2 == 0 else zero for c in range(nc)])
    a_pair = _dot_nt(stack_heads(q_pair), k_pair) * mask_ref[HGRN_HEADS * Tt:, :]
    q_half = cat([qa[at[2]], qa[at[3]] * jnp.exp2(tot[at[2]])])
    k_half = cat([zero if pos[c] >= 2 else (kb[c] * jnp.exp2(tot[at[1]]) if pos[c] == 0 else kb[c]) for c in range(nc)])
    a_half = _dot_nt(stack_heads(q_half), k_half)

    def add_cross(a_intra):
        blocks = []
        for h in range(HGRN_HEADS):
            for c in range(nc):
                blk = a_intra[h * Tt + c * C:h * Tt + (c + 1) * C]
                if pos[c] % 2 == 1:
                    r0 = h * 2 * C + (pos[c] // 2) * C
                    blk = blk + a_pair[r0:r0 + C]
                if pos[c] >= 2:
                    r0 = h * 2 * C + (pos[c] - 2) * C
                    blk = blk + a_half[r0:r0 + C]
                blocks.append(blk)
        return jnp.concatenate(blocks, axis=0)

    pre, suf = [None] * nc, [None] * nc
    run = jnp.zeros_like(tot[0])
    for p in range(nc):
        pre[at[p]] = run
        run = run + tot[at[p]]
    b_tile = run
    run = jnp.zeros_like(tot[0])
    for p in reversed(range(nc)):
        suf[at[p]] = run
        run = run + tot[at[p]]
    q_in = cat([qa[c] * jnp.exp2(pre[c]) for c in range(nc)])
    k_out = cat([kb[c] * jnp.exp2(suf[c]) for c in range(nc)])

    st = st_ref[...]
    o_inter = _dot_nt(q_in, st.astype(BF16))
    st_ref[...] = st * jnp.exp2(b_tile) + _dot_tn(v, k_out) * bd_ref[...]

    def output(a_intra):
        o_heads = _dot(add_cross(a_intra).astype(BF16), v)
        o = o_inter
        for h in range(HGRN_HEADS):
            o = o + jnp.where(lane_head == h, o_heads[h * Tt:(h + 1) * Tt], 0.0)
        return o

    return output(intra_fast()), jnp.min(functools.reduce(jnp.minimum, tot)), lambda: output(intra_safe())


def _hgrn(lf, hk, hv, hq):
    Bt, T, _ = hv.shape
    Tt, C = TT_HGRN, HGRN_CHUNK
    bb = HGRN_BB
    assert Tt == 4 * C and Bt % bb == 0
    nt = T // Tt
    fwd = lambda col: pl.BlockSpec((bb, Tt, HGRN_W), lambda b, n: (b, n, col))
    bwd = lambda col: pl.BlockSpec((bb, Tt, HGRN_W), lambda b, n: (b, nt - 1 - n, col))
    tri_f, mask_f, bd = _hgrn_consts(Tt, C, False)
    tri_b, mask_b, _ = _hgrn_consts(Tt, C, True)
    tri_spec, mask_spec = _const_spec((Tt, Tt)), _const_spec((HGRN_HEADS * (Tt + 2 * C), Tt))
    out = jax.ShapeDtypeStruct((Bt, T, HGRN_W), BF16)
    return pl.pallas_call(
        functools.partial(_hgrn_kernel, Tt=Tt, C=C),
        grid=(Bt // bb, nt),
        in_specs=[fwd(0), fwd(0), fwd(0), fwd(0), bwd(1), bwd(1), bwd(0), bwd(0),
                  tri_spec, mask_spec, tri_spec, mask_spec, _const_spec((HGRN_W, HGRN_W))],
        out_specs=[fwd(0), bwd(0)],
        out_shape=[out, out],
        scratch_shapes=[pltpu.VMEM((bb, HGRN_W, HGRN_W), F32), pltpu.VMEM((bb, HGRN_W, HGRN_W), F32)],
        compiler_params=pltpu.CompilerParams(dimension_semantics=("parallel", "arbitrary"),
                                             vmem_limit_bytes=VMEM_LIMIT),
        name="hgrn",
    )(lf, hk, hv, hq, lf, hk, hv, hq, tri_f, mask_f, tri_b, mask_b, bd)


def _merge_kernel(x_ref, u_ref, ob_ref, hf_ref, hb_ref, hg_ref, hn_ref, grp_ref, od_ref, g_ref, pw_ref, ps_ref,
                  wbp_ref, wbg_ref, wbh_ref, wbn_ref, wo_ref, o_ref, pool_ref, *, tm, T):
    i = pl.program_id(1)
    t0 = pl.multiple_of(i * tm, tm)
    win = tm + 2 * POOL_HALO
    w0 = pl.multiple_of(jnp.clip(t0 - POOL_HALO, 0, T - win), POOL_HALO)
    pad = jnp.zeros((POOL_PAD, POOL_WIDTH), F32)
    run = jnp.concatenate([pad, u_ref[0, pl.ds(w0, win), :].astype(F32), pad], axis=0)
    rows = win + 2 * POOL_PAD
    group = lax.broadcasted_iota(jnp.int32, (1, POOL_WIDTH), 1) // POOL_GW
    sums, width = None, 1
    for g, w in enumerate(POOL_WINDOWS):
        while width < w:
            run = run + pltpu.roll(run, width, 0)
            width *= 2
        centred = run if w == 2 else pltpu.roll(run, rows - (w // 2 - 1), 0)
        sums = centred if sums is None else jnp.where(group == g, centred, sums)
    pool_ref[...] = sums
    half_w = functools.reduce(lambda acc, gw: jnp.where(group == gw[0], gw[1] // 2, acc),
                              list(enumerate(POOL_WINDOWS))[1:], POOL_WINDOWS[0] // 2)
    tg = t0 + lax.broadcasted_iota(jnp.int32, (tm, POOL_WIDTH), 0)
    cnt = (jnp.minimum(tg + half_w, T) - jnp.maximum(tg - half_w, 0)).astype(F32)
    uc = u_ref[0, pl.ds(t0, tm), :].astype(F32)
    mixed = pool_ref[pl.ds(pl.multiple_of(POOL_PAD + t0 - w0, 8), tm), :] / cnt - uc
    sub = tm // MERGE_SUB
    for r in range(MERGE_SUB):
        rs = slice(r * sub, (r + 1) * sub)
        oa = _dot(mixed[rs].astype(BF16), pw_ref[...]) * ps_ref[...]
        hsum = hf_ref[0, rs, :].astype(F32) + hb_ref[0, rs, :].astype(F32)
        msq = _dot((hsum * hsum).astype(BF16), grp_ref[...])
        oc = hsum * lax.rsqrt(msq + EPS) * hn_ref[...] * hg_ref[0, rs, :].astype(F32)
        gate = lambda c: g_ref[0, rs, c * D_MODEL:(c + 1) * D_MODEL].astype(F32)
        merged = (gate(0) * _dot(oa.astype(BF16), wbp_ref[...])
                  + gate(1) * _dot(ob_ref[0, rs, :], wbg_ref[...])
                  + gate(2) * _dot(oc.astype(BF16), wbh_ref[...])
                  + gate(3) * _dot(od_ref[0, rs, :], wbn_ref[...]))
        o_ref[0, rs, :] = x_ref[0, rs, :] + _dot(merged.astype(BF16), wo_ref[...])


def _merge(x, u, ob, hf, hb, hg, hn, grp, od, gates, pw, ps, wbp, wbg, wbh, wbn, wo):
    Bt, T, _ = x.shape
    tm = TM_MERGE
    tok = lambda w: pl.BlockSpec((1, tm, w), lambda b, i: (b, i, 0))
    return pl.pallas_call(
        functools.partial(_merge_kernel, tm=tm, T=T),
        grid=(Bt, T // tm),
        in_specs=[
            tok(D_MODEL),
            pl.BlockSpec((1, T, POOL_WIDTH), lambda b, i: (b, 0, 0)),
            tok(GQA_QW), tok(HGRN_W), tok(HGRN_W), tok(HGRN_W), _const_spec((1, HGRN_W)),
            _const_spec((HGRN_W, HGRN_W)), tok(NAT_W), tok(N_BRANCH * D_MODEL),
            _const_spec((POOL_WIDTH, POOL_WIDTH)), _const_spec((1, POOL_WIDTH)),
            _const_spec((POOL_WIDTH, D_MODEL)), _const_spec((GQA_QW, D_MODEL)),
            _const_spec((HGRN_W, D_MODEL)), _const_spec((NAT_W, D_MODEL)),
            _const_spec((D_MODEL, D_MODEL)),
        ],
        out_specs=tok(D_MODEL),
        out_shape=jax.ShapeDtypeStruct((Bt, T, D_MODEL), F32),
        scratch_shapes=[pltpu.VMEM((tm + 2 * POOL_HALO + 2 * POOL_PAD, POOL_WIDTH), F32)],
        compiler_params=pltpu.CompilerParams(dimension_semantics=("parallel", "arbitrary"),
                                             vmem_limit_bytes=VMEM_LIMIT),
        name="merge",
    )(x, u, ob, hf, hb, hg, hn, grp, od, gates, pw, ps, wbp, wbg, wbh, wbn, wo)


def _mlp_kernel(x_ref, g_ref, wu_ref, wd_ref, o_ref):
    x = x_ref[...]
    ms = jnp.mean(x * x, axis=-1, keepdims=True)
    h = (x * lax.rsqrt(ms + EPS) * g_ref[...]).astype(BF16)
    acc = x
    for c in range(D_FF // D_MODEL):
        sl = slice(c * D_MODEL, (c + 1) * D_MODEL)
        hid = jnp.maximum(_dot(h, wu_ref[:, sl]), 0.0)
        acc = acc + _dot((hid * hid).astype(BF16), wd_ref[sl, :])
    o_ref[...] = acc


def _mlp(x2, g2, wu, wd):
    N = x2.shape[0]
    tm = TM_MLP
    return pl.pallas_call(
        _mlp_kernel,
        grid=(N // tm,),
        in_specs=[pl.BlockSpec((tm, D_MODEL), lambda i: (i, 0)), _const_spec((1, D_MODEL)),
                  _const_spec((D_MODEL, D_FF)), _const_spec((D_FF, D_MODEL))],
        out_specs=pl.BlockSpec((tm, D_MODEL), lambda i: (i, 0)),
        out_shape=jax.ShapeDtypeStruct((N, D_MODEL), F32),
        compiler_params=pltpu.CompilerParams(dimension_semantics=("parallel",), vmem_limit_bytes=VMEM_LIMIT),
        name="mlp",
    )(x2, g2, wu, wd)


def _block_diag(blocks):
    n, a, b = blocks.shape
    eye = jnp.eye(n, dtype=blocks.dtype)
    return (eye[:, None, :, None] * blocks[:, :, None, :]).reshape(n * a, n * b)


def _prepare(T, norm1_g, w_in, pool_w, pool_scale, gqa_qnorm, gqa_knorm, gqa_sink, hgrn_lb, hgrn_onorm,
             nat_qnorm, nat_knorm, nat_rpb, w_br_pool, w_br_gqa, w_br_hgrn, w_br_nat, w_o, norm2_g, w_up, w_down):
    D = D_MODEL
    half = HEAD_DIM // 2
    inv = ROPE_THETA ** (-jnp.arange(half, dtype=F32) / half)
    ang = jnp.arange(T, dtype=F32)[:, None] * inv[None, :]
    cos = jnp.tile(jnp.concatenate([jnp.cos(ang), jnp.cos(ang)], axis=1), (1, 2))
    sin = jnp.tile(jnp.concatenate([-jnp.sin(ang), jnp.sin(ang)], axis=1), (1, 2))
    sm = jax.nn.softmax(hgrn_lb.astype(F32), axis=0)
    lower = jnp.cumsum(sm, axis=0) - sm[:1]
    grp = _block_diag(jnp.full((V7X_MXU_DIM // HEAD_DIM, HEAD_DIM, HEAD_DIM), 1.0 / HEAD_DIM, F32)).astype(BF16)
    scale = HEAD_DIM ** -0.5 * LOG2E
    order = jnp.asarray(GQA_HEAD_ORDER)
    layers = []
    for l in range(w_in.shape[0]):
        q_cols = [w_in[l][:, POOL_WIDTH + h * HEAD_DIM:POOL_WIDTH + (h + 1) * HEAD_DIM] for h in GQA_HEAD_ORDER]
        layers.append(dict(
            g1=norm1_g[l][None, :],
            w_in=jnp.concatenate([w_in[l][:, :POOL_WIDTH]] + q_cols + [w_in[l][:, POOL_WIDTH + GQA_QW:]],
                                 axis=1).astype(BF16),
            qk_gain=jnp.concatenate([jnp.tile(gqa_qnorm[l] * scale, GQA_HEADS),
                                     jnp.tile(gqa_knorm[l], GQA_KV_HEADS)])[None, :],
            n_gain=jnp.concatenate([jnp.tile(nat_qnorm[l] * scale, NAT_HEADS),
                                    jnp.tile(nat_knorm[l], NAT_HEADS)])[None, :],
            lower=lower[l].reshape(1, 2 * HGRN_W),
            sink=gqa_sink[l].astype(F32)[order] * LOG2E,
            nat_bias=_nat_bias_table(nat_rpb[l], T // GRID_W),
            onorm=jnp.tile(hgrn_onorm[l], HGRN_HEADS)[None, :],
            pool_w=_block_diag(pool_w[l]).astype(BF16), pool_scale=pool_scale[l][None, :],
            w_bp=w_br_pool[l].astype(BF16),
            w_bg=jnp.concatenate([w_br_gqa[l][h * HEAD_DIM:(h + 1) * HEAD_DIM] for h in GQA_HEAD_ORDER],
                                 axis=0).astype(BF16),
            w_bh=w_br_hgrn[l].astype(BF16), w_bn=w_br_nat[l].astype(BF16), w_o=w_o[l].astype(BF16),
            g2=norm2_g[l][None, :], w_up=w_up[l].astype(BF16), w_down=w_down[l].astype(BF16)))
    return dict(cos=cos, sin=sin, grp=grp, layers=layers)


def _trunk(x, prep):
    Bt, T, D = x.shape
    N = Bt * T
    grp = prep["grp"]
    r3 = lambda a: a.reshape(Bt, T, a.shape[-1])
    for p in prep["layers"]:
        (up, gq, gk, gv, lf, hk, hv, hq, hg, nq, nk, nv, gates) = _inproj(
            x.reshape(N, D), p["g1"], p["w_in"], prep["cos"], prep["sin"], p["qk_gain"], p["n_gain"], p["lower"], grp, T)
        ob = _gqa(r3(gq), r3(gk), r3(gv), p["sink"])
        od = _nat(r3(nq), r3(nk), r3(nv), p["nat_bias"])
        h_f, h_b = _hgrn(r3(lf), r3(hk), r3(hv), r3(hq))
        x = _merge(x, r3(up), ob, h_f, h_b, r3(hg), p["onorm"], grp, od, r3(gates),
                   p["pool_w"], p["pool_scale"],
                   p["w_bp"], p["w_bg"], p["w_bh"], p["w_bn"], p["w_o"])
        x = _mlp(x.reshape(N, D), p["g2"], p["w_up"], p["w_down"]).reshape(Bt, T, D)
    return x


def kernel(x_prompt, x_sample, norm1_g, w_in, pool_w, pool_scale, gqa_qnorm, gqa_knorm, gqa_sink, hgrn_lb,
           hgrn_onorm, nat_qnorm, nat_knorm, nat_rpb, w_br_pool, w_br_gqa, w_br_hgrn, w_br_nat, w_o, norm2_g,
           w_up, w_down):
    assert x_prompt.shape[1] == x_sample.shape[1]
    prep = _prepare(x_prompt.shape[1], norm1_g, w_in, pool_w, pool_scale, gqa_qnorm, gqa_knorm, gqa_sink, hgrn_lb,
                    hgrn_onorm, nat_qnorm, nat_knorm, nat_rpb, w_br_pool, w_br_gqa, w_br_hgrn, w_br_nat, w_o,
                    norm2_g, w_up, w_down)
    return (_trunk(x_prompt, prep), _trunk(x_sample, prep))
```

```python
import functools

import numpy as np

import jax
import jax.numpy as jnp
from jax import lax
from jax.experimental import pallas as pl
from jax.experimental.pallas import tpu as pltpu

F32 = jnp.float32
BF16 = jnp.bfloat16

D_MODEL = 1024
GRID_W = 64
EPS = 1e-6
NEG = -1e30
TINY = 1e-30
HEAD_DIM = 64
ROPE_THETA = 10000.0
LOG2E = 1.4426950408889634
N_BRANCH = 4
D_FF = 4 * D_MODEL

POOL_WIDTH = 256
POOL_WINDOWS = (2, 4, 8, 16)
POOL_GW = 64
POOL_HALO = 16
POOL_PAD = 8

GQA_HEADS = 8
GQA_KV_HEADS = 2
GQA_GROUP = GQA_HEADS // GQA_KV_HEADS
GQA_WINDOW = 128
GQA_QW = GQA_HEADS * HEAD_DIM
GQA_KVW = GQA_KV_HEADS * HEAD_DIM

HGRN_HEADS = 4
HGRN_W = HGRN_HEADS * HEAD_DIM

NAT_HEADS = 4
NAT_KR = 8
NAT_KC = 16
NAT_W = NAT_HEADS * HEAD_DIM
NAT_QR = 4
NAT_KROWS = 12
NAT_SUB = 8
NAT_QCB = 16
NAT_KCB = NAT_QCB + NAT_KC
GQA_HEAD_ORDER = tuple((j % 2) * GQA_GROUP + j // 2 for j in range(GQA_HEADS))

_IN_GROUPS = (POOL_WIDTH, GQA_QW + GQA_KVW, GQA_KVW, 2 * HGRN_W, HGRN_W, HGRN_W, HGRN_W, 2 * NAT_W, NAT_W) \
    + (D_MODEL,) * N_BRANCH
_IN_EDGES = tuple(int(e) for e in np.cumsum((0,) + _IN_GROUPS))
IN_WIDTH = _IN_EDGES[-1]
IN_BLOCKS = tuple(tuple((a, b) for a, b in zip(_IN_EDGES[:-1], _IN_EDGES[1:]) if a // D_MODEL == blk)
                  for blk in range(IN_WIDTH // D_MODEL))

TM_IN = 512
TQ_GQA = 128
GQA_SUB = 8
TT_HGRN = 128
HGRN_BB = 4
HGRN_CHUNK = 32
HGRN_SAFE = 86.0
TM_MERGE = 1024
MERGE_SUB = 2
TM_MLP = 1024
V7X_VMEM_BYTES = 64 * 1024 * 1024
V7X_MXU_DIM = 256
VMEM_LIMIT = V7X_VMEM_BYTES * 7 // 8


def _sigmoid(x):
    return 0.5 * jnp.tanh(0.5 * x) + 0.5


def _dot(a, b):
    return jnp.dot(a, b, preferred_element_type=F32)


def _dot_nt(a, b):
    return lax.dot_general(a, b, (((1,), (1,)), ((), ())), preferred_element_type=F32)


def _dot_tn(a, b):
    return lax.dot_general(a, b, (((0,), (0,)), ((), ())), preferred_element_type=F32)


def _const_spec(shape):
    n = len(shape)
    return pl.BlockSpec(shape, lambda *_: (0,) * n, pipeline_mode=pl.Buffered(1))


def _inproj_kernel(x_ref, g_ref, w_ref, cos_ref, sin_ref, qkg_ref, ng_ref, lb_ref, grp_ref,
                   up_ref, gq_ref, gk_ref, gv_ref, lf_ref, hk_ref, hv_ref, hq_ref, hg_ref,
                   nq_ref, nk_ref, nv_ref, gate_ref):
    x = x_ref[...]
    ms = jnp.mean(x * x, axis=-1, keepdims=True)
    h = (x * lax.rsqrt(ms + EPS) * g_ref[...]).astype(BF16)

    def headnorm(y, gain):
        y2 = (y * y).astype(BF16)
        tile = grp_ref.shape[0]
        msq = jnp.concatenate([_dot(y2[:, a:min(a + tile, y.shape[1])], grp_ref[0:min(tile, y.shape[1] - a),
                                                                                0:min(tile, y.shape[1] - a)])
                               for a in range(0, y.shape[1], tile)], axis=1)
        return y * lax.rsqrt(msq + EPS) * gain

    def cols(block):
        z = _dot(h, w_ref[:, block * D_MODEL:(block + 1) * D_MODEL])
        return [z[:, a - block * D_MODEL:b - block * D_MODEL] for a, b in IN_BLOCKS[block]]

    zp, zqk, zv = cols(0)
    up_ref[...] = zp.astype(BF16)
    qk = headnorm(zqk, qkg_ref[...])
    qkw = GQA_QW + GQA_KVW
    reps = qkw // cos_ref.shape[1]
    cos = jnp.concatenate([cos_ref[...]] * reps, axis=1)
    sin = jnp.concatenate([sin_ref[...]] * reps, axis=1)
    first = (lax.broadcasted_iota(jnp.int32, (1, qkw), 1) % HEAD_DIM) < (HEAD_DIM // 2)
    partner = jnp.where(first, pltpu.roll(qk, qkw - HEAD_DIM // 2, 1), pltpu.roll(qk, HEAD_DIM // 2, 1))
    qk = qk * cos + partner * sin
    gq_ref[...] = qk[:, 0:GQA_QW].astype(BF16)
    gk_ref[...] = qk[:, GQA_QW:qkw].astype(BF16)
    gv_ref[...] = zv.astype(BF16)

    zf, zi, zq = cols(1)
    lb = lb_ref[...]
    f = lb + (1.0 - lb) / (1.0 + jnp.exp(-zf))
    lf_ref[...] = jnp.log2(jnp.maximum(f, TINY))
    hk_ref[...] = (1.0 - f).astype(BF16)
    hv_ref[...] = zi.astype(BF16)
    hq_ref[...] = (zq * _sigmoid(zq)).astype(BF16)

    zg, znqk, znv = cols(2)
    hg_ref[...] = (zg * _sigmoid(zg)).astype(BF16)
    nqk = headnorm(znqk, ng_ref[...])
    nq_ref[...] = nqk[:, 0:NAT_W].astype(BF16)
    nk_ref[...] = nqk[:, NAT_W:2 * NAT_W].astype(BF16)
    nv_ref[...] = znv.astype(BF16)

    for c in range(N_BRANCH):
        (z,) = cols(3 + c)
        gate_ref[:, c * D_MODEL:(c + 1) * D_MODEL] = _sigmoid(z).astype(BF16)


def _inproj(x2, g1, w_in, cos, sin, qk_gain, n_gain, lb, grp, T):
    N = x2.shape[0]
    tm = TM_IN
    tpb = T // tm
    tok = lambda w: pl.BlockSpec((tm, w), lambda i: (i, 0))
    widths = [(POOL_WIDTH, BF16), (GQA_QW, BF16), (GQA_KVW, BF16), (GQA_KVW, BF16), (2 * HGRN_W, F32),
              (2 * HGRN_W, BF16), (HGRN_W, BF16), (HGRN_W, BF16), (HGRN_W, BF16), (NAT_W, BF16), (NAT_W, BF16),
              (NAT_W, BF16), (N_BRANCH * D_MODEL, BF16)]
    return pl.pallas_call(
        _inproj_kernel,
        grid=(N // tm,),
        in_specs=[
            tok(D_MODEL),
            _const_spec((1, D_MODEL)),
            _const_spec((D_MODEL, IN_WIDTH)),
            pl.BlockSpec((tm, cos.shape[1]), lambda i: (i % tpb, 0)),
            pl.BlockSpec((tm, sin.shape[1]), lambda i: (i % tpb, 0)),
            _const_spec((1, GQA_QW + GQA_KVW)),
            _const_spec((1, 2 * NAT_W)),
            _const_spec((1, 2 * HGRN_W)),
            _const_spec(grp.shape),
        ],
        out_specs=[tok(w) for w, _ in widths],
        out_shape=[jax.ShapeDtypeStruct((N, w), dt) for w, dt in widths],
        compiler_params=pltpu.CompilerParams(dimension_semantics=("parallel",), vmem_limit_bytes=VMEM_LIMIT),
        name="inproj",
    )(x2, g1, w_in, cos, sin, qk_gain, n_gain, lb, grp)


def _gqa_kernel(sink_ref, q_ref, k_ref, v_ref, o_ref, *, tq, T):
    for sub in range(GQA_SUB):
        t0 = (pl.program_id(1) * GQA_SUB + sub) * tq
        o_ref[0, sub * tq:(sub + 1) * tq, :] = _gqa_tile(sink_ref, q_ref[0, sub * tq:(sub + 1) * tq, :], k_ref, v_ref,
                                                        t0, tq, T)


def _gqa_tile(sink_ref, q, k_ref, v_ref, t0, tq, T):
    win = tq + 2 * GQA_WINDOW
    w0 = pl.multiple_of(jnp.clip(t0 - GQA_WINDOW, 0, T - win), GQA_WINDOW)
    k = k_ref[0, pl.ds(w0, win), :]
    v = v_ref[0, pl.ds(w0, win), :]
    qpos = t0 + lax.broadcasted_iota(jnp.int32, (tq, 1), 0)
    kpos = w0 + lax.broadcasted_iota(jnp.int32, (1, win), 1)
    valid = jnp.abs(kpos - qpos) <= GQA_WINDOW
    pw = 2 * HEAD_DIM
    lane = lax.broadcasted_iota(jnp.int32, (1, pw), 1)
    half = [jnp.where(lane < HEAD_DIM, 1.0, 0.0).astype(BF16), jnp.where(lane < HEAD_DIM, 0.0, 1.0).astype(BF16)]
    qs = jnp.concatenate([q[:, (j // 2) * pw:(j // 2 + 1) * pw] * half[j % 2] for j in range(GQA_HEADS)], axis=0)
    s_all = _dot_nt(qs, k)
    ps, invs = [], []
    for j in range(GQA_HEADS):
        s = jnp.where(valid, s_all[j * tq:(j + 1) * tq], NEG)
        sink = sink_ref[j]
        m = jnp.maximum(jnp.max(s, axis=-1, keepdims=True), sink)
        p = jnp.exp2(s - m)
        invs.append(1.0 / (jnp.sum(p, axis=-1, keepdims=True) + jnp.exp2(sink - m)))
        ps.append(p.astype(BF16))
    o_all = _dot(jnp.concatenate(ps, axis=0), v)
    o_heads = [o_all[j * tq:(j + 1) * tq] * invs[j] for j in range(GQA_HEADS)]
    outs = [jnp.where(lane < HEAD_DIM, o_heads[2 * pair], o_heads[2 * pair + 1]) for pair in range(GQA_HEADS // 2)]
    return jnp.concatenate(outs, axis=1).astype(BF16)


def _gqa(q, k, v, sink):
    Bt, T, _ = q.shape
    tq = TQ_GQA
    step = tq * GQA_SUB
    return pl.pallas_call(
        functools.partial(_gqa_kernel, tq=tq, T=T),
        grid=(Bt, T // step),
        in_specs=[
            pl.BlockSpec(memory_space=pltpu.SMEM),
            pl.BlockSpec((1, step, GQA_QW), lambda b, i: (b, i, 0)),
            pl.BlockSpec((1, T, GQA_KVW), lambda b, i: (b, 0, 0)),
            pl.BlockSpec((1, T, GQA_KVW), lambda b, i: (b, 0, 0)),
        ],
        out_specs=pl.BlockSpec((1, step, GQA_QW), lambda b, i: (b, i, 0)),
        out_shape=jax.ShapeDtypeStruct((Bt, T, GQA_QW), BF16),
        compiler_params=pltpu.CompilerParams(dimension_semantics=("parallel", "arbitrary"),
                                             vmem_limit_bytes=VMEM_LIMIT),
        name="gqa",
    )(sink, q, k, v)


def _nat_kernel(q_ref, k_ref, v_ref, bias_ref, o_ref, *, nblk):
    nq, nk = NAT_QR * GRID_W, NAT_KROWS * GRID_W
    lane_head = lax.broadcasted_iota(jnp.int32, (1, NAT_W), 1) // HEAD_DIM
    head_sel = [jnp.where(lane_head == h, 1.0, 0.0).astype(BF16) for h in range(NAT_HEADS)]
    nqc = NAT_QR * NAT_QCB
    for sub in range(NAT_SUB):
        blk = pl.program_id(1) * NAT_SUB + sub
        first = jnp.clip(blk - 1, 0, nblk - NAT_KROWS // NAT_QR)
        variant = jnp.where(blk == 0, 0, jnp.where(blk == nblk - 1, 2, 1))
        off = pl.multiple_of(first * nq, nq)
        k = k_ref[0, pl.ds(off, nk), :]
        v = v_ref[0, pl.ds(off, nk), :]
        for cb in range(GRID_W // NAT_QCB):
            c0 = min(max(cb * NAT_QCB - NAT_KC // 2, 0), GRID_W - NAT_KCB)
            qrows = [slice(sub * nq + ri * GRID_W + cb * NAT_QCB, sub * nq + ri * GRID_W + (cb + 1) * NAT_QCB)
                     for ri in range(NAT_QR)]
            q = jnp.concatenate([q_ref[0, r, :] for r in qrows], axis=0)
            kc = jnp.concatenate([k[j * GRID_W + c0:j * GRID_W + c0 + NAT_KCB] for j in range(NAT_KROWS)], axis=0)
            vc = jnp.concatenate([v[j * GRID_W + c0:j * GRID_W + c0 + NAT_KCB] for j in range(NAT_KROWS)], axis=0)
            s = _dot_nt(jnp.concatenate([q * m for m in head_sel], axis=0), kc) + bias_ref[variant, cb]
            m = jnp.max(s, axis=-1, keepdims=True)
            p = jnp.exp2(s - m)
            inv = 1.0 / jnp.sum(p, axis=-1, keepdims=True)
            o_all = _dot(p.astype(BF16), vc) * inv
            out = o_all[0:nqc]
            for h in range(1, NAT_HEADS):
                out = jnp.where(lane_head == h, o_all[h * nqc:(h + 1) * nqc], out)
            for ri, r in enumerate(qrows):
                o_ref[0, r, :] = out[ri * NAT_QCB:(ri + 1) * NAT_QCB].astype(BF16)


def _nat(q, k, v, bias):
    Bt, T, _ = q.shape
    rows = T // GRID_W
    nblk = rows // NAT_QR
    assert rows % (NAT_QR * NAT_SUB) == 0 and rows >= NAT_KROWS
    nq = NAT_QR * GRID_W
    return pl.pallas_call(
        functools.partial(_nat_kernel, nblk=nblk),
        grid=(Bt, nblk // NAT_SUB),
        in_specs=[
            pl.BlockSpec((1, NAT_SUB * nq, NAT_W), lambda b, r: (b, r, 0)),
            pl.BlockSpec((1, T, NAT_W), lambda b, r: (b, 0, 0)),
            pl.BlockSpec((1, T, NAT_W), lambda b, r: (b, 0, 0)),
            _const_spec(bias.shape),
        ],
        out_specs=pl.BlockSpec((1, NAT_SUB * nq, NAT_W), lambda b, r: (b, r, 0)),
        out_shape=jax.ShapeDtypeStruct((Bt, T, NAT_W), BF16),
        compiler_params=pltpu.CompilerParams(dimension_semantics=("parallel", "arbitrary"),
                                             vmem_limit_bytes=VMEM_LIMIT),
        name="nat",
    )(q, k, v, bias)


def _nat_bias_table(rpb, rows):
    nblk = rows // NAT_QR
    ncb = GRID_W // NAT_QCB
    cb = np.arange(ncb)[:, None, None]
    c = cb * NAT_QCB + np.arange(NAT_QCB)[None, :, None]
    kc = np.clip(cb * NAT_QCB - NAT_KC // 2, 0, GRID_W - NAT_KCB) + np.arange(NAT_KCB)[None, None, :]
    qc0 = np.clip(c - NAT_KC // 2, 0, GRID_W - NAT_KC)
    col_ok = (kc >= qc0) & (kc < qc0 + NAT_KC)
    dcol = np.clip(kc - c + (NAT_KC - 1), 0, 2 * NAT_KC - 2)
    oh_col = (dcol[None] == np.arange(2 * NAT_KC - 1)[:, None, None, None]) & col_ok[None]
    oh_row = np.zeros((3, NAT_QR, NAT_KROWS, 2 * NAT_KR - 1), np.float32)
    for var, blk in enumerate((0, 1, nblk - 1)):
        first = int(np.clip(blk - 1, 0, nblk - NAT_KROWS // NAT_QR))
        for ri in range(NAT_QR):
            r = blk * NAT_QR + ri
            r0 = int(np.clip(r - NAT_KR // 2, 0, rows - NAT_KR))
            for j in range(NAT_KROWS):
                kr = first * NAT_QR + j
                if r0 <= kr < r0 + NAT_KR:
                    oh_row[var, ri, j, kr - r + NAT_KR - 1] = 1.0
    oh_row = np.concatenate([oh_row, 1.0 - oh_row.sum(-1, keepdims=True)], axis=-1)
    hp = lax.Precision.HIGHEST
    by_col = jnp.einsum('hde,ebck->hdbck', rpb.astype(F32), jnp.asarray(oh_col, F32), precision=hp)
    by_col = jnp.where(jnp.asarray(col_ok)[None, None], by_col * LOG2E, NEG)
    by_col = jnp.concatenate([by_col, jnp.full_like(by_col[:, :1], NEG)], axis=1)
    bias = jnp.einsum('vrjd,hdbck->vbhrcjk', jnp.asarray(oh_row), by_col, precision=hp)
    return bias.reshape(3, ncb, NAT_HEADS * NAT_QR * NAT_QCB, NAT_KROWS * NAT_KCB)


def _hgrn_consts(Tt, C, rev):
    nc = Tt // C
    t = np.arange(Tt)
    row, col = t[:, None], t[None, :]
    same = ((row // C) == (col // C)) & ((col >= row) if rev else (col <= row))
    at = [(nc - 1 - p) if rev else p for p in range(nc)]
    pair = np.concatenate([np.broadcast_to((t // C) == at[p - 1], (C, Tt)) for p in (1, 3)])
    masks = np.concatenate([np.tile(same, (HGRN_HEADS, 1)), np.tile(pair, (HGRN_HEADS, 1))]).astype(np.float32)
    hl = np.arange(HGRN_W) // HEAD_DIM
    bd = (hl[:, None] == hl[None, :]).astype(np.float32)
    return jnp.asarray(same.astype(np.float32), BF16), jnp.asarray(masks), jnp.asarray(bd)


def _hgrn_kernel(lf_f, k_f, v_f, q_f, lf_b, k_b, v_b, q_b, tri_f, mask_f, tri_b, mask_b, bd_ref,
                 of_ref, ob_ref, st_f, st_b, *, Tt, C):
    @pl.when(pl.program_id(1) == 0)
    def _():
        st_f[...] = jnp.zeros_like(st_f)
        st_b[...] = jnp.zeros_like(st_b)

    tiles = []
    for bi in range(HGRN_BB):
        tiles.append((of_ref, bi) + _hgrn_tile(lf_f.at[bi], k_f.at[bi], v_f.at[bi], q_f.at[bi], tri_f, mask_f, bd_ref,
                                               st_f.at[bi], rev=False, Tt=Tt, C=C))
        tiles.append((ob_ref, bi) + _hgrn_tile(lf_b.at[bi], k_b.at[bi], v_b.at[bi], q_b.at[bi], tri_b, mask_b, bd_ref,
                                               st_b.at[bi], rev=True, Tt=Tt, C=C))
    for o_ref, bi, o, _, _ in tiles:
        o_ref[bi] = o.astype(BF16)

    @pl.when(functools.reduce(jnp.minimum, [t[3] for t in tiles]) < -HGRN_SAFE)
    def _():
        for o_ref, bi, _, _, redo in tiles:
            o_ref[bi] = redo().astype(BF16)


def _hgrn_tile(lf_ref, k_ref, v_ref, q_ref, tri_ref, mask_ref, bd_ref, st_ref, *, rev, Tt, C):
    nc = Tt // C
    lf = lf_ref[...]
    lane_head = lax.broadcasted_iota(jnp.int32, (1, HGRN_W), 1) // HEAD_DIM
    head_sel = [jnp.where(lane_head == h, 1.0, 0.0).astype(BF16) for h in range(HGRN_HEADS)]

    def stack_heads(x16):
        return jnp.concatenate([x16 * m for m in head_sel], axis=0)

    tri = tri_ref[...]
    hi = lf.astype(BF16)
    lo = (lf - hi.astype(F32)).astype(BF16)
    bc = _dot(tri, hi) + _dot(tri, lo)

    cs = [slice(c * C, (c + 1) * C) for c in range(nc)]
    pos = [(nc - 1 - c) if rev else c for c in range(nc)]
    at = [pos.index(p) for p in range(nc)]
    last = 0 if rev else C - 1
    tot = [bc[c * C + last:c * C + last + 1] for c in range(nc)]
    zero = jnp.zeros((C, HGRN_W), F32)
    cat = lambda xs: jnp.concatenate(xs, axis=0).astype(BF16)

    def intra_fast():
        ka = (k * jnp.exp2(-bc)).astype(BF16)
        return jnp.where(mask_ref[0:HGRN_HEADS * Tt, :] > 0.5, _dot_nt(stack_heads(cat(qa)), ka), 0.0)

    def intra_safe():
        q = q_ref[...].astype(F32)
        k = k_ref[...].astype(F32)
        t_idx = lax.broadcasted_iota(jnp.int32, (Tt, 1), 0)
        tau = (Tt - 1 - t_idx) if rev else t_idx
        row = lax.broadcasted_iota(jnp.int32, (HGRN_HEADS * Tt, Tt), 0) % Tt
        col = lax.broadcasted_iota(jnp.int32, (HGRN_HEADS * Tt, Tt), 1)
        acc = jnp.where(row == col, _dot_nt(stack_heads(q_ref[...]), k_ref[...]), 0.0)
        before = bc - lf
        end = bc
        size = 2
        while size <= C:
            half = size // 2
            upper = (tau % size) >= half
            w = jnp.exp2(jnp.where(upper, bc - before, end - bc))
            qt = jnp.where(upper, q * w, 0.0).astype(BF16)
            kt = jnp.where(upper, 0.0, k * w).astype(BF16)
            acc = acc + jnp.where((row // size) == (col // size), _dot_nt(stack_heads(qt), kt), 0.0)
            if size < C:
                back, fwd = (Tt - half, half) if rev else (half, Tt - half)
                before = jnp.where(upper, pltpu.roll(before, back, 0), before)
                end = jnp.where(upper, end, pltpu.roll(end, fwd, 0))
            size *= 2
        return acc

    q = q_ref[...].astype(F32)
    k = k_ref[...].astype(F32)
    v = v_ref[...]
    qa = [q[cs[c]] * jnp.exp2(bc[cs[c]]) for c in range(nc)]
    kb = [k[cs[c]] * jnp.exp2(tot[c] - bc[cs[c]]) for c in range(nc)]
    q_pair = cat([qa[at[1]], qa[at[3]]])
    k_pair = cat([kb[c] if pos[c] % 2 == 0 else zero for c in range(nc)])
    a_pair = _dot_nt(stack_heads(q_pair), k_pair) * mask_ref[HGRN_HEADS * Tt:, :]
    q_half = cat([qa[at[2]], qa[at[3]] * jnp.exp2(tot[at[2]])])
    k_half = cat([zero if pos[c] >= 2 else (kb[c] * jnp.exp2(tot[at[1]]) if pos[c] == 0 else kb[c]) for c in range(nc)])
    a_half = _dot_nt(stack_heads(q_half), k_half)

    def add_cross(a_intra):
        blocks = []
        for h in range(HGRN_HEADS):
            for c in range(nc):
                blk = a_intra[h * Tt + c * C:h * Tt + (c + 1) * C]
                if pos[c] % 2 == 1:
                    r0 = h * 2 * C + (pos[c] // 2) * C
                    blk = blk + a_pair[r0:r0 + C]
                if pos[c] >= 2:
                    r0 = h * 2 * C + (pos[c] - 2) * C
                    blk = blk + a_half[r0:r0 + C]
                blocks.append(blk)
        return jnp.concatenate(blocks, axis=0)

    pre, suf = [None] * nc, [None] * nc
    run = jnp.zeros_like(tot[0])
    for p in range(nc):
        pre[at[p]] = run
        run = run + tot[at[p]]
    b_tile = run
    run = jnp.zeros_like(tot[0])
    for p in reversed(range(nc)):
        suf[at[p]] = run
        run = run + tot[at[p]]
    q_in = cat([qa[c] * jnp.exp2(pre[c]) for c in range(nc)])
    k_out = cat([kb[c] * jnp.exp2(suf[c]) for c in range(nc)])

    st = st_ref[...]
    o_inter = _dot_nt(q_in, st.astype(BF16))
    st_ref[...] = st * jnp.exp2(b_tile) + _dot_tn(v, k_out) * bd_ref[...]

    def output(a_intra):
        o_heads = _dot(add_cross(a_intra).astype(BF16), v)
        o = o_inter
        for h in range(HGRN_HEADS):
            o = o + jnp.where(lane_head == h, o_heads[h * Tt:(h + 1) * Tt], 0.0)
        return o

    return output(intra_fast()), jnp.min(functools.reduce(jnp.minimum, tot)), lambda: output(intra_safe())


def _hgrn(lf, hk, hv, hq):
    Bt, T, _ = hv.shape
    Tt, C = TT_HGRN, HGRN_CHUNK
    bb = HGRN_BB
    assert Tt == 4 * C and Bt % bb == 0
    nt = T // Tt
    fwd = lambda col: pl.BlockSpec((bb, Tt, HGRN_W), lambda b, n: (b, n, col))
    bwd = lambda col: pl.BlockSpec((bb, Tt, HGRN_W), lambda b, n: (b, nt - 1 - n, col))
    tri_f, mask_f, bd = _hgrn_consts(Tt, C, False)
    tri_b, mask_b, _ = _hgrn_consts(Tt, C, True)
    tri_spec, mask_spec = _const_spec((Tt, Tt)), _const_spec((HGRN_HEADS * (Tt + 2 * C), Tt))
    out = jax.ShapeDtypeStruct((Bt, T, HGRN_W), BF16)
    return pl.pallas_call(
        functools.partial(_hgrn_kernel, Tt=Tt, C=C),
        grid=(Bt // bb, nt),
        in_specs=[fwd(0), fwd(0), fwd(0), fwd(0), bwd(1), bwd(1), bwd(0), bwd(0),
                  tri_spec, mask_spec, tri_spec, mask_spec, _const_spec((HGRN_W, HGRN_W))],
        out_specs=[fwd(0), bwd(0)],
        out_shape=[out, out],
        scratch_shapes=[pltpu.VMEM((bb, HGRN_W, HGRN_W), F32), pltpu.VMEM((bb, HGRN_W, HGRN_W), F32)],
        compiler_params=pltpu.CompilerParams(dimension_semantics=("parallel", "arbitrary"),
                                             vmem_limit_bytes=VMEM_LIMIT),
        name="hgrn",
    )(lf, hk, hv, hq, lf, hk, hv, hq, tri_f, mask_f, tri_b, mask_b, bd)


def _merge_kernel(x_ref, u_ref, ob_ref, hf_ref, hb_ref, hg_ref, hn_ref, grp_ref, od_ref, g_ref, pw_ref, ps_ref,
                  wbp_ref, wbg_ref, wbh_ref, wbn_ref, wo_ref, o_ref, pool_ref, *, tm, T):
    i = pl.program_id(1)
    t0 = pl.multiple_of(i * tm, tm)
    win = tm + 2 * POOL_HALO
    w0 = pl.multiple_of(jnp.clip(t0 - POOL_HALO, 0, T - win), POOL_HALO)
    pad = jnp.zeros((POOL_PAD, POOL_WIDTH), F32)
    run = jnp.concatenate([pad, u_ref[0, pl.ds(w0, win), :].astype(F32), pad], axis=0)
    rows = win + 2 * POOL_PAD
    group = lax.broadcasted_iota(jnp.int32, (1, POOL_WIDTH), 1) // POOL_GW
    sums, width = None, 1
    for g, w in enumerate(POOL_WINDOWS):
        while width < w:
            run = run + pltpu.roll(run, width, 0)
            width *= 2
        centred = run if w == 2 else pltpu.roll(run, rows - (w // 2 - 1), 0)
        sums = centred if sums is None else jnp.where(group == g, centred, sums)
    pool_ref[...] = sums
    half_w = functools.reduce(lambda acc, gw: jnp.where(group == gw[0], gw[1] // 2, acc),
                              list(enumerate(POOL_WINDOWS))[1:], POOL_WINDOWS[0] // 2)
    tg = t0 + lax.broadcasted_iota(jnp.int32, (tm, POOL_WIDTH), 0)
    cnt = (jnp.minimum(tg + half_w, T) - jnp.maximum(tg - half_w, 0)).astype(F32)
    uc = u_ref[0, pl.ds(t0, tm), :].astype(F32)
    mixed = pool_ref[pl.ds(pl.multiple_of(POOL_PAD + t0 - w0, 8), tm), :] / cnt - uc
    sub = tm // MERGE_SUB
    for r in range(MERGE_SUB):
        rs = slice(r * sub, (r + 1) * sub)
        oa = _dot(mixed[rs].astype(BF16), pw_ref[...]) * ps_ref[...]
        hsum = hf_ref[0, rs, :].astype(F32) + hb_ref[0, rs, :].astype(F32)
        msq = _dot((hsum * hsum).astype(BF16), grp_ref[...])
        oc = hsum * lax.rsqrt(msq + EPS) * hn_ref[...] * hg_ref[0, rs, :].astype(F32)
        gate = lambda c: g_ref[0, rs, c * D_MODEL:(c + 1) * D_MODEL].astype(F32)
        merged = (gate(0) * _dot(oa.astype(BF16), wbp_ref[...])
                  + gate(1) * _dot(ob_ref[0, rs, :], wbg_ref[...])
                  + gate(2) * _dot(oc.astype(BF16), wbh_ref[...])
                  + gate(3) * _dot(od_ref[0, rs, :], wbn_ref[...]))
        o_ref[0, rs, :] = x_ref[0, rs, :] + _dot(merged.astype(BF16), wo_ref[...])


def _merge(x, u, ob, hf, hb, hg, hn, grp, od, gates, pw, ps, wbp, wbg, wbh, wbn, wo):
    Bt, T, _ = x.shape
    tm = TM_MERGE
    tok = lambda w: pl.BlockSpec((1, tm, w), lambda b, i: (b, i, 0))
    return pl.pallas_call(
        functools.partial(_merge_kernel, tm=tm, T=T),
        grid=(Bt, T // tm),
        in_specs=[
            tok(D_MODEL),
            pl.BlockSpec((1, T, POOL_WIDTH), lambda b, i: (b, 0, 0)),
            tok(GQA_QW), tok(HGRN_W), tok(HGRN_W), tok(HGRN_W), _const_spec((1, HGRN_W)),
            _const_spec((HGRN_W, HGRN_W)), tok(NAT_W), tok(N_BRANCH * D_MODEL),
            _const_spec((POOL_WIDTH, POOL_WIDTH)), _const_spec((1, POOL_WIDTH)),
            _const_spec((POOL_WIDTH, D_MODEL)), _const_spec((GQA_QW, D_MODEL)),
            _const_spec((HGRN_W, D_MODEL)), _const_spec((NAT_W, D_MODEL)),
            _const_spec((D_MODEL, D_MODEL)),
        ],
        out_specs=tok(D_MODEL),
        out_shape=jax.ShapeDtypeStruct((Bt, T, D_MODEL), F32),
        scratch_shapes=[pltpu.VMEM((tm + 2 * POOL_HALO + 2 * POOL_PAD, POOL_WIDTH), F32)],
        compiler_params=pltpu.CompilerParams(dimension_semantics=("parallel", "arbitrary"),
                                             vmem_limit_bytes=VMEM_LIMIT),
        name="merge",
    )(x, u, ob, hf, hb, hg, hn, grp, od, gates, pw, ps, wbp, wbg, wbh, wbn, wo)


def _mlp_kernel(x_ref, g_ref, wu_ref, wd_ref, o_ref):
    x = x_ref[...]
    ms = jnp.mean(x * x, axis=-1, keepdims=True)
    h = (x * lax.rsqrt(ms + EPS) * g_ref[...]).astype(BF16)
    acc = x
    for c in range(D_FF // D_MODEL):
        sl = slice(c * D_MODEL, (c + 1) * D_MODEL)
        hid = jnp.maximum(_dot(h, wu_ref[:, sl]), 0.0)
        acc = acc + _dot((hid * hid).astype(BF16), wd_ref[sl, :])
    o_ref[...] = acc


def _mlp(x2, g2, wu, wd):
    N = x2.shape[0]
    tm = TM_MLP
    return pl.pallas_call(
        _mlp_kernel,
        grid=(N // tm,),
        in_specs=[pl.BlockSpec((tm, D_MODEL), lambda i: (i, 0)), _const_spec((1, D_MODEL)),
                  _const_spec((D_MODEL, D_FF)), _const_spec((D_FF, D_MODEL))],
        out_specs=pl.BlockSpec((tm, D_MODEL), lambda i: (i, 0)),
        out_shape=jax.ShapeDtypeStruct((N, D_MODEL), F32),
        compiler_params=pltpu.CompilerParams(dimension_semantics=("parallel",), vmem_limit_bytes=VMEM_LIMIT),
        name="mlp",
    )(x2, g2, wu, wd)


def _block_diag(blocks):
    n, a, b = blocks.shape
    eye = jnp.eye(n, dtype=blocks.dtype)
    return (eye[:, None, :, None] * blocks[:, :, None, :]).reshape(n * a, n * b)


def _prepare(T, norm1_g, w_in, pool_w, pool_scale, gqa_qnorm, gqa_knorm, gqa_sink, hgrn_lb, hgrn_onorm,
             nat_qnorm, nat_knorm, nat_rpb, w_br_pool, w_br_gqa, w_br_hgrn, w_br_nat, w_o, norm2_g, w_up, w_down):
    D = D_MODEL
    half = HEAD_DIM // 2
    inv = ROPE_THETA ** (-jnp.arange(half, dtype=F32) / half)
    ang = jnp.arange(T, dtype=F32)[:, None] * inv[None, :]
    cos = jnp.tile(jnp.concatenate([jnp.cos(ang), jnp.cos(ang)], axis=1), (1, 2))
    sin = jnp.tile(jnp.concatenate([-jnp.sin(ang), jnp.sin(ang)], axis=1), (1, 2))
    sm = jax.nn.softmax(hgrn_lb.astype(F32), axis=0)
    lower = jnp.cumsum(sm, axis=0) - sm[:1]
    grp = _block_diag(jnp.full((V7X_MXU_DIM // HEAD_DIM, HEAD_DIM, HEAD_DIM), 1.0 / HEAD_DIM, F32)).astype(BF16)
    scale = HEAD_DIM ** -0.5 * LOG2E
    order = jnp.asarray(GQA_HEAD_ORDER)
    layers = []
    for l in range(w_in.shape[0]):
        q_cols = [w_in[l][:, POOL_WIDTH + h * HEAD_DIM:POOL_WIDTH + (h + 1) * HEAD_DIM] for h in GQA_HEAD_ORDER]
        layers.append(dict(
            g1=norm1_g[l][None, :],
            w_in=jnp.concatenate([w_in[l][:, :POOL_WIDTH]] + q_cols + [w_in[l][:, POOL_WIDTH + GQA_QW:]],
                                 axis=1).astype(BF16),
            qk_gain=jnp.concatenate([jnp.tile(gqa_qnorm[l] * scale, GQA_HEADS),
                                     jnp.tile(gqa_knorm[l], GQA_KV_HEADS)])[None, :],
            n_gain=jnp.concatenate([jnp.tile(nat_qnorm[l] * scale, NAT_HEADS),
                                    jnp.tile(nat_knorm[l], NAT_HEADS)])[None, :],
            lower=lower[l].reshape(1, 2 * HGRN_W),
            sink=gqa_sink[l].astype(F32)[order] * LOG2E,
            nat_bias=_nat_bias_table(nat_rpb[l], T // GRID_W),
            onorm=jnp.tile(hgrn_onorm[l], HGRN_HEADS)[None, :],
            pool_w=_block_diag(pool_w[l]).astype(BF16), pool_scale=pool_scale[l][None, :],
            w_bp=w_br_pool[l].astype(BF16),
            w_bg=jnp.concatenate([w_br_gqa[l][h * HEAD_DIM:(h + 1) * HEAD_DIM] for h in GQA_HEAD_ORDER],
                                 axis=0).astype(BF16),
            w_bh=w_br_hgrn[l].astype(BF16), w_bn=w_br_nat[l].astype(BF16), w_o=w_o[l].astype(BF16),
            g2=norm2_g[l][None, :], w_up=w_up[l].astype(BF16), w_down=w_down[l].astype(BF16)))
    return dict(cos=cos, sin=sin, grp=grp, layers=layers)


def _trunk(x, prep):
    Bt, T, D = x.shape
    N = Bt * T
    grp = prep["grp"]
    r3 = lambda a: a.reshape(Bt, T, a.shape[-1])
    for p in prep["layers"]:
        (up, gq, gk, gv, lf, hk, hv, hq, hg, nq, nk, nv, gates) = _inproj(
            x.reshape(N, D), p["g1"], p["w_in"], prep["cos"], prep["sin"], p["qk_gain"], p["n_gain"], p["lower"], grp, T)
        ob = _gqa(r3(gq), r3(gk), r3(gv), p["sink"])
        od = _nat(r3(nq), r3(nk), r3(nv), p["nat_bias"])
        h_f, h_b = _hgrn(r3(lf), r3(hk), r3(hv), r3(hq))
        x = _merge(x, r3(up), ob, h_f, h_b, r3(hg), p["onorm"], grp, od, r3(gates),
                   p["pool_w"], p["pool_scale"],
                   p["w_bp"], p["w_bg"], p["w_bh"], p["w_bn"], p["w_o"])
        x = _mlp(x.reshape(N, D), p["g2"], p["w_up"], p["w_down"]).reshape(Bt, T, D)
    return x


def kernel(x_prompt, x_sample, norm1_g, w_in, pool_w, pool_scale, gqa_qnorm, gqa_knorm, gqa_sink, hgrn_lb,
           hgrn_onorm, nat_qnorm, nat_knorm, nat_rpb, w_br_pool, w_br_gqa, w_br_hgrn, w_br_nat, w_o, norm2_g,
           w_up, w_down):
    assert x_prompt.shape[1] == x_sample.shape[1]
    prep = _prepare(x_prompt.shape[1], norm1_g, w_in, pool_w, pool_scale, gqa_qnorm, gqa_knorm, gqa_sink, hgrn_lb,
                    hgrn_onorm, nat_qnorm, nat_knorm, nat_rpb, w_br_pool, w_br_gqa, w_br_hgrn, w_br_nat, w_o,
                    norm2_g, w_up, w_down)
    return (_trunk(x_prompt, prep), _trunk(x_sample, prep))
```

```python
import functools

import numpy as np

import jax
import jax.numpy as jnp
from jax import lax
from jax.experimental import pallas as pl
from jax.experimental.pallas import tpu as pltpu

F32 = jnp.float32
BF16 = jnp.bfloat16

D_MODEL = 1024
GRID_W = 64
EPS = 1e-6
NEG = -1e30
TINY = 1e-30
HEAD_DIM = 64
ROPE_THETA = 10000.0
LOG2E = 1.4426950408889634
N_BRANCH = 4
D_FF = 4 * D_MODEL

POOL_WIDTH = 256
POOL_WINDOWS = (2, 4, 8, 16)
POOL_GW = 64
POOL_HALO = 16
POOL_PAD = 8

GQA_HEADS = 8
GQA_KV_HEADS = 2
GQA_GROUP = GQA_HEADS // GQA_KV_HEADS
GQA_WINDOW = 128
GQA_QW = GQA_HEADS * HEAD_DIM
GQA_KVW = GQA_KV_HEADS * HEAD_DIM

HGRN_HEADS = 4
HGRN_W = HGRN_HEADS * HEAD_DIM

NAT_HEADS = 4
NAT_KR = 8
NAT_KC = 16
NAT_W = NAT_HEADS * HEAD_DIM
NAT_QR = 4
NAT_KROWS = 12
NAT_SUB = 16
NAT_QCB = 16
NAT_KCB = NAT_QCB + NAT_KC
GQA_HEAD_ORDER = tuple((j % 2) * GQA_GROUP + j // 2 for j in range(GQA_HEADS))

_IN_GROUPS = (POOL_WIDTH, GQA_QW + GQA_KVW, GQA_KVW, 2 * HGRN_W, HGRN_W, HGRN_W, HGRN_W, 2 * NAT_W, NAT_W) \
    + (D_MODEL,) * N_BRANCH
_IN_EDGES = tuple(int(e) for e in np.cumsum((0,) + _IN_GROUPS))
IN_WIDTH = _IN_EDGES[-1]
IN_BLOCKS = tuple(tuple((a, b) for a, b in zip(_IN_EDGES[:-1], _IN_EDGES[1:]) if a // D_MODEL == blk)
                  for blk in range(IN_WIDTH // D_MODEL))

TM_IN = 512
TQ_GQA = 128
GQA_SUB = 8
TT_HGRN = 128
HGRN_BB = 4
HGRN_CHUNK = 32
HGRN_SAFE = 86.0
TM_MERGE = 1024
MERGE_SUB = 2
TM_MLP = 1024
MLP_SUB = 2
V7X_VMEM_BYTES = 64 * 1024 * 1024
V7X_MXU_DIM = 256
VMEM_LIMIT = V7X_VMEM_BYTES * 7 // 8


def _sigmoid(x):
    return 0.5 * jnp.tanh(0.5 * x) + 0.5


def _dot(a, b):
    return jnp.dot(a, b, preferred_element_type=F32)


def _dot_nt(a, b):
    return lax.dot_general(a, b, (((1,), (1,)), ((), ())), preferred_element_type=F32)


def _dot_tn(a, b):
    return lax.dot_general(a, b, (((0,), (0,)), ((), ())), preferred_element_type=F32)


def _const_spec(shape):
    n = len(shape)
    return pl.BlockSpec(shape, lambda *_: (0,) * n, pipeline_mode=pl.Buffered(1))


def _inproj_kernel(x_ref, g_ref, w_ref, cos_ref, sin_ref, qkg_ref, ng_ref, lb_ref, grp_ref,
                   up_ref, gq_ref, gk_ref, gv_ref, lf_ref, hk_ref, hv_ref, hq_ref, hg_ref,
                   nq_ref, nk_ref, nv_ref, gate_ref):
    x = x_ref[...]
    ms = jnp.mean(x * x, axis=-1, keepdims=True)
    h = (x * lax.rsqrt(ms + EPS) * g_ref[...]).astype(BF16)

    def headnorm(y, gain):
        y2 = (y * y).astype(BF16)
        tile = grp_ref.shape[0]
        msq = jnp.concatenate([_dot(y2[:, a:min(a + tile, y.shape[1])], grp_ref[0:min(tile, y.shape[1] - a),
                                                                                0:min(tile, y.shape[1] - a)])
                               for a in range(0, y.shape[1], tile)], axis=1)
        return y * lax.rsqrt(msq + EPS) * gain

    def cols(block):
        z = _dot(h, w_ref[:, block * D_MODEL:(block + 1) * D_MODEL])
        return [z[:, a - block * D_MODEL:b - block * D_MODEL] for a, b in IN_BLOCKS[block]]

    zp, zqk, zv = cols(0)
    up_ref[...] = zp.astype(BF16)
    qk = headnorm(zqk, qkg_ref[...])
    qkw = GQA_QW + GQA_KVW
    reps = qkw // cos_ref.shape[1]
    cos = jnp.concatenate([cos_ref[...]] * reps, axis=1)
    sin = jnp.concatenate([sin_ref[...]] * reps, axis=1)
    first = (lax.broadcasted_iota(jnp.int32, (1, qkw), 1) % HEAD_DIM) < (HEAD_DIM // 2)
    partner = jnp.where(first, pltpu.roll(qk, qkw - HEAD_DIM // 2, 1), pltpu.roll(qk, HEAD_DIM // 2, 1))
    qk = qk * cos + partner * sin
    gq_ref[...] = qk[:, 0:GQA_QW].astype(BF16)
    gk_ref[...] = qk[:, GQA_QW:qkw].astype(BF16)
    gv_ref[...] = zv.astype(BF16)

    zf, zi, zq = cols(1)
    lb = lb_ref[...]
    f = lb + (1.0 - lb) / (1.0 + jnp.exp(-zf))
    lf_ref[...] = jnp.log2(jnp.maximum(f, TINY))
    hk_ref[...] = (1.0 - f).astype(BF16)
    hv_ref[...] = zi.astype(BF16)
    hq_ref[...] = (zq * _sigmoid(zq)).astype(BF16)

    zg, znqk, znv = cols(2)
    hg_ref[...] = (zg * _sigmoid(zg)).astype(BF16)
    nqk = headnorm(znqk, ng_ref[...])
    nq_ref[...] = nqk[:, 0:NAT_W].astype(BF16)
    nk_ref[...] = nqk[:, NAT_W:2 * NAT_W].astype(BF16)
    nv_ref[...] = znv.astype(BF16)

    for c in range(N_BRANCH):
        (z,) = cols(3 + c)
        gate_ref[:, c * D_MODEL:(c + 1) * D_MODEL] = _sigmoid(z).astype(BF16)


def _inproj(x2, g1, w_in, cos, sin, qk_gain, n_gain, lb, grp, T):
    N = x2.shape[0]
    tm = TM_IN
    tpb = T // tm
    tok = lambda w: pl.BlockSpec((tm, w), lambda i: (i, 0))
    widths = [(POOL_WIDTH, BF16), (GQA_QW, BF16), (GQA_KVW, BF16), (GQA_KVW, BF16), (2 * HGRN_W, F32),
              (2 * HGRN_W, BF16), (HGRN_W, BF16), (HGRN_W, BF16), (HGRN_W, BF16), (NAT_W, BF16), (NAT_W, BF16),
              (NAT_W, BF16), (N_BRANCH * D_MODEL, BF16)]
    return pl.pallas_call(
        _inproj_kernel,
        grid=(N // tm,),
        in_specs=[
            tok(D_MODEL),
            _const_spec((1, D_MODEL)),
            _const_spec((D_MODEL, IN_WIDTH)),
            pl.BlockSpec((tm, cos.shape[1]), lambda i: (i % tpb, 0)),
            pl.BlockSpec((tm, sin.shape[1]), lambda i: (i % tpb, 0)),
            _const_spec((1, GQA_QW + GQA_KVW)),
            _const_spec((1, 2 * NAT_W)),
            _const_spec((1, 2 * HGRN_W)),
            _const_spec(grp.shape),
        ],
        out_specs=[tok(w) for w, _ in widths],
        out_shape=[jax.ShapeDtypeStruct((N, w), dt) for w, dt in widths],
        compiler_params=pltpu.CompilerParams(dimension_semantics=("parallel",), vmem_limit_bytes=VMEM_LIMIT),
        name="inproj",
    )(x2, g1, w_in, cos, sin, qk_gain, n_gain, lb, grp)


def _gqa_kernel(sink_ref, q_ref, k_ref, v_ref, o_ref, *, tq, T):
    for sub in range(GQA_SUB):
        t0 = (pl.program_id(1) * GQA_SUB + sub) * tq
        o_ref[0, sub * tq:(sub + 1) * tq, :] = _gqa_tile(sink_ref, q_ref[0, sub * tq:(sub + 1) * tq, :], k_ref, v_ref,
                                                        t0, tq, T)


def _gqa_tile(sink_ref, q, k_ref, v_ref, t0, tq, T):
    win = tq + 2 * GQA_WINDOW
    w0 = pl.multiple_of(jnp.clip(t0 - GQA_WINDOW, 0, T - win), GQA_WINDOW)
    k = k_ref[0, pl.ds(w0, win), :]
    v = v_ref[0, pl.ds(w0, win), :]
    qpos = t0 + lax.broadcasted_iota(jnp.int32, (tq, 1), 0)
    kpos = w0 + lax.broadcasted_iota(jnp.int32, (1, win), 1)
    band = jnp.where(jnp.abs(kpos - qpos) <= GQA_WINDOW, 0.0, NEG)
    pw = 2 * HEAD_DIM
    lane = lax.broadcasted_iota(jnp.int32, (1, pw), 1)
    half = [jnp.where(lane < HEAD_DIM, 1.0, 0.0).astype(BF16), jnp.where(lane < HEAD_DIM, 0.0, 1.0).astype(BF16)]
    qs = jnp.concatenate([q[:, (j // 2) * pw:(j // 2 + 1) * pw] * half[j % 2] for j in range(GQA_HEADS)], axis=0)
    s_all = _dot_nt(qs, k)
    ps, invs = [], []
    for j in range(GQA_HEADS):
        s = s_all[j * tq:(j + 1) * tq] + band
        sink = sink_ref[j]
        m = jnp.maximum(jnp.max(s, axis=-1, keepdims=True), sink)
        p = jnp.exp2(s - m)
        invs.append(1.0 / (jnp.sum(p, axis=-1, keepdims=True) + jnp.exp2(sink - m)))
        ps.append(p.astype(BF16))
    o_all = _dot(jnp.concatenate(ps, axis=0), v)
    o_heads = [o_all[j * tq:(j + 1) * tq] * invs[j] for j in range(GQA_HEADS)]
    outs = [jnp.where(lane < HEAD_DIM, o_heads[2 * pair], o_heads[2 * pair + 1]) for pair in range(GQA_HEADS // 2)]
    return jnp.concatenate(outs, axis=1).astype(BF16)


def _gqa(q, k, v, sink):
    Bt, T, _ = q.shape
    tq = TQ_GQA
    step = tq * GQA_SUB
    return pl.pallas_call(
        functools.partial(_gqa_kernel, tq=tq, T=T),
        grid=(Bt, T // step),
        in_specs=[
            pl.BlockSpec(memory_space=pltpu.SMEM),
            pl.BlockSpec((1, step, GQA_QW), lambda b, i: (b, i, 0)),
            pl.BlockSpec((1, T, GQA_KVW), lambda b, i: (b, 0, 0)),
            pl.BlockSpec((1, T, GQA_KVW), lambda b, i: (b, 0, 0)),
        ],
        out_specs=pl.BlockSpec((1, step, GQA_QW), lambda b, i: (b, i, 0)),
        out_shape=jax.ShapeDtypeStruct((Bt, T, GQA_QW), BF16),
        compiler_params=pltpu.CompilerParams(dimension_semantics=("parallel", "arbitrary"),
                                             vmem_limit_bytes=VMEM_LIMIT),
        name="gqa",
    )(sink, q, k, v)


def _nat_kernel(q_ref, k_ref, v_ref, bias_ref, o_ref, *, nblk):
    nq, nk = NAT_QR * GRID_W, NAT_KROWS * GRID_W
    lane_head = lax.broadcasted_iota(jnp.int32, (1, NAT_W), 1) // HEAD_DIM
    head_sel = [jnp.where(lane_head == h, 1.0, 0.0).astype(BF16) for h in range(NAT_HEADS)]
    nqc = NAT_QR * NAT_QCB
    for sub in range(NAT_SUB):
        blk = pl.program_id(1) * NAT_SUB + sub
        first = jnp.clip(blk - 1, 0, nblk - NAT_KROWS // NAT_QR)
        variant = jnp.where(blk == 0, 0, jnp.where(blk == nblk - 1, 2, 1))
        off = pl.multiple_of(first * nq, nq)
        k = k_ref[0, pl.ds(off, nk), :]
        v = v_ref[0, pl.ds(off, nk), :]
        for cb in range(GRID_W // NAT_QCB):
            c0 = min(max(cb * NAT_QCB - NAT_KC // 2, 0), GRID_W - NAT_KCB)
            qrows = [slice(sub * nq + ri * GRID_W + cb * NAT_QCB, sub * nq + ri * GRID_W + (cb + 1) * NAT_QCB)
                     for ri in range(NAT_QR)]
            q = jnp.concatenate([q_ref[0, r, :] for r in qrows], axis=0)
            kc = jnp.concatenate([k[j * GRID_W + c0:j * GRID_W + c0 + NAT_KCB] for j in range(NAT_KROWS)], axis=0)
            vc = jnp.concatenate([v[j * GRID_W + c0:j * GRID_W + c0 + NAT_KCB] for j in range(NAT_KROWS)], axis=0)
            s = _dot_nt(jnp.concatenate([q * m for m in head_sel], axis=0), kc) + bias_ref[variant, cb]
            m = jnp.max(s, axis=-1, keepdims=True)
            p = jnp.exp2(s - m)
            inv = 1.0 / jnp.sum(p, axis=-1, keepdims=True)
            o_all = _dot(p.astype(BF16), vc) * inv
            out = o_all[0:nqc]
            for h in range(1, NAT_HEADS):
                out = jnp.where(lane_head == h, o_all[h * nqc:(h + 1) * nqc], out)
            for ri, r in enumerate(qrows):
                o_ref[0, r, :] = out[ri * NAT_QCB:(ri + 1) * NAT_QCB].astype(BF16)


def _nat(q, k, v, bias):
    Bt, T, _ = q.shape
    rows = T // GRID_W
    nblk = rows // NAT_QR
    assert rows % (NAT_QR * NAT_SUB) == 0 and rows >= NAT_KROWS
    nq = NAT_QR * GRID_W
    return pl.pallas_call(
        functools.partial(_nat_kernel, nblk=nblk),
        grid=(Bt, nblk // NAT_SUB),
        in_specs=[
            pl.BlockSpec((1, NAT_SUB * nq, NAT_W), lambda b, r: (b, r, 0)),
            pl.BlockSpec((1, T, NAT_W), lambda b, r: (b, 0, 0)),
            pl.BlockSpec((1, T, NAT_W), lambda b, r: (b, 0, 0)),
            _const_spec(bias.shape),
        ],
        out_specs=pl.BlockSpec((1, NAT_SUB * nq, NAT_W), lambda b, r: (b, r, 0)),
        out_shape=jax.ShapeDtypeStruct((Bt, T, NAT_W), BF16),
        compiler_params=pltpu.CompilerParams(dimension_semantics=("parallel", "arbitrary"),
                                             vmem_limit_bytes=VMEM_LIMIT),
        name="nat",
    )(q, k, v, bias)


def _nat_bias_table(rpb, rows):
    nblk = rows // NAT_QR
    ncb = GRID_W // NAT_QCB
    cb = np.arange(ncb)[:, None, None]
    c = cb * NAT_QCB + np.arange(NAT_QCB)[None, :, None]
    kc = np.clip(cb * NAT_QCB - NAT_KC // 2, 0, GRID_W - NAT_KCB) + np.arange(NAT_KCB)[None, None, :]
    qc0 = np.clip(c - NAT_KC // 2, 0, GRID_W - NAT_KC)
    col_ok = (kc >= qc0) & (kc < qc0 + NAT_KC)
    dcol = np.clip(kc - c + (NAT_KC - 1), 0, 2 * NAT_KC - 2)
    oh_col = (dcol[None] == np.arange(2 * NAT_KC - 1)[:, None, None, None]) & col_ok[None]
    oh_row = np.zeros((3, NAT_QR, NAT_KROWS, 2 * NAT_KR - 1), np.float32)
    for var, blk in enumerate((0, 1, nblk - 1)):
        first = int(np.clip(blk - 1, 0, nblk - NAT_KROWS // NAT_QR))
        for ri in range(NAT_QR):
            r = blk * NAT_QR + ri
            r0 = int(np.clip(r - NAT_KR // 2, 0, rows - NAT_KR))
            for j in range(NAT_KROWS):
                kr = first * NAT_QR + j
                if r0 <= kr < r0 + NAT_KR:
                    oh_row[var, ri, j, kr - r + NAT_KR - 1] = 1.0
    oh_row = np.concatenate([oh_row, 1.0 - oh_row.sum(-1, keepdims=True)], axis=-1)
    hp = lax.Precision.HIGHEST
    by_col = jnp.einsum('hde,ebck->hdbck', rpb.astype(F32), jnp.asarray(oh_col, F32), precision=hp)
    by_col = jnp.where(jnp.asarray(col_ok)[None, None], by_col * LOG2E, NEG)
    by_col = jnp.concatenate([by_col, jnp.full_like(by_col[:, :1], NEG)], axis=1)
    bias = jnp.einsum('vrjd,hdbck->vbhrcjk', jnp.asarray(oh_row), by_col, precision=hp)
    return bias.reshape(3, ncb, NAT_HEADS * NAT_QR * NAT_QCB, NAT_KROWS * NAT_KCB)


def _hgrn_consts(Tt, C, rev):
    nc = Tt // C
    t = np.arange(Tt)
    row, col = t[:, None], t[None, :]
    same = ((row // C) == (col // C)) & ((col >= row) if rev else (col <= row))
    at = [(nc - 1 - p) if rev else p for p in range(nc)]
    pair = np.concatenate([np.broadcast_to((t // C) == at[p - 1], (C, Tt)) for p in (1, 3)])
    masks = np.concatenate([np.tile(same, (HGRN_HEADS, 1)), np.tile(pair, (HGRN_HEADS, 1))]).astype(np.float32)
    hl = np.arange(HGRN_W) // HEAD_DIM
    bd = (hl[:, None] == hl[None, :]).astype(np.float32)
    return jnp.asarray(same.astype(np.float32), BF16), jnp.asarray(masks), jnp.asarray(bd)


def _hgrn_kernel(lf_f, k_f, v_f, q_f, lf_b, k_b, v_b, q_b, tri_f, mask_f, tri_b, mask_b, bd_ref,
                 of_ref, ob_ref, st_f, st_b, *, Tt, C):
    @pl.when(pl.program_id(1) == 0)
    def _():
        st_f[...] = jnp.zeros_like(st_f)
        st_b[...] = jnp.zeros_like(st_b)

    tiles = []
    for bi in range(HGRN_BB):
        tiles.append((of_ref, bi) + _hgrn_tile(lf_f.at[bi], k_f.at[bi], v_f.at[bi], q_f.at[bi], tri_f, mask_f, bd_ref,
                                               st_f.at[bi], rev=False, Tt=Tt, C=C))
        tiles.append((ob_ref, bi) + _hgrn_tile(lf_b.at[bi], k_b.at[bi], v_b.at[bi], q_b.at[bi], tri_b, mask_b, bd_ref,
                                               st_b.at[bi], rev=True, Tt=Tt, C=C))
    for o_ref, bi, o, _, _ in tiles:
        o_ref[bi] = o.astype(BF16)

    @pl.when(functools.reduce(jnp.minimum, [t[3] for t in tiles]) < -HGRN_SAFE)
    def _():
        for o_ref, bi, _, _, redo in tiles:
            o_ref[bi] = redo().astype(BF16)


def _hgrn_tile(lf_ref, k_ref, v_ref, q_ref, tri_ref, mask_ref, bd_ref, st_ref, *, rev, Tt, C):
    nc = Tt // C
    lf = lf_ref[...]
    lane_head = lax.broadcasted_iota(jnp.int32, (1, HGRN_W), 1) // HEAD_DIM
    head_sel = [jnp.where(lane_head == h, 1.0, 0.0).astype(BF16) for h in range(HGRN_HEADS)]

    def stack_heads(x16):
        return jnp.concatenate([x16 * m for m in head_sel], axis=0)

    tri = tri_ref[...]
    hi = lf.astype(BF16)
    lo = (lf - hi.astype(F32)).astype(BF16)
    bc = _dot(tri, hi) + _dot(tri, lo)

    cs = [slice(c * C, (c + 1) * C) for c in range(nc)]
    pos = [(nc - 1 - c) if rev else c for c in range(nc)]
    at = [pos.index(p) for p in range(nc)]
    last = 0 if rev else C - 1
    tot = [bc[c * C + last:c * C + last + 1] for c in range(nc)]
    zero = jnp.zeros((C, HGRN_W), F32)
    cat = lambda xs: jnp.concatenate(xs, axis=0).astype(BF16)

    def intra_fast():
        ka = (k * jnp.exp2(-bc)).astype(BF16)
        return jnp.where(mask_ref[0:HGRN_HEADS * Tt, :] > 0.5, _dot_nt(stack_heads(cat(qa)), ka), 0.0)

    def intra_safe():
        q = q_ref[...].astype(F32)
        k = k_ref[...].astype(F32)
        t_idx = lax.broadcasted_iota(jnp.int32, (Tt, 1), 0)
        tau = (Tt - 1 - t_idx) if rev else t_idx
        row = lax.broadcasted_iota(jnp.int32, (HGRN_HEADS * Tt, Tt), 0) % Tt
        col = lax.broadcasted_iota(jnp.int32, (HGRN_HEADS * Tt, Tt), 1)
        acc = jnp.where(row == col, _dot_nt(stack_heads(q_ref[...]), k_ref[...]), 0.0)
        before = bc - lf
        end = bc
        size = 2
        while size <= C:
            half = size // 2
            upper = (tau % size) >= half
            w = jnp.exp2(jnp.where(upper, bc - before, end - bc))
            qt = jnp.where(upper, q * w, 0.0).astype(BF16)
            kt = jnp.where(upper, 0.0, k * w).astype(BF16)
            acc = acc + jnp.where((row // size) == (col // size), _dot_nt(stack_heads(qt), kt), 0.0)
            if size < C:
                back, fwd = (Tt - half, half) if rev else (half, Tt - half)
                before = jnp.where(upper, pltpu.roll(before, back, 0), before)
                end = jnp.where(upper, end, pltpu.roll(end, fwd, 0))
            size *= 2
        return acc

    q = q_ref[...].astype(F32)
    k = k_ref[...].astype(F32)
    v = v_ref[...]
    qa = [q[cs[c]] * jnp.exp2(bc[cs[c]]) for c in range(nc)]
    kb = [k[cs[c]] * jnp.exp2(tot[c] - bc[cs[c]]) for c in range(nc)]
    q_pair = cat([qa[at[1]], qa[at[3]]])
    k_pair = cat([kb[c] if pos[c] % 2 == 0 else zero for c in range(nc)])
    a_pair = _dot_nt(stack_heads(q_pair), k_pair) * mask_ref[HGRN_HEADS * Tt:, :]
    q_half = cat([qa[at[2]], qa[at[3]] * jnp.exp2(tot[at[2]])])
    k_half = cat([zero if pos[c] >= 2 else (kb[c] * jnp.exp2(tot[at[1]]) if pos[c] == 0 else kb[c]) for c in range(nc)])
    a_half = _dot_nt(stack_heads(q_half), k_half)

    def add_cross(a_intra):
        blocks = []
        for h in range(HGRN_HEADS):
            for c in range(nc):
                blk = a_intra[h * Tt + c * C:h * Tt + (c + 1) * C]
                if pos[c] % 2 == 1:
                    r0 = h * 2 * C + (pos[c] // 2) * C
                    blk = blk + a_pair[r0:r0 + C]
                if pos[c] >= 2:
                    r0 = h * 2 * C + (pos[c] - 2) * C
                    blk = blk + a_half[r0:r0 + C]
                blocks.append(blk)
        return jnp.concatenate(blocks, axis=0)

    pre, suf = [None] * nc, [None] * nc
    run = jnp.zeros_like(tot[0])
    for p in range(nc):
        pre[at[p]] = run
        run = run + tot[at[p]]
    b_tile = run
    run = jnp.zeros_like(tot[0])
    for p in reversed(range(nc)):
        suf[at[p]] = run
        run = run + tot[at[p]]
    q_in = cat([qa[c] * jnp.exp2(pre[c]) for c in range(nc)])
    k_out = cat([kb[c] * jnp.exp2(suf[c]) for c in range(nc)])

    st = st_ref[...]
    o_inter = _dot_nt(q_in, st.astype(BF16))
    st_ref[...] = st * jnp.exp2(b_tile) + _dot_tn(v, k_out) * bd_ref[...]

    def output(a_intra):
        o_heads = _dot(add_cross(a_intra).astype(BF16), v)
        o = o_inter
        for h in range(HGRN_HEADS):
            o = o + jnp.where(lane_head == h, o_heads[h * Tt:(h + 1) * Tt], 0.0)
        return o

    return output(intra_fast()), jnp.min(functools.reduce(jnp.minimum, tot)), lambda: output(intra_safe())


def _hgrn(lf, hk, hv, hq):
    Bt, T, _ = hv.shape
    Tt, C = TT_HGRN, HGRN_CHUNK
    bb = HGRN_BB
    assert Tt == 4 * C and Bt % bb == 0
    nt = T // Tt
    fwd = lambda col: pl.BlockSpec((bb, Tt, HGRN_W), lambda b, n: (b, n, col))
    bwd = lambda col: pl.BlockSpec((bb, Tt, HGRN_W), lambda b, n: (b, nt - 1 - n, col))
    tri_f, mask_f, bd = _hgrn_consts(Tt, C, False)
    tri_b, mask_b, _ = _hgrn_consts(Tt, C, True)
    tri_spec, mask_spec = _const_spec((Tt, Tt)), _const_spec((HGRN_HEADS * (Tt + 2 * C), Tt))
    out = jax.ShapeDtypeStruct((Bt, T, HGRN_W), BF16)
    return pl.pallas_call(
        functools.partial(_hgrn_kernel, Tt=Tt, C=C),
        grid=(Bt // bb, nt),
        in_specs=[fwd(0), fwd(0), fwd(0), fwd(0), bwd(1), bwd(1), bwd(0), bwd(0),
                  tri_spec, mask_spec, tri_spec, mask_spec, _const_spec((HGRN_W, HGRN_W))],
        out_specs=[fwd(0), bwd(0)],
        out_shape=[out, out],
        scratch_shapes=[pltpu.VMEM((bb, HGRN_W, HGRN_W), F32), pltpu.VMEM((bb, HGRN_W, HGRN_W), F32)],
        compiler_params=pltpu.CompilerParams(dimension_semantics=("parallel", "arbitrary"),
                                             vmem_limit_bytes=VMEM_LIMIT),
        name="hgrn",
    )(lf, hk, hv, hq, lf, hk, hv, hq, tri_f, mask_f, tri_b, mask_b, bd)


def _merge_kernel(x_ref, u_ref, ob_ref, hf_ref, hb_ref, hg_ref, hn_ref, grp_ref, od_ref, g_ref, pw_ref, ps_ref,
                  wbp_ref, wbg_ref, wbh_ref, wbn_ref, wo_ref, o_ref, pool_ref, *, tm, T):
    i = pl.program_id(1)
    t0 = pl.multiple_of(i * tm, tm)
    win = tm + 2 * POOL_HALO
    w0 = pl.multiple_of(jnp.clip(t0 - POOL_HALO, 0, T - win), POOL_HALO)
    pad = jnp.zeros((POOL_PAD, POOL_WIDTH), F32)
    run = jnp.concatenate([pad, u_ref[0, pl.ds(w0, win), :].astype(F32), pad], axis=0)
    rows = win + 2 * POOL_PAD
    group = lax.broadcasted_iota(jnp.int32, (1, POOL_WIDTH), 1) // POOL_GW
    sums, width = None, 1
    for g, w in enumerate(POOL_WINDOWS):
        while width < w:
            run = run + pltpu.roll(run, width, 0)
            width *= 2
        centred = run if w == 2 else pltpu.roll(run, rows - (w // 2 - 1), 0)
        sums = centred if sums is None else jnp.where(group == g, centred, sums)
    pool_ref[...] = sums
    half_w = functools.reduce(lambda acc, gw: jnp.where(group == gw[0], gw[1] // 2, acc),
                              list(enumerate(POOL_WINDOWS))[1:], POOL_WINDOWS[0] // 2)
    tg = t0 + lax.broadcasted_iota(jnp.int32, (tm, POOL_WIDTH), 0)
    cnt = (jnp.minimum(tg + half_w, T) - jnp.maximum(tg - half_w, 0)).astype(F32)
    uc = u_ref[0, pl.ds(t0, tm), :].astype(F32)
    mixed = pool_ref[pl.ds(pl.multiple_of(POOL_PAD + t0 - w0, 8), tm), :] / cnt - uc
    sub = tm // MERGE_SUB
    for r in range(MERGE_SUB):
        rs = slice(r * sub, (r + 1) * sub)
        oa = _dot(mixed[rs].astype(BF16), pw_ref[...]) * ps_ref[...]
        hsum = hf_ref[0, rs, :].astype(F32) + hb_ref[0, rs, :].astype(F32)
        msq = _dot((hsum * hsum).astype(BF16), grp_ref[...])
        oc = hsum * lax.rsqrt(msq + EPS) * hn_ref[...] * hg_ref[0, rs, :].astype(F32)
        gate = lambda c: g_ref[0, rs, c * D_MODEL:(c + 1) * D_MODEL].astype(F32)
        merged = (gate(0) * _dot(oa.astype(BF16), wbp_ref[...])
                  + gate(1) * _dot(ob_ref[0, rs, :], wbg_ref[...])
                  + gate(2) * _dot(oc.astype(BF16), wbh_ref[...])
                  + gate(3) * _dot(od_ref[0, rs, :], wbn_ref[...]))
        o_ref[0, rs, :] = x_ref[0, rs, :] + _dot(merged.astype(BF16), wo_ref[...])


def _merge(x, u, ob, hf, hb, hg, hn, grp, od, gates, pw, ps, wbp, wbg, wbh, wbn, wo):
    Bt, T, _ = x.shape
    tm = TM_MERGE
    tok = lambda w: pl.BlockSpec((1, tm, w), lambda b, i: (b, i, 0))
    return pl.pallas_call(
        functools.partial(_merge_kernel, tm=tm, T=T),
        grid=(Bt, T // tm),
        in_specs=[
            tok(D_MODEL),
            pl.BlockSpec((1, T, POOL_WIDTH), lambda b, i: (b, 0, 0)),
            tok(GQA_QW), tok(HGRN_W), tok(HGRN_W), tok(HGRN_W), _const_spec((1, HGRN_W)),
            _const_spec((HGRN_W, HGRN_W)), tok(NAT_W), tok(N_BRANCH * D_MODEL),
            _const_spec((POOL_WIDTH, POOL_WIDTH)), _const_spec((1, POOL_WIDTH)),
            _const_spec((POOL_WIDTH, D_MODEL)), _const_spec((GQA_QW, D_MODEL)),
            _const_spec((HGRN_W, D_MODEL)), _const_spec((NAT_W, D_MODEL)),
            _const_spec((D_MODEL, D_MODEL)),
        ],
        out_specs=tok(D_MODEL),
        out_shape=jax.ShapeDtypeStruct((Bt, T, D_MODEL), F32),
        scratch_shapes=[pltpu.VMEM((tm + 2 * POOL_HALO + 2 * POOL_PAD, POOL_WIDTH), F32)],
        compiler_params=pltpu.CompilerParams(dimension_semantics=("parallel", "arbitrary"),
                                             vmem_limit_bytes=VMEM_LIMIT),
        name="merge",
    )(x, u, ob, hf, hb, hg, hn, grp, od, gates, pw, ps, wbp, wbg, wbh, wbn, wo)


def _mlp_kernel(x_ref, g_ref, wu_ref, wd_ref, o_ref):
    sub = x_ref.shape[0] // MLP_SUB
    for r in range(MLP_SUB):
        rs = slice(r * sub, (r + 1) * sub)
        x = x_ref[rs, :]
        ms = jnp.mean(x * x, axis=-1, keepdims=True)
        h = (x * lax.rsqrt(ms + EPS) * g_ref[...]).astype(BF16)
        acc = x
        for c in range(D_FF // D_MODEL):
            sl = slice(c * D_MODEL, (c + 1) * D_MODEL)
            hid = jnp.maximum(_dot(h, wu_ref[:, sl]), 0.0)
            acc = acc + _dot((hid * hid).astype(BF16), wd_ref[sl, :])
        o_ref[rs, :] = acc


def _mlp(x2, g2, wu, wd):
    N = x2.shape[0]
    tm = TM_MLP
    return pl.pallas_call(
        _mlp_kernel,
        grid=(N // tm,),
        in_specs=[pl.BlockSpec((tm, D_MODEL), lambda i: (i, 0)), _const_spec((1, D_MODEL)),
                  _const_spec((D_MODEL, D_FF)), _const_spec((D_FF, D_MODEL))],
        out_specs=pl.BlockSpec((tm, D_MODEL), lambda i: (i, 0)),
        out_shape=jax.ShapeDtypeStruct((N, D_MODEL), F32),
        compiler_params=pltpu.CompilerParams(dimension_semantics=("parallel",), vmem_limit_bytes=VMEM_LIMIT),
        name="mlp",
    )(x2, g2, wu, wd)


def _block_diag(blocks):
    n, a, b = blocks.shape
    eye = jnp.eye(n, dtype=blocks.dtype)
    return (eye[:, None, :, None] * blocks[:, :, None, :]).reshape(n * a, n * b)


def _prepare(T, norm1_g, w_in, pool_w, pool_scale, gqa_qnorm, gqa_knorm, gqa_sink, hgrn_lb, hgrn_onorm,
             nat_qnorm, nat_knorm, nat_rpb, w_br_pool, w_br_gqa, w_br_hgrn, w_br_nat, w_o, norm2_g, w_up, w_down):
    D = D_MODEL
    half = HEAD_DIM // 2
    inv = ROPE_THETA ** (-jnp.arange(half, dtype=F32) / half)
    ang = jnp.arange(T, dtype=F32)[:, None] * inv[None, :]
    cos = jnp.tile(jnp.concatenate([jnp.cos(ang), jnp.cos(ang)], axis=1), (1, 2))
    sin = jnp.tile(jnp.concatenate([-jnp.sin(ang), jnp.sin(ang)], axis=1), (1, 2))
    sm = jax.nn.softmax(hgrn_lb.astype(F32), axis=0)
    lower = jnp.cumsum(sm, axis=0) - sm[:1]
    grp = _block_diag(jnp.full((V7X_MXU_DIM // HEAD_DIM, HEAD_DIM, HEAD_DIM), 1.0 / HEAD_DIM, F32)).astype(BF16)
    scale = HEAD_DIM ** -0.5 * LOG2E
    order = jnp.asarray(GQA_HEAD_ORDER)
    layers = []
    for l in range(w_in.shape[0]):
        q_cols = [w_in[l][:, POOL_WIDTH + h * HEAD_DIM:POOL_WIDTH + (h + 1) * HEAD_DIM] for h in GQA_HEAD_ORDER]
        layers.append(dict(
            g1=norm1_g[l][None, :],
            w_in=jnp.concatenate([w_in[l][:, :POOL_WIDTH]] + q_cols + [w_in[l][:, POOL_WIDTH + GQA_QW:]],
                                 axis=1).astype(BF16),
            qk_gain=jnp.concatenate([jnp.tile(gqa_qnorm[l] * scale, GQA_HEADS),
                                     jnp.tile(gqa_knorm[l], GQA_KV_HEADS)])[None, :],
            n_gain=jnp.concatenate([jnp.tile(nat_qnorm[l] * scale, NAT_HEADS),
                                    jnp.tile(nat_knorm[l], NAT_HEADS)])[None, :],
            lower=lower[l].reshape(1, 2 * HGRN_W),
            sink=gqa_sink[l].astype(F32)[order] * LOG2E,
            nat_bias=_nat_bias_table(nat_rpb[l], T // GRID_W),
            onorm=jnp.tile(hgrn_onorm[l], HGRN_HEADS)[None, :],
            pool_w=_block_diag(pool_w[l]).astype(BF16), pool_scale=pool_scale[l][None, :],
            w_bp=w_br_pool[l].astype(BF16),
            w_bg=jnp.concatenate([w_br_gqa[l][h * HEAD_DIM:(h + 1) * HEAD_DIM] for h in GQA_HEAD_ORDER],
                                 axis=0).astype(BF16),
            w_bh=w_br_hgrn[l].astype(BF16), w_bn=w_br_nat[l].astype(BF16), w_o=w_o[l].astype(BF16),
            g2=norm2_g[l][None, :], w_up=w_up[l].astype(BF16), w_down=w_down[l].astype(BF16)))
    return dict(cos=cos, sin=sin, grp=grp, layers=layers)


def _trunk(x, prep):
    Bt, T, D = x.shape
    N = Bt * T
    grp = prep["grp"]
    r3 = lambda a: a.reshape(Bt, T, a.shape[-1])
    for p in prep["layers"]:
        (up, gq, gk, gv, lf, hk, hv, hq, hg, nq, nk, nv, gates) = _inproj(
            x.reshape(N, D), p["g1"], p["w_in"], prep["cos"], prep["sin"], p["qk_gain"], p["n_gain"], p["lower"], grp, T)
        ob = _gqa(r3(gq), r3(gk), r3(gv), p["sink"])
        od = _nat(r3(nq), r3(nk), r3(nv), p["nat_bias"])
        h_f, h_b = _hgrn(r3(lf), r3(hk), r3(hv), r3(hq))
        x = _merge(x, r3(up), ob, h_f, h_b, r3(hg), p["onorm"], grp, od, r3(gates),
                   p["pool_w"], p["pool_scale"],
                   p["w_bp"], p["w_bg"], p["w_bh"], p["w_bn"], p["w_o"])
        x = _mlp(x.reshape(N, D), p["g2"], p["w_up"], p["w_down"]).reshape(Bt, T, D)
    return x


def kernel(x_prompt, x_sample, norm1_g, w_in, pool_w, pool_scale, gqa_qnorm, gqa_knorm, gqa_sink, hgrn_lb,
           hgrn_onorm, nat_qnorm, nat_knorm, nat_rpb, w_br_pool, w_br_gqa, w_br_hgrn, w_br_nat, w_o, norm2_g,
           w_up, w_down):
    assert x_prompt.shape[1] == x_sample.shape[1]
    prep = _prepare(x_prompt.shape[1], norm1_g, w_in, pool_w, pool_scale, gqa_qnorm, gqa_knorm, gqa_sink, hgrn_lb,
                    hgrn_onorm, nat_qnorm, nat_knorm, nat_rpb, w_br_pool, w_br_gqa, w_br_hgrn, w_br_nat, w_o,
                    norm2_g, w_up, w_down)
    return (_trunk(x_prompt, prep), _trunk(x_sample, prep))
```
